```python
import math
import jax, jax.numpy as jnp
from jax import lax
import numpy as np

D_MODEL = 2048
BATCH = 2
SEQ = 8192
DEPTH = 1

N_HEADS = 8
HEAD_DIM = 128
ATTN_DIM = N_HEADS * HEAD_DIM
IDX_HEADS = 16
IDX_DIM = 64
TOPK_MAX = 256
Q_BLOCK = 128
NUM_BUCKETS = 32
MAX_DISTANCE = 128
CONV_CH = 1024
CONV_WIDTH = 31
D_FF = 5632
FFN_RES_WEIGHT = 0.5
EPS = 1e-6

IN_SPLITS = (ATTN_DIM,
             ATTN_DIM,
             ATTN_DIM,
             IDX_HEADS * IDX_DIM,
             IDX_DIM,
             IDX_HEADS,
             2 * CONV_CH,
             2 * D_MODEL)
IN_DIM = int(sum(IN_SPLITS))
IN_OFFSETS = tuple(int(o) for o in np.cumsum(IN_SPLITS)[:-1])

kernel_name = "hybrid_dsa_conformer_gated_macaron"


def rmsnorm(x, g):
    xf = x.astype(jnp.float32)
    y = xf * lax.rsqrt(jnp.mean(xf * xf, axis=-1, keepdims=True) + EPS)
    return (y * g.astype(jnp.float32)).astype(x.dtype)


def layernorm(x, g, b):
    xf = x.astype(jnp.float32)
    mu = jnp.mean(xf, axis=-1, keepdims=True)
    var = jnp.mean(jnp.square(xf - mu), axis=-1, keepdims=True)
    y = (xf - mu) * lax.rsqrt(var + EPS)
    return (y * g.astype(jnp.float32) + b.astype(jnp.float32)).astype(x.dtype)


def swiglu(x, wg, wu, wd):
    return (jax.nn.silu(x @ wg) * (x @ wu)) @ wd


def t5_bucket(dist):
    max_exact = NUM_BUCKETS // 2
    n = jnp.maximum(dist, 0)
    nf = jnp.maximum(n, max_exact).astype(jnp.float32)
    large = max_exact + (jnp.log(nf / max_exact) / math.log(MAX_DISTANCE / max_exact)
                         * (NUM_BUCKETS - max_exact)).astype(jnp.int32)
    large = jnp.minimum(large, NUM_BUCKETS - 1)
    return jnp.where(n < max_exact, n, large)


def dsa_attention(q, k, v, q_idx, k_idx, w_idx, rel_bias):
    B, S = q.shape[0], q.shape[1]
    topk = min(TOPK_MAX, S // 4)
    nb = S // Q_BLOCK
    s_pos = jnp.arange(S, dtype=jnp.int32)
    k_idx_f = k_idx.astype(jnp.float32)
    bias_tab = rel_bias.astype(jnp.float32)
    scale = HEAD_DIM ** -0.5

    def to_blocks(a):
        return jnp.moveaxis(a.reshape((B, nb, Q_BLOCK) + a.shape[2:]), 1, 0)

    def block_fn(args):
        qb, qib, wb, t0 = args
        t_pos = t0 + jnp.arange(Q_BLOCK, dtype=jnp.int32)
        dots = jnp.einsum('bthd,bsd->bths', qib.astype(jnp.float32), k_idx_f) * (IDX_DIM ** -0.5)
        score = jnp.einsum('bths,bth->bts', jax.nn.relu(dots), wb.astype(jnp.float32))
        causal = s_pos[None, :] <= t_pos[:, None]
        score = jnp.where(causal[None], score, -jnp.inf)
        _, sel = lax.top_k(score, topk)
        valid = sel <= t_pos[None, :, None]
        k_sel = jax.vmap(lambda kk, ii: kk[ii])(k, sel)
        v_sel = jax.vmap(lambda vv, ii: vv[ii])(v, sel)
        logits = jnp.einsum('bthd,btkhd->bthk', qb.astype(jnp.float32),
                            k_sel.astype(jnp.float32)) * scale
        bias = bias_tab[t5_bucket(t_pos[None, :, None] - sel)]
        logits = logits + jnp.moveaxis(bias, 3, 2)
        logits = jnp.where(valid[:, :, None, :], logits, -jnp.inf)
        p = jax.nn.softmax(logits, axis=-1)
        out = jnp.einsum('bthk,btkhd->bthd', p, v_sel.astype(jnp.float32))
        return out.astype(v.dtype)

    starts = jnp.arange(nb, dtype=jnp.int32) * Q_BLOCK
    out = lax.map(block_fn, (to_blocks(q), to_blocks(q_idx), to_blocks(w_idx), starts))
    return jnp.moveaxis(out, 0, 1).reshape(B, S, ATTN_DIM)


def conv_module(glu_in, dw, dw_b, ln_g, ln_b, w_pw2):
    a, gate = jnp.split(glu_in, 2, axis=-1)
    h = a * jax.nn.sigmoid(gate)
    h = lax.conv_general_dilated(h, dw[:, None, :].astype(h.dtype), window_strides=(1,),
                                 padding=[(CONV_WIDTH - 1, 0)],
                                 dimension_numbers=('NWC', 'WIO', 'NWC'),
                                 feature_group_count=CONV_CH) + dw_b
    h = jax.nn.silu(layernorm(h, ln_g, ln_b))
    return h @ w_pw2


def setup_inputs(seed: int = 0) -> dict:
    key = jax.random.key(seed)
    ks = iter(jax.random.split(key, 32))
    f32 = jnp.float32

    def w(shape, fan_in):
        return jax.random.normal(next(ks), (DEPTH,) + shape, f32) * fan_in ** -0.5

    def gain(n):
        return 1.0 + 0.05 * jax.random.normal(next(ks), (DEPTH, n), f32)

    def small(n, s=0.02):
        return s * jax.random.normal(next(ks), (DEPTH, n), f32)

    x = jax.random.normal(next(ks), (BATCH, SEQ, D_MODEL), f32)
    rel_bias = 0.5 * jax.random.normal(next(ks), (NUM_BUCKETS, N_HEADS), f32)
    return {
        "x": x,
        "rel_bias": rel_bias,
        "ffn1_pre_g": gain(D_MODEL),
        "ffn1_wg": w((D_MODEL, D_FF), D_MODEL),
        "ffn1_wu": w((D_MODEL, D_FF), D_MODEL),
        "ffn1_wd": w((D_FF, D_MODEL), D_FF),
        "ffn1_post_g": gain(D_MODEL),
        "mix_pre_g": gain(D_MODEL),
        "w_in": w((D_MODEL, IN_DIM), D_MODEL),
        "b_gate": small(2 * D_MODEL, 0.1),
        "w_o": w((ATTN_DIM, D_MODEL), ATTN_DIM),
        "conv_dw": w((CONV_WIDTH, CONV_CH), CONV_WIDTH),
        "conv_dw_b": small(CONV_CH),
        "conv_ln_g": gain(CONV_CH),
        "conv_ln_b": small(CONV_CH),
        "w_pw2": w((CONV_CH, D_MODEL), CONV_CH),
        "w_out": w((D_MODEL, D_MODEL), D_MODEL),
        "mix_post_g": gain(D_MODEL),
        "ffn2_pre_g": gain(D_MODEL),
        "ffn2_wg": w((D_MODEL, D_FF), D_MODEL),
        "ffn2_wu": w((D_MODEL, D_FF), D_MODEL),
        "ffn2_wd": w((D_FF, D_MODEL), D_FF),
        "ffn2_post_g": gain(D_MODEL),
    }


def reference(x, rel_bias, ffn1_pre_g, ffn1_wg, ffn1_wu, ffn1_wd, ffn1_post_g,
              mix_pre_g, w_in, b_gate, w_o, conv_dw, conv_dw_b, conv_ln_g, conv_ln_b,
              w_pw2, w_out, mix_post_g, ffn2_pre_g, ffn2_wg, ffn2_wu, ffn2_wd, ffn2_post_g):
    B, S, _ = x.shape
    h = x
    for l in range(DEPTH):
        f = swiglu(rmsnorm(h, ffn1_pre_g[l]), ffn1_wg[l], ffn1_wu[l], ffn1_wd[l])
        h = h + FFN_RES_WEIGHT * rmsnorm(f, ffn1_post_g[l])

        u = rmsnorm(h, mix_pre_g[l])
        proj = u @ w_in[l]
        q, k, v, qi, ki, wi, glu_in, gates = jnp.split(proj, IN_OFFSETS, axis=-1)
        q = q.reshape(B, S, N_HEADS, HEAD_DIM)
        k = k.reshape(B, S, N_HEADS, HEAD_DIM)
        v = v.reshape(B, S, N_HEADS, HEAD_DIM)
        qi = qi.reshape(B, S, IDX_HEADS, IDX_DIM)
        wi = wi * (IDX_HEADS ** -0.5)
        attn = dsa_attention(q, k, v, qi, ki, wi, rel_bias) @ w_o[l]
        conv = conv_module(glu_in, conv_dw[l], conv_dw_b[l], conv_ln_g[l],
                           conv_ln_b[l], w_pw2[l])
        g_attn, g_conv = jnp.split(jax.nn.sigmoid(gates + b_gate[l]), 2, axis=-1)
        mixed = (g_attn * attn + g_conv * conv) @ w_out[l]
        h = h + rmsnorm(mixed, mix_post_g[l])

        f = swiglu(rmsnorm(h, ffn2_pre_g[l]), ffn2_wg[l], ffn2_wu[l], ffn2_wd[l])
        h = h + FFN_RES_WEIGHT * rmsnorm(f, ffn2_post_g[l])
    return h
```

```python
import functools
import math

import jax
import jax.numpy as jnp
from jax import lax
from jax.experimental import pallas as pl
from jax.experimental.pallas import tpu as pltpu

D_MODEL = 2048
N_HEADS = 8
HEAD_DIM = 128
ATTN_DIM = N_HEADS * HEAD_DIM
IDX_HEADS = 16
IDX_DIM = 64
TOPK_MAX = 256
NUM_BUCKETS = 32
MAX_DISTANCE = 128
CONV_CH = 1024
CONV_WIDTH = 31
D_FF = 5632
FFN_RES_WEIGHT = 0.5
EPS = 1e-6

F32 = jnp.float32
BF16 = jnp.bfloat16
LANES = 128
V7X_VMEM_BYTES = 64 * 1024 * 1024
MASKED = -1e30
INT_MIN = -(2 ** 31)
LOG2E = math.log2(math.e)

FFN_TM, FFN_TF = 512, 512
PROJ_TM = 1024
IDX_TQ = 128
IDX_TC = 512
ATT_T = 256
CONV_TS = 512
CONV_HALO = 32
CONV_RB, CONV_CB = 128, 256
OUT_TM = 256


def _vmem_limit(nbytes):
    return int(min(nbytes + (8 << 20), V7X_VMEM_BYTES - (4 << 20)))


def _rms(x, g):
    y = x * lax.rsqrt(jnp.mean(x * x, axis=-1, keepdims=True) + EPS)
    return y * g


def _ffn_body(emit_next, x_ref, pre_g_ref, wg_ref, wu_ref, wd_ref, post_g_ref, *rest):
    if emit_next:
        next_g_ref, out_ref, u_ref, xn_ref, acc_ref = rest
    else:
        out_ref, xn_ref, acc_ref = rest
    j = pl.program_id(1)

    @pl.when(j == 0)
    def _():
        xn_ref[...] = _rms(x_ref[...], pre_g_ref[...]).astype(BF16)
        acc_ref[...] = jnp.zeros_like(acc_ref)

    xn = xn_ref[...]
    g = jnp.dot(xn, wg_ref[...], preferred_element_type=F32)
    u = jnp.dot(xn, wu_ref[...], preferred_element_type=F32)
    a = (g * jax.nn.sigmoid(g) * u).astype(BF16)
    acc_ref[...] += jnp.dot(a, wd_ref[...], preferred_element_type=F32)

    @pl.when(j == pl.num_programs(1) - 1)
    def _():
        h = x_ref[...] + FFN_RES_WEIGHT * _rms(acc_ref[...], post_g_ref[...])
        out_ref[...] = h
        if emit_next:
            u_ref[...] = _rms(h, next_g_ref[...]).astype(BF16)


def _ffn(x, pre_g, wg, wu, wd, post_g, next_g=None):
    t, d = x.shape
    dff = wg.shape[1]
    emit_next = next_g is not None
    row = pl.BlockSpec((FFN_TM, d), lambda i, j: (i, 0))
    gain = pl.BlockSpec((1, d), lambda i, j: (0, 0))
    in_specs = [row, gain,
                pl.BlockSpec((d, FFN_TF), lambda i, j: (0, j)),
                pl.BlockSpec((d, FFN_TF), lambda i, j: (0, j)),
                pl.BlockSpec((FFN_TF, d), lambda i, j: (j, 0)),
                gain]
    args = [x, pre_g, wg, wu, wd, post_g]
    out_shape = [jax.ShapeDtypeStruct((t, d), F32)]
    out_specs = [row]
    if emit_next:
        in_specs.append(gain)
        args.append(next_g)
        out_shape.append(jax.ShapeDtypeStruct((t, d), BF16))
        out_specs.append(row)
    vmem = (2 * FFN_TM * d * 4 * 2 + 2 * FFN_TM * d * 2 + FFN_TM * d * (2 + 4)
            + 2 * 3 * d * FFN_TF * 2 + 3 * FFN_TM * FFN_TF * 4)
    res = pl.pallas_call(
        functools.partial(_ffn_body, emit_next),
        grid=(t // FFN_TM, dff // FFN_TF),
        in_specs=in_specs, out_specs=out_specs, out_shape=out_shape,
        scratch_shapes=[pltpu.VMEM((FFN_TM, d), BF16), pltpu.VMEM((FFN_TM, d), F32)],
        compiler_params=pltpu.CompilerParams(
            dimension_semantics=("arbitrary", "arbitrary"), vmem_limit_bytes=_vmem_limit(vmem)),
        name="ffn_next" if emit_next else "ffn",
    )(*args)
    return res if emit_next else res[0]


def _proj_body(gate, x_ref, w_ref, *rest):
    if gate:
        b_ref, o_ref = rest
    else:
        (o_ref,) = rest
    y = jnp.dot(x_ref[...], w_ref[...], preferred_element_type=F32)
    if gate:
        y = jax.nn.sigmoid(y + b_ref[...])
    o_ref[...] = y.astype(o_ref.dtype)


def _proj(x, w, out_dtype, name, bias=None):
    t, k = x.shape
    n = w.shape[1]
    tn = min(n, 512)
    gate = bias is not None
    in_specs = [pl.BlockSpec((PROJ_TM, k), lambda i, j: (i, 0)),
                pl.BlockSpec((k, tn), lambda i, j: (0, j))]
    args = [x, w]
    if gate:
        in_specs.append(pl.BlockSpec((1, tn), lambda i, j: (0, j)))
        args.append(bias)
    vmem = 2 * PROJ_TM * k * 2 + 2 * k * tn * 2 + 3 * PROJ_TM * tn * 4
    return pl.pallas_call(
        functools.partial(_proj_body, gate),
        grid=(t // PROJ_TM, n // tn),
        in_specs=in_specs,
        out_specs=pl.BlockSpec((PROJ_TM, tn), lambda i, j: (i, j)),
        out_shape=jax.ShapeDtypeStruct((t, n), out_dtype),
        compiler_params=pltpu.CompilerParams(
            dimension_semantics=("arbitrary", "arbitrary"), vmem_limit_bytes=_vmem_limit(vmem)),
        name=name,
    )(*args)


def _indexer_body(seq, qi_ref, wq_ref, kw_ref, mask_ref, kb_ref, key_ref):
    i = pl.program_id(1)
    t0 = i * IDX_TQ
    n_chunks_total = seq // IDX_TC
    nch = lax.shift_right_logical(t0 + IDX_TQ + IDX_TC - 1, int(math.log2(IDX_TC)))
    tiles_per_chunk = IDX_TC // ATT_T

    @pl.when(i == 0)
    def _():
        kb_ref[...] = kw_ref[:, :IDX_DIM].astype(BF16)

    w = wq_ref[:, IDX_DIM:IDX_DIM + IDX_HEADS] * (IDX_HEADS ** -0.5 * IDX_DIM ** -0.5)
    row = t0 + lax.broadcasted_iota(jnp.int32, (IDX_TQ, 1), 0)

    def score_chunk(c, carry):
        ks = kb_ref[pl.ds(pl.multiple_of(c * IDX_TC, IDX_TC), IDX_TC), :]
        acc = jnp.zeros((IDX_TQ, IDX_TC), F32)
        for h in range(IDX_HEADS):
            qh = qi_ref[:, h * IDX_DIM:(h + 1) * IDX_DIM]
            d = lax.dot_general(qh, ks, (((1,), (1,)), ((), ())), preferred_element_type=F32)
            acc = acc + jnp.maximum(d, 0.0) * w[:, h:h + 1]
        bits = pltpu.bitcast(acc, jnp.int32)
        key = bits ^ ((bits >> 31) & jnp.int32(0x7FFFFFFF))
        col = c * IDX_TC + lax.broadcasted_iota(jnp.int32, (1, IDX_TC), 1)
        key_ref[c] = jnp.where(col <= row, key, jnp.int32(INT_MIN))
        return carry

    lax.fori_loop(0, nch, score_chunk, 0)

    def search_pass(b, u):
        cand_u = u | lax.shift_left(jnp.int32(1), 31 - b)
        cand = cand_u ^ jnp.int32(INT_MIN)

        def count_chunk(c, cnt):
            m = jnp.where(key_ref[c] >= cand, 1.0, 0.0)
            for s in range(IDX_TC // LANES):
                cnt = cnt + m[:, s * LANES:(s + 1) * LANES]
            return cnt

        cnt = lax.fori_loop(0, nch, count_chunk, jnp.zeros((IDX_TQ, LANES), F32))
        total = jnp.sum(cnt, axis=1, keepdims=True)
        return jnp.where(total >= float(TOPK_MAX), cand_u, u)

    u = lax.fori_loop(0, 32, search_pass, jnp.zeros((IDX_TQ, 1), jnp.int32))
    thr = jnp.maximum(u ^ jnp.int32(INT_MIN), jnp.int32(INT_MIN + 1))

    def write_chunk(c, carry):
        m = jnp.where(key_ref[c] >= thr, 0.0, MASKED).astype(BF16)
        for s in range(tiles_per_chunk):
            mask_ref[0, c * tiles_per_chunk + s] = m[:, s * ATT_T:(s + 1) * ATT_T]
        return carry

    lax.fori_loop(0, nch, write_chunk, 0)

    def fill_chunk(c, carry):
        for s in range(tiles_per_chunk):
            mask_ref[0, c * tiles_per_chunk + s] = jnp.full((IDX_TQ, ATT_T), MASKED, BF16)
        return carry

    lax.fori_loop(nch, n_chunks_total, fill_chunk, 0)


def _indexer(qi, kw, batch, seq):
    nq = seq // IDX_TQ
    nkt = seq // ATT_T
    vmem = (2 * IDX_TQ * ATTN_DIM * 2 + 2 * IDX_TQ * LANES * 4 + 2 * seq * LANES * 4
            + 2 * seq * IDX_TQ * 2 + seq * LANES * 2 + seq * IDX_TQ * 4 + 8 * IDX_TQ * IDX_TC * 4)
    return pl.pallas_call(
        functools.partial(_indexer_body, seq),
        grid=(batch, nq),
        in_specs=[pl.BlockSpec((IDX_TQ, IDX_HEADS * IDX_DIM), lambda b, i: (b * nq + i, 0)),
                  pl.BlockSpec((IDX_TQ, LANES), lambda b, i: (b * nq + i, 0)),
                  pl.BlockSpec((seq, LANES), lambda b, i: (b, 0))],
        out_specs=pl.BlockSpec((1, nkt, IDX_TQ, ATT_T), lambda b, i: (b, 0, i, 0)),
        out_shape=jax.ShapeDtypeStruct((batch, nkt, seq, ATT_T), BF16),
        scratch_shapes=[pltpu.VMEM((seq, IDX_DIM), BF16),
                        pltpu.VMEM((seq // IDX_TC, IDX_TQ, IDX_TC), jnp.int32)],
        compiler_params=pltpu.CompilerParams(
            dimension_semantics=("arbitrary", "arbitrary"), vmem_limit_bytes=_vmem_limit(vmem)),
        name="indexer",
    )(qi, kw, kw)


def _bias_init(rb_ref, bias_ref):
    shape = (ATT_T, 2 * ATT_T)
    dist = (lax.broadcasted_iota(jnp.int32, shape, 0) - lax.broadcasted_iota(jnp.int32, shape, 1) + ATT_T)
    max_exact = NUM_BUCKETS // 2
    n = jnp.maximum(dist, 0)
    nf = jnp.maximum(n, max_exact).astype(F32)
    large = max_exact + (jnp.log(nf / max_exact) / math.log(MAX_DISTANCE / max_exact)
                         * (NUM_BUCKETS - max_exact)).astype(jnp.int32)
    large = jnp.minimum(large, NUM_BUCKETS - 1)
    bucket = jnp.where(n < max_exact, n, large)
    for h in range(N_HEADS):
        far = rb_ref[NUM_BUCKETS - 1, h]
        acc = jnp.zeros(shape, F32)
        for b in range(NUM_BUCKETS - 1):
            acc = jnp.where(bucket == b, (rb_ref[b, h] - far) * LOG2E, acc)
        bias_ref[h] = acc


def _attn_body(rb_ref, q_ref, k_ref, v_ref, mask_ref, o_ref, bias_ref, acc_ref, m_ref, l_ref):
    b = pl.program_id(0)
    i = pl.program_id(1)

    @pl.when((b == 0) & (i == 0))
    def _():
        _bias_init(rb_ref, bias_ref)

    acc_ref[...] = jnp.zeros_like(acc_ref)
    m_ref[...] = jnp.full_like(m_ref, MASKED)
    l_ref[...] = jnp.zeros_like(l_ref)

    def tile(j, near):
        maskf = mask_ref[0, j].astype(F32)
        rows = pl.ds(pl.multiple_of(j * ATT_T, ATT_T), ATT_T)
        for h in range(N_HEADS):
            cols = slice(h * HEAD_DIM, (h + 1) * HEAD_DIM)
            s = lax.dot_general(q_ref[:, cols], k_ref[rows, cols], (((1,), (1,)), ((), ())),
                                preferred_element_type=F32)
            s = s + maskf
            if near is not None:
                s = s + bias_ref[h, :, near * ATT_T:(near + 1) * ATT_T]
            m_old = m_ref[h]
            m_new = jnp.maximum(m_old, jnp.max(s, axis=1, keepdims=True))
            alpha = jnp.exp2(m_old - m_new)
            p = jnp.exp2(s - m_new)
            l_ref[h] = alpha * l_ref[h] + jnp.sum(p, axis=1, keepdims=True)
            pv = jnp.dot(p.astype(BF16), v_ref[rows, cols], preferred_element_type=F32)
            acc_ref[:, cols] = alpha * acc_ref[:, cols] + pv
            m_ref[h] = m_new

    def far_tile(j, carry):
        tile(j, None)
        return carry

    lax.fori_loop(0, i - 1, far_tile, 0)

    @pl.when(i >= 1)
    def _():
        tile(i - 1, 0)

    tile(i, 1)

    for h in range(N_HEADS):
        cols = slice(h * HEAD_DIM, (h + 1) * HEAD_DIM)
        o_ref[:, cols] = (acc_ref[:, cols] / l_ref[h]).astype(o_ref.dtype)


def _attention(qkv, mask, rel_bias, batch, seq):
    nq = seq // ATT_T
    nkt = seq // ATT_T
    vmem = (2 * ATT_T * ATTN_DIM * 2 * 2 + 2 * seq * ATTN_DIM * 2 + 2 * nkt * ATT_T * ATT_T * 2
            + N_HEADS * ATT_T * 2 * ATT_T * 4 + ATT_T * ATTN_DIM * 4 + 2 * N_HEADS * ATT_T * LANES * 4
            + 6 * ATT_T * ATT_T * 4)
    resident = pl.Buffered(1)
    return pl.pallas_call(
        _attn_body,
        grid=(batch, nq),
        in_specs=[pl.BlockSpec(memory_space=pltpu.SMEM),
                  pl.BlockSpec((ATT_T, ATTN_DIM), lambda b, i: (b * nq + i, 0)),
                  pl.BlockSpec((seq, ATTN_DIM), lambda b, i: (b, 1), pipeline_mode=resident),
                  pl.BlockSpec((seq, ATTN_DIM), lambda b, i: (b, 2), pipeline_mode=resident),
                  pl.BlockSpec((1, nkt, ATT_T, ATT_T), lambda b, i: (b, 0, i, 0))],
        out_specs=pl.BlockSpec((ATT_T, ATTN_DIM), lambda b, i: (b * nq + i, 0)),
        out_shape=jax.ShapeDtypeStruct((batch * seq, ATTN_DIM), BF16),
        scratch_shapes=[pltpu.VMEM((N_HEADS, ATT_T, 2 * ATT_T), F32),
                        pltpu.VMEM((ATT_T, ATTN_DIM), F32),
                        pltpu.VMEM((N_HEADS, ATT_T, 1), F32),
                        pltpu.VMEM((N_HEADS, ATT_T, 1), F32)],
        compiler_params=pltpu.CompilerParams(
            dimension_semantics=("arbitrary", "arbitrary"), vmem_limit_bytes=_vmem_limit(vmem)),
        name="attention",
    )(rel_bias, qkv, qkv, qkv, mask)


def _conv_body(cur_ref, halo_ref, dw_ref, dwb_ref, lng_ref, lnb_ref, o_ref, h_ref, y_ref):
    i = pl.program_id(1)

    def glu(x):
        return x[:, :CONV_CH].astype(F32) * jax.nn.sigmoid(x[:, CONV_CH:].astype(F32))

    h_ref[:CONV_HALO, :] = jnp.where(i > 0, glu(halo_ref[...]), 0.0)
    h_ref[CONV_HALO:, :] = glu(cur_ref[...])

    first = CONV_HALO - (CONV_WIDTH - 1)
    for r in range(CONV_TS // CONV_RB):
        for c in range(CONV_CH // CONV_CB):
            cols = slice(c * CONV_CB, (c + 1) * CONV_CB)
            acc = jnp.broadcast_to(dwb_ref[:, cols], (CONV_RB, CONV_CB))
            for j in range(CONV_WIDTH):
                acc = acc + h_ref[r * CONV_RB + first + j:r * CONV_RB + first + j + CONV_RB, cols] * dw_ref[j:j + 1, cols]
            y_ref[r * CONV_RB:(r + 1) * CONV_RB, cols] = acc

    y = y_ref[...]
    mu = jnp.mean(y, axis=-1, keepdims=True)
    var = jnp.mean(jnp.square(y - mu), axis=-1, keepdims=True)
    z = (y - mu) * lax.rsqrt(var + EPS) * lng_ref[...] + lnb_ref[...]
    o_ref[...] = (z * jax.nn.sigmoid(z)).astype(o_ref.dtype)


def _conv_module(glu_in, dw, dw_b, ln_g, ln_b, batch, seq):
    ns = seq // CONV_TS
    halo_per_tile = CONV_TS // CONV_HALO
    vec = pl.BlockSpec((1, CONV_CH), lambda b, i: (0, 0))
    vmem = (2 * (CONV_TS + CONV_HALO) * 2 * CONV_CH * 2 + 2 * CONV_TS * CONV_CH * 2
            + (2 * CONV_TS + CONV_HALO) * CONV_CH * 4 + 4 * CONV_TS * CONV_CH * 4)
    return pl.pallas_call(
        _conv_body,
        grid=(batch, ns),
        in_specs=[pl.BlockSpec((CONV_TS, 2 * CONV_CH), lambda b, i: (b * ns + i, 0)),
                  pl.BlockSpec((CONV_HALO, 2 * CONV_CH),
                               lambda b, i: (jnp.maximum((b * ns + i) * halo_per_tile - 1, 0), 0)),
                  pl.BlockSpec((CONV_WIDTH, CONV_CH), lambda b, i: (0, 0)),
                  vec, vec, vec],
        out_specs=pl.BlockSpec((CONV_TS, CONV_CH), lambda b, i: (b * ns + i, 0)),
        out_shape=jax.ShapeDtypeStruct((batch * seq, CONV_CH), BF16),
        scratch_shapes=[pltpu.VMEM((CONV_TS + CONV_HALO, CONV_CH), F32),
                        pltpu.VMEM((CONV_TS, CONV_CH), F32)],
        compiler_params=pltpu.CompilerParams(
            dimension_semantics=("arbitrary", "arbitrary"), vmem_limit_bytes=_vmem_limit(vmem)),
        name="conv_module",
    )(glu_in, glu_in, dw, dw_b, ln_g, ln_b)


def _mix_body(attn_ref, conv_ref, gates_ref, h_ref, wo_ref, wpw_ref, wout_ref, g_ref, o_ref):
    a = jnp.dot(attn_ref[...], wo_ref[...], preferred_element_type=F32)
    c = jnp.dot(conv_ref[...], wpw_ref[...], preferred_element_type=F32)
    mixed = (gates_ref[:, :D_MODEL].astype(F32) * a + gates_ref[:, D_MODEL:].astype(F32) * c).astype(BF16)
    y = jnp.dot(mixed, wout_ref[...], preferred_element_type=F32)
    o_ref[...] = h_ref[...] + _rms(y, g_ref[...])


def _mix_out(attn, conv, gates, h, w_o, w_pw2, w_out, post_g):
    t = h.shape[0]
    resident = pl.Buffered(1)

    def rows(n):
        return pl.BlockSpec((OUT_TM, n), lambda i: (i, 0))

    def whole(a):
        return pl.BlockSpec(a.shape, lambda i: (0, 0), pipeline_mode=resident)

    vmem = ((w_o.size + w_pw2.size + w_out.size) * 2
            + 2 * OUT_TM * (2 * ATTN_DIM * 2 + 2 * D_MODEL * 2 + 2 * D_MODEL * 4) + 5 * OUT_TM * D_MODEL * 4)
    return pl.pallas_call(
        _mix_body,
        grid=(t // OUT_TM,),
        in_specs=[rows(ATTN_DIM), rows(CONV_CH), rows(2 * D_MODEL), rows(D_MODEL),
                  whole(w_o), whole(w_pw2), whole(w_out), pl.BlockSpec((1, D_MODEL), lambda i: (0, 0))],
        out_specs=rows(D_MODEL),
        out_shape=jax.ShapeDtypeStruct((t, D_MODEL), F32),
        compiler_params=pltpu.CompilerParams(
            dimension_semantics=("arbitrary",), vmem_limit_bytes=_vmem_limit(vmem)),
        name="mix_out",
    )(attn, conv, gates, h, w_o, w_pw2, w_out, post_g)


def kernel(x, rel_bias, ffn1_pre_g, ffn1_wg, ffn1_wu, ffn1_wd, ffn1_post_g, mix_pre_g, w_in, b_gate, w_o,
           conv_dw, conv_dw_b, conv_ln_g, conv_ln_b, w_pw2, w_out, mix_post_g, ffn2_pre_g, ffn2_wg, ffn2_wu,
           ffn2_wd, ffn2_post_g):
    batch, seq, d = x.shape
    depth = ffn1_wg.shape[0]
    h = x.reshape(batch * seq, d)
    o_k, o_v, o_qi, o_ki, o_wi, o_glu, o_gates = (
        ATTN_DIM, 2 * ATTN_DIM, 3 * ATTN_DIM, 3 * ATTN_DIM + IDX_HEADS * IDX_DIM,
        3 * ATTN_DIM + IDX_HEADS * IDX_DIM + IDX_DIM,
        3 * ATTN_DIM + IDX_HEADS * IDX_DIM + IDX_DIM + IDX_HEADS,
        3 * ATTN_DIM + IDX_HEADS * IDX_DIM + IDX_DIM + IDX_HEADS + 2 * CONV_CH)
    for l in range(depth):
        wl = w_in[l]
        w_qkv = jnp.concatenate([wl[:, :o_k] * (HEAD_DIM ** -0.5 * LOG2E), wl[:, o_k:o_qi]], axis=1).astype(BF16)
        w_qi = wl[:, o_qi:o_ki].astype(BF16)
        w_kw = jnp.pad(wl[:, o_ki:o_glu], ((0, 0), (0, LANES - IDX_DIM - IDX_HEADS))).astype(BF16)
        w_glu = wl[:, o_glu:o_gates].astype(BF16)
        w_gates = wl[:, o_gates:].astype(BF16)

        h, u = _ffn(h, ffn1_pre_g[l][None], ffn1_wg[l].astype(BF16), ffn1_wu[l].astype(BF16),
                    ffn1_wd[l].astype(BF16), ffn1_post_g[l][None], mix_pre_g[l][None])

        qkv = _proj(u, w_qkv, BF16, "proj_qkv")
        qi = _proj(u, w_qi, BF16, "proj_qi")
        kw = _proj(u, w_kw, F32, "proj_kw")
        glu_in = _proj(u, w_glu, BF16, "proj_glu")
        gates = _proj(u, w_gates, BF16, "proj_gates", bias=b_gate[l][None])

        mask = _indexer(qi, kw, batch, seq)
        attn = _attention(qkv, mask, rel_bias, batch, seq)
        conv = _conv_module(glu_in, conv_dw[l], conv_dw_b[l][None], conv_ln_g[l][None], conv_ln_b[l][None],
                            batch, seq)
        h = _mix_out(attn, conv, gates, h, w_o[l].astype(BF16), w_pw2[l].astype(BF16),
                     w_out[l].astype(BF16), mix_post_g[l][None])

        h = _ffn(h, ffn2_pre_g[l][None], ffn2_wg[l].astype(BF16), ffn2_wu[l].astype(BF16),
                 ffn2_wd[l].astype(BF16), ffn2_post_g[l][None])
    return h.reshape(batch, seq, d)
```

```python
import functools
import math

import jax
import jax.numpy as jnp
from jax import lax
from jax.experimental import pallas as pl
from jax.experimental.pallas import tpu as pltpu

D_MODEL = 2048
N_HEADS = 8
HEAD_DIM = 128
ATTN_DIM = N_HEADS * HEAD_DIM
IDX_HEADS = 16
IDX_DIM = 64
TOPK_MAX = 256
NUM_BUCKETS = 32
MAX_DISTANCE = 128
CONV_CH = 1024
CONV_WIDTH = 31
D_FF = 5632
FFN_RES_WEIGHT = 0.5
EPS = 1e-6

F32 = jnp.float32
BF16 = jnp.bfloat16
LANES = 128
SUBLANES = 8
V7X_VMEM_BYTES = 64 * 1024 * 1024
MASKED = -1e30
INT_MIN = -(2 ** 31)
LOG2E = math.log2(math.e)

FFN_TM, FFN_TF = 512, 512
PROJ_TM = 1024
TILE = 256
PROJT_ROWS = 512
COUNT_TILES = 4
CONV_TS = 512
CONV_HALO = 32
CONV_RB, CONV_CB = 128, 256
OUT_TM = 256

_NT = (((1,), (1,)), ((), ()))


def _vmem_limit(nbytes):
    return int(min(nbytes + (8 << 20), V7X_VMEM_BYTES - (4 << 20)))


def _rms(x, g):
    y = x * lax.rsqrt(jnp.mean(x * x, axis=-1, keepdims=True) + EPS)
    return y * g


def _ffn_body(emit_next, x_ref, pre_g_ref, wg_ref, wu_ref, wd_ref, post_g_ref, *rest):
    if emit_next:
        next_g_ref, out_ref, u_ref, xn_ref, acc_ref = rest
    else:
        out_ref, xn_ref, acc_ref = rest
    j = pl.program_id(1)

    @pl.when(j == 0)
    def _():
        xn_ref[...] = _rms(x_ref[...], pre_g_ref[...]).astype(BF16)
        acc_ref[...] = jnp.zeros_like(acc_ref)

    xn = xn_ref[...]
    g = jnp.dot(xn, wg_ref[...], preferred_element_type=F32)
    u = jnp.dot(xn, wu_ref[...], preferred_element_type=F32)
    a = (g * jax.nn.sigmoid(g) * u).astype(BF16)
    acc_ref[...] += jnp.dot(a, wd_ref[...], preferred_element_type=F32)

    @pl.when(j == pl.num_programs(1) - 1)
    def _():
        h = x_ref[...] + FFN_RES_WEIGHT * _rms(acc_ref[...], post_g_ref[...])
        out_ref[...] = h
        if emit_next:
            u_ref[...] = _rms(h, next_g_ref[...]).astype(BF16)


def _ffn(x, pre_g, wg, wu, wd, post_g, next_g=None):
    t, d = x.shape
    dff = wg.shape[1]
    emit_next = next_g is not None
    row = pl.BlockSpec((FFN_TM, d), lambda i, j: (i, 0))
    gain = pl.BlockSpec((1, d), lambda i, j: (0, 0))
    in_specs = [row, gain,
                pl.BlockSpec((d, FFN_TF), lambda i, j: (0, j)),
                pl.BlockSpec((d, FFN_TF), lambda i, j: (0, j)),
                pl.BlockSpec((FFN_TF, d), lambda i, j: (j, 0)),
                gain]
    args = [x, pre_g, wg, wu, wd, post_g]
    out_shape = [jax.ShapeDtypeStruct((t, d), F32)]
    out_specs = [row]
    if emit_next:
        in_specs.append(gain)
        args.append(next_g)
        out_shape.append(jax.ShapeDtypeStruct((t, d), BF16))
        out_specs.append(row)
    vmem = (2 * FFN_TM * d * 4 * 2 + 2 * FFN_TM * d * 2 + FFN_TM * d * (2 + 4)
            + 2 * 3 * d * FFN_TF * 2 + 3 * FFN_TM * FFN_TF * 4)
    res = pl.pallas_call(
        functools.partial(_ffn_body, emit_next),
        grid=(t // FFN_TM, dff // FFN_TF),
        in_specs=in_specs, out_specs=out_specs, out_shape=out_shape,
        scratch_shapes=[pltpu.VMEM((FFN_TM, d), BF16), pltpu.VMEM((FFN_TM, d), F32)],
        compiler_params=pltpu.CompilerParams(
            dimension_semantics=("arbitrary", "arbitrary"), vmem_limit_bytes=_vmem_limit(vmem)),
        name="ffn_next" if emit_next else "ffn",
    )(*args)
    return res if emit_next else res[0]


def _proj_body(gate, x_ref, w_ref, *rest):
    if gate:
        b_ref, o_ref = rest
    else:
        (o_ref,) = rest
    y = jnp.dot(x_ref[...], w_ref[...], preferred_element_type=F32)
    if gate:
        y = jax.nn.sigmoid(y + b_ref[...])
    o_ref[...] = y.astype(o_ref.dtype)


def _proj(x, w, out_dtype, name, bias=None):
    t, k = x.shape
    n = w.shape[1]
    tn = min(n, 512)
    gate = bias is not None
    in_specs = [pl.BlockSpec((PROJ_TM, k), lambda i, j: (i, 0)),
                pl.BlockSpec((k, tn), lambda i, j: (0, j))]
    args = [x, w]
    if gate:
        in_specs.append(pl.BlockSpec((1, tn), lambda i, j: (0, j)))
        args.append(bias)
    vmem = 2 * PROJ_TM * k * 2 + 2 * k * tn * 2 + 3 * PROJ_TM * tn * 4
    return pl.pallas_call(
        functools.partial(_proj_body, gate),
        grid=(t // PROJ_TM, n // tn),
        in_specs=in_specs,
        out_specs=pl.BlockSpec((PROJ_TM, tn), lambda i, j: (i, j)),
        out_shape=jax.ShapeDtypeStruct((t, n), out_dtype),
        compiler_params=pltpu.CompilerParams(
            dimension_semantics=("arbitrary", "arbitrary"), vmem_limit_bytes=_vmem_limit(vmem)),
        name=name,
    )(*args)


def _proj_t_body(u_ref, wt_ref, wwt_ref, o_ref, w_ref):
    u = u_ref[...]
    for r in range(wt_ref.shape[0] // PROJT_ROWS):
        rows = slice(r * PROJT_ROWS, (r + 1) * PROJT_ROWS)
        o_ref[0, rows, :] = lax.dot_general(wt_ref[rows, :], u, _NT, preferred_element_type=F32).astype(BF16)
    w_ref[0] = lax.dot_general(wwt_ref[...], u, _NT, preferred_element_type=F32)


def _proj_t(u, wt, wwt):
    t, k = u.shape
    n, nw = wt.shape[0], wwt.shape[0]
    resident = pl.Buffered(1)
    vmem = 2 * TILE * k * 2 + (n + nw) * k * 2 + 2 * (n * 2 + nw * 4) * TILE + 2 * PROJT_ROWS * TILE * 4
    return pl.pallas_call(
        _proj_t_body,
        grid=(t // TILE,),
        in_specs=[pl.BlockSpec((TILE, k), lambda i: (i, 0)),
                  pl.BlockSpec((n, k), lambda i: (0, 0), pipeline_mode=resident),
                  pl.BlockSpec((nw, k), lambda i: (0, 0), pipeline_mode=resident)],
        out_specs=[pl.BlockSpec((1, n, TILE), lambda i: (i, 0, 0)),
                   pl.BlockSpec((1, nw, TILE), lambda i: (i, 0, 0))],
        out_shape=[jax.ShapeDtypeStruct((t // TILE, n, TILE), BF16),
                   jax.ShapeDtypeStruct((t // TILE, nw, TILE), F32)],
        compiler_params=pltpu.CompilerParams(
            dimension_semantics=("arbitrary",), vmem_limit_bytes=_vmem_limit(vmem)),
        name="proj_t",
    )(u, wt, wwt)


def _indexer_body(seq, qi_ref, w_ref, kw_ref, mask_ref, kb_ref, key_ref):
    i = pl.program_id(1)
    nch = i + 1
    n_tiles = seq // TILE

    @pl.when(i == 0)
    def _():
        kb_ref[...] = kw_ref[:, :IDX_DIM].astype(BF16)

    w = w_ref[0] * (IDX_HEADS ** -0.5 * IDX_DIM ** -0.5)
    q_pos = i * TILE + lax.broadcasted_iota(jnp.int32, (1, TILE), 1)

    def rows_of(c):
        return pl.ds(pl.multiple_of(c * TILE, TILE), TILE)

    def score_chunk(c, carry):
        ks = kb_ref[rows_of(c), :]
        acc = jnp.zeros((TILE, TILE), F32)
        for h in range(IDX_HEADS):
            d = jnp.dot(ks, qi_ref[0, h * IDX_DIM:(h + 1) * IDX_DIM, :], preferred_element_type=F32)
            acc = acc + jnp.maximum(d, 0.0) * w[h:h + 1, :]
        bits = pltpu.bitcast(acc, jnp.int32)
        key = bits ^ ((bits >> 31) & jnp.int32(0x7FFFFFFF))
        k_pos = c * TILE + lax.broadcasted_iota(jnp.int32, (TILE, 1), 0)
        key_ref[rows_of(c), :] = jnp.where(k_pos <= q_pos, key, jnp.int32(INT_MIN))
        return carry

    lax.fori_loop(0, nch, score_chunk, 0)

    n_steps = (nch + COUNT_TILES - 1) // COUNT_TILES

    def pad_chunk(c, carry):
        key_ref[rows_of(c), :] = jnp.full((TILE, TILE), INT_MIN, jnp.int32)
        return carry

    lax.fori_loop(nch, n_steps * COUNT_TILES, pad_chunk, 0)

    def search_pass(b, u):
        cand_u = u | lax.shift_left(jnp.int32(1), 31 - b)
        cand = cand_u ^ jnp.int32(INT_MIN)

        def count_chunk(c, cnt):
            rows = pl.ds(pl.multiple_of(c * (COUNT_TILES * TILE), COUNT_TILES * TILE), COUNT_TILES * TILE)
            m = jnp.where(key_ref[rows, :] >= cand, 1.0, 0.0)
            return cnt + jnp.sum(m.reshape(COUNT_TILES * TILE // SUBLANES, SUBLANES, TILE), axis=0)

        cnt = lax.fori_loop(0, n_steps, count_chunk, jnp.zeros((SUBLANES, TILE), F32))
        total = jnp.sum(cnt, axis=0, keepdims=True)
        return jnp.where(total >= float(TOPK_MAX), cand_u, u)

    u = lax.fori_loop(0, 32, search_pass, jnp.zeros((1, TILE), jnp.int32))
    thr = jnp.maximum(u ^ jnp.int32(INT_MIN), jnp.int32(INT_MIN + 1))

    def write_chunk(c, carry):
        mask_ref[0, 0, rows_of(c), :] = jnp.where(key_ref[rows_of(c), :] >= thr, 0.0, MASKED).astype(BF16)
        return carry

    lax.fori_loop(0, nch, write_chunk, 0)

    def fill_chunk(c, carry):
        mask_ref[0, 0, rows_of(c), :] = jnp.full((TILE, TILE), MASKED, BF16)
        return carry

    lax.fori_loop(nch, n_tiles, fill_chunk, 0)


def _indexer(qvt, wt, kw, batch, seq):
    nq = seq // TILE
    qi_block = (3 * ATTN_DIM) // (IDX_HEADS * IDX_DIM) - 1
    vmem = (2 * IDX_HEADS * IDX_DIM * TILE * 2 + 2 * seq * LANES * 4 + 2 * seq * TILE * 2
            + seq * LANES * 2 + seq * TILE * 4 + 8 * TILE * TILE * 4)
    return pl.pallas_call(
        functools.partial(_indexer_body, seq),
        grid=(batch, nq),
        in_specs=[pl.BlockSpec((1, IDX_HEADS * IDX_DIM, TILE), lambda b, i: (b * nq + i, qi_block, 0)),
                  pl.BlockSpec((1, IDX_HEADS, TILE), lambda b, i: (b * nq + i, 0, 0)),
                  pl.BlockSpec((seq, LANES), lambda b, i: (b, 0))],
        out_specs=pl.BlockSpec((1, 1, seq, TILE), lambda b, i: (b, i, 0, 0)),
        out_shape=jax.ShapeDtypeStruct((batch, nq, seq, TILE), BF16),
        scratch_shapes=[pltpu.VMEM((seq, IDX_DIM), BF16),
                        pltpu.VMEM((seq, TILE), jnp.int32)],
        compiler_params=pltpu.CompilerParams(
            dimension_semantics=("arbitrary", "arbitrary"), vmem_limit_bytes=_vmem_limit(vmem)),
        name="indexer",
    )(qvt, wt, kw)


def _bias_init(rb_ref, bias_ref):
    shape = (2 * TILE, TILE)
    dist = (lax.broadcasted_iota(jnp.int32, shape, 1) - lax.broadcasted_iota(jnp.int32, shape, 0) + TILE)
    max_exact = NUM_BUCKETS // 2
    n = jnp.maximum(dist, 0)
    nf = jnp.maximum(n, max_exact).astype(F32)
    large = max_exact + (jnp.log(nf / max_exact) / math.log(MAX_DISTANCE / max_exact)
                         * (NUM_BUCKETS - max_exact)).astype(jnp.int32)
    large = jnp.minimum(large, NUM_BUCKETS - 1)
    bucket = jnp.where(n < max_exact, n, large)
    for h in range(N_HEADS):
        far = rb_ref[NUM_BUCKETS - 1, h]
        acc = jnp.zeros(shape, F32)
        for b in range(NUM_BUCKETS - 1):
            acc = jnp.where(bucket == b, (rb_ref[b, h] - far) * LOG2E, acc)
        bias_ref[h] = acc


def _attn_body(rb_ref, q_ref, k_ref, vt_ref, mask_ref, o_ref, bias_ref, acc_ref, m_ref, l_ref, alpha_ref,
               s_ref, p_ref):
    b = pl.program_id(0)
    i = pl.program_id(1)

    @pl.when((b == 0) & (i == 0))
    def _():
        _bias_init(rb_ref, bias_ref)

    acc_ref[...] = jnp.zeros_like(acc_ref)
    m_ref[...] = jnp.full_like(m_ref, MASKED)
    l_ref[...] = jnp.zeros_like(l_ref)

    def tile(j, near):
        rows = pl.ds(pl.multiple_of(j * TILE, TILE), TILE)
        maskf = mask_ref[0, 0, rows, :].astype(F32)

        for h in range(N_HEADS):
            feat = slice(h * HEAD_DIM, (h + 1) * HEAD_DIM)
            s = jnp.dot(k_ref[rows, feat], q_ref[0, feat, :], preferred_element_type=F32) + maskf
            if near is not None:
                s = s + bias_ref[h, near * TILE:(near + 1) * TILE, :]
            s_ref[h] = s
        for h in range(N_HEADS):
            s = s_ref[h]
            m_old = m_ref[h:h + 1, :]
            m_new = jnp.maximum(m_old, jnp.max(s, axis=0, keepdims=True))
            alpha = jnp.exp2(m_old - m_new)
            p = jnp.exp2(s - m_new)
            l_ref[h:h + 1, :] = alpha * l_ref[h:h + 1, :] + jnp.sum(p, axis=0, keepdims=True)
            p_ref[h] = p.astype(BF16)
            alpha_ref[h:h + 1, :] = alpha
            m_ref[h:h + 1, :] = m_new
        for h in range(N_HEADS):
            feat = slice(h * HEAD_DIM, (h + 1) * HEAD_DIM)
            pv = jnp.dot(vt_ref[j, feat, :], p_ref[h], preferred_element_type=F32)
            acc_ref[feat, :] = alpha_ref[h:h + 1, :] * acc_ref[feat, :] + pv

    def far_tile(j, carry):
        tile(j, None)
        return carry

    lax.fori_loop(0, i - 1, far_tile, 0)

    @pl.when(i >= 1)
    def _():
        tile(i - 1, 0)

    tile(i, 1)

    for h in range(N_HEADS):
        feat = slice(h * HEAD_DIM, (h + 1) * HEAD_DIM)
        o_ref[:, feat] = (acc_ref[feat, :] / l_ref[h:h + 1, :]).T.astype(o_ref.dtype)


def _attention(qvt, k, mask, rel_bias, batch, seq):
    nq = seq // TILE
    vmem = (2 * ATTN_DIM * TILE * 2 * 2 + 2 * seq * ATTN_DIM * 2 + 2 * seq * TILE * 2
            + N_HEADS * 2 * TILE * TILE * 4 + ATTN_DIM * TILE * 4 + N_HEADS * TILE * TILE * (4 + 2)
            + 4 * TILE * TILE * 4)
    resident = pl.Buffered(1)
    return pl.pallas_call(
        _attn_body,
        grid=(batch, nq),
        in_specs=[pl.BlockSpec(memory_space=pltpu.SMEM),
                  pl.BlockSpec((1, ATTN_DIM, TILE), lambda b, i: (b * nq + i, 0, 0)),
                  pl.BlockSpec((seq, ATTN_DIM), lambda b, i: (b, 0), pipeline_mode=resident),
                  pl.BlockSpec((nq, ATTN_DIM, TILE), lambda b, i: (b, 1, 0), pipeline_mode=resident),
                  pl.BlockSpec((1, 1, seq, TILE), lambda b, i: (b, i, 0, 0))],
        out_specs=pl.BlockSpec((TILE, ATTN_DIM), lambda b, i: (b * nq + i, 0)),
        out_shape=jax.ShapeDtypeStruct((batch * seq, ATTN_DIM), BF16),
        scratch_shapes=[pltpu.VMEM((N_HEADS, 2 * TILE, TILE), F32),
                        pltpu.VMEM((ATTN_DIM, TILE), F32),
                        pltpu.VMEM((N_HEADS, TILE), F32),
                        pltpu.VMEM((N_HEADS, TILE), F32),
                        pltpu.VMEM((N_HEADS, TILE), F32),
                        pltpu.VMEM((N_HEADS, TILE, TILE), F32),
                        pltpu.VMEM((N_HEADS, TILE, TILE), BF16)],
        compiler_params=pltpu.CompilerParams(
            dimension_semantics=("arbitrary", "arbitrary"), vmem_limit_bytes=_vmem_limit(vmem)),
        name="attention",
    )(rel_bias, qvt, k, qvt, mask)


def _conv_body(cur_ref, halo_ref, dw_ref, dwb_ref, lng_ref, lnb_ref, o_ref, h_ref, y_ref):
    i = pl.program_id(1)

    def glu(x):
        return x[:, :CONV_CH].astype(F32) * jax.nn.sigmoid(x[:, CONV_CH:].astype(F32))

    h_ref[:CONV_HALO, :] = jnp.where(i > 0, glu(halo_ref[...]), 0.0)
    h_ref[CONV_HALO:, :] = glu(cur_ref[...])

    first = CONV_HALO - (CONV_WIDTH - 1)
    for r in range(CONV_TS // CONV_RB):
        for c in range(CONV_CH // CONV_CB):
            cols = slice(c * CONV_CB, (c + 1) * CONV_CB)
            acc = jnp.broadcast_to(dwb_ref[:, cols], (CONV_RB, CONV_CB))
            for j in range(CONV_WIDTH):
                acc = acc + h_ref[r * CONV_RB + first + j:r * CONV_RB + first + j + CONV_RB, cols] * dw_ref[j:j + 1, cols]
            y_ref[r * CONV_RB:(r + 1) * CONV_RB, cols] = acc

    y = y_ref[...]
    mu = jnp.mean(y, axis=-1, keepdims=True)
    var = jnp.mean(jnp.square(y - mu), axis=-1, keepdims=True)
    z = (y - mu) * lax.rsqrt(var + EPS) * lng_ref[...] + lnb_ref[...]
    o_ref[...] = (z * jax.nn.sigmoid(z)).astype(o_ref.dtype)


def _conv_module(glu_in, dw, dw_b, ln_g, ln_b, batch, seq):
    ns = seq // CONV_TS
    halo_per_tile = CONV_TS // CONV_HALO
    vec = pl.BlockSpec((1, CONV_CH), lambda b, i: (0, 0))
    vmem = (2 * (CONV_TS + CONV_HALO) * 2 * CONV_CH * 2 + 2 * CONV_TS * CONV_CH * 2
            + (2 * CONV_TS + CONV_HALO) * CONV_CH * 4 + 4 * CONV_TS * CONV_CH * 4)
    return pl.pallas_call(
        _conv_body,
        grid=(batch, ns),
        in_specs=[pl.BlockSpec((CONV_TS, 2 * CONV_CH), lambda b, i: (b * ns + i, 0)),
                  pl.BlockSpec((CONV_HALO, 2 * CONV_CH),
                               lambda b, i: (jnp.maximum((b * ns + i) * halo_per_tile - 1, 0), 0)),
                  pl.BlockSpec((CONV_WIDTH, CONV_CH), lambda b, i: (0, 0)),
                  vec, vec, vec],
        out_specs=pl.BlockSpec((CONV_TS, CONV_CH), lambda b, i: (b * ns + i, 0)),
        out_shape=jax.ShapeDtypeStruct((batch * seq, CONV_CH), BF16),
        scratch_shapes=[pltpu.VMEM((CONV_TS + CONV_HALO, CONV_CH), F32),
                        pltpu.VMEM((CONV_TS, CONV_CH), F32)],
        compiler_params=pltpu.CompilerParams(
            dimension_semantics=("arbitrary", "arbitrary"), vmem_limit_bytes=_vmem_limit(vmem)),
        name="conv_module",
    )(glu_in, glu_in, dw, dw_b, ln_g, ln_b)


def _mix_body(attn_ref, conv_ref, gates_ref, h_ref, wo_ref, wpw_ref, wout_ref, g_ref, o_ref):
    a = jnp.dot(attn_ref[...], wo_ref[...], preferred_element_type=F32)
    c = jnp.dot(conv_ref[...], wpw_ref[...], preferred_element_type=F32)
    mixed = (gates_ref[:, :D_MODEL].astype(F32) * a + gates_ref[:, D_MODEL:].astype(F32) * c).astype(BF16)
    y = jnp.dot(mixed, wout_ref[...], preferred_element_type=F32)
    o_ref[...] = h_ref[...] + _rms(y, g_ref[...])


def _mix_out(attn, conv, gates, h, w_o, w_pw2, w_out, post_g):
    t = h.shape[0]
    resident = pl.Buffered(1)

    def rows(n):
        return pl.BlockSpec((OUT_TM, n), lambda i: (i, 0))

    def whole(a):
        return pl.BlockSpec(a.shape, lambda i: (0, 0), pipeline_mode=resident)

    vmem = ((w_o.size + w_pw2.size + w_out.size) * 2
            + 2 * OUT_TM * (2 * ATTN_DIM * 2 + 2 * D_MODEL * 2 + 2 * D_MODEL * 4) + 5 * OUT_TM * D_MODEL * 4)
    return pl.pallas_call(
        _mix_body,
        grid=(t // OUT_TM,),
        in_specs=[rows(ATTN_DIM), rows(CONV_CH), rows(2 * D_MODEL), rows(D_MODEL),
                  whole(w_o), whole(w_pw2), whole(w_out), pl.BlockSpec((1, D_MODEL), lambda i: (0, 0))],
        out_specs=rows(D_MODEL),
        out_shape=jax.ShapeDtypeStruct((t, D_MODEL), F32),
        compiler_params=pltpu.CompilerParams(
            dimension_semantics=("arbitrary",), vmem_limit_bytes=_vmem_limit(vmem)),
        name="mix_out",
    )(attn, conv, gates, h, w_o, w_pw2, w_out, post_g)


def kernel(x, rel_bias, ffn1_pre_g, ffn1_wg, ffn1_wu, ffn1_wd, ffn1_post_g, mix_pre_g, w_in, b_gate, w_o,
           conv_dw, conv_dw_b, conv_ln_g, conv_ln_b, w_pw2, w_out, mix_post_g, ffn2_pre_g, ffn2_wg, ffn2_wu,
           ffn2_wd, ffn2_post_g):
    batch, seq, d = x.shape
    depth = ffn1_wg.shape[0]
    h = x.reshape(batch * seq, d)
    o_k = ATTN_DIM
    o_v = o_k + ATTN_DIM
    o_qi = o_v + ATTN_DIM
    o_ki = o_qi + IDX_HEADS * IDX_DIM
    o_wi = o_ki + IDX_DIM
    o_glu = o_wi + IDX_HEADS
    o_gates = o_glu + 2 * CONV_CH
    for l in range(depth):
        wl = w_in[l]
        w_t = jnp.concatenate([wl[:, :o_k] * (HEAD_DIM ** -0.5 * LOG2E), wl[:, o_v:o_qi], wl[:, o_qi:o_ki]],
                              axis=1).T.astype(BF16)
        w_wt = wl[:, o_wi:o_glu].T.astype(BF16)
        w_k = wl[:, o_k:o_v].astype(BF16)
        w_ki = jnp.pad(wl[:, o_ki:o_wi], ((0, 0), (0, LANES - IDX_DIM))).astype(BF16)
        w_glu = wl[:, o_glu:o_gates].astype(BF16)
        w_gates = wl[:, o_gates:].astype(BF16)

        h, u = _ffn(h, ffn1_pre_g[l][None], ffn1_wg[l].astype(BF16), ffn1_wu[l].astype(BF16),
                    ffn1_wd[l].astype(BF16), ffn1_post_g[l][None], mix_pre_g[l][None])

        qvt, wt = _proj_t(u, w_t, w_wt)
        k = _proj(u, w_k, BF16, "proj_k")
        kw = _proj(u, w_ki, F32, "proj_ki")
        glu_in = _proj(u, w_glu, BF16, "proj_glu")
        gates = _proj(u, w_gates, BF16, "proj_gates", bias=b_gate[l][None])

        mask = _indexer(qvt, wt, kw, batch, seq)
        attn = _attention(qvt, k, mask, rel_bias, batch, seq)
        conv = _conv_module(glu_in, conv_dw[l], conv_dw_b[l][None], conv_ln_g[l][None], conv_ln_b[l][None],
                            batch, seq)
        h = _mix_out(attn, conv, gates, h, w_o[l].astype(BF16), w_pw2[l].astype(BF16),
                     w_out[l].astype(BF16), mix_post_g[l][None])

        h = _ffn(h, ffn2_pre_g[l][None], ffn2_wg[l].astype(BF16), ffn2_wu[l].astype(BF16),
                 ffn2_wd[l].astype(BF16), ffn2_post_g[l][None])
    return h.reshape(batch, seq, d)
```

```python
import functools
import math

import jax
import jax.numpy as jnp
from jax import lax
from jax.experimental import pallas as pl
from jax.experimental.pallas import tpu as pltpu

D_MODEL = 2048
N_HEADS = 8
HEAD_DIM = 128
ATTN_DIM = N_HEADS * HEAD_DIM
IDX_HEADS = 16
IDX_DIM = 64
TOPK_MAX = 256
NUM_BUCKETS = 32
MAX_DISTANCE = 128
CONV_CH = 1024
CONV_WIDTH = 31
D_FF = 5632
FFN_RES_WEIGHT = 0.5
EPS = 1e-6

F32 = jnp.float32
BF16 = jnp.bfloat16
LANES = 128
SUBLANES = 8
V7X_VMEM_BYTES = 64 * 1024 * 1024
MASKED = -1e30
INT_MIN = -(2 ** 31)
I16 = jnp.int16
I16_MIN = -(2 ** 15)
PACKED_ROWS = 16
LOG2E = math.log2(math.e)

FFN_TM, FFN_TF = 512, 512
PROJ_TM = 1024
TILE = 256
PROJT_ROWS = 512
COUNT_TILES = 4
CONV_TS = 512
CONV_HALO = 32
CONV_RB, CONV_CB = 128, 256
OUT_TM = 256

_NT = (((1,), (1,)), ((), ()))


def _vmem_limit(nbytes):
    return int(min(nbytes + (8 << 20), V7X_VMEM_BYTES - (4 << 20)))


def _rms(x, g):
    y = x * lax.rsqrt(jnp.mean(x * x, axis=-1, keepdims=True) + EPS)
    return y * g


def _ffn_body(emit_next, x_ref, pre_g_ref, wg_ref, wu_ref, wd_ref, post_g_ref, *rest):
    if emit_next:
        next_g_ref, out_ref, u_ref, xn_ref, acc_ref = rest
    else:
        out_ref, xn_ref, acc_ref = rest
    j = pl.program_id(1)

    @pl.when(j == 0)
    def _():
        xn_ref[...] = _rms(x_ref[...], pre_g_ref[...]).astype(BF16)
        acc_ref[...] = jnp.zeros_like(acc_ref)

    xn = xn_ref[...]
    g = jnp.dot(xn, wg_ref[...], preferred_element_type=F32)
    u = jnp.dot(xn, wu_ref[...], preferred_element_type=F32)
    a = (g * jax.nn.sigmoid(g) * u).astype(BF16)
    acc_ref[...] += jnp.dot(a, wd_ref[...], preferred_element_type=F32)

    @pl.when(j == pl.num_programs(1) - 1)
    def _():
        h = x_ref[...] + FFN_RES_WEIGHT * _rms(acc_ref[...], post_g_ref[...])
        out_ref[...] = h
        if emit_next:
            u_ref[...] = _rms(h, next_g_ref[...]).astype(BF16)


def _ffn(x, pre_g, wg, wu, wd, post_g, next_g=None):
    t, d = x.shape
    dff = wg.shape[1]
    emit_next = next_g is not None
    row = pl.BlockSpec((FFN_TM, d), lambda i, j: (i, 0))
    gain = pl.BlockSpec((1, d), lambda i, j: (0, 0))
    in_specs = [row, gain,
                pl.BlockSpec((d, FFN_TF), lambda i, j: (0, j)),
                pl.BlockSpec((d, FFN_TF), lambda i, j: (0, j)),
                pl.BlockSpec((FFN_TF, d), lambda i, j: (j, 0)),
                gain]
    args = [x, pre_g, wg, wu, wd, post_g]
    out_shape = [jax.ShapeDtypeStruct((t, d), F32)]
    out_specs = [row]
    if emit_next:
        in_specs.append(gain)
        args.append(next_g)
        out_shape.append(jax.ShapeDtypeStruct((t, d), BF16))
        out_specs.append(row)
    vmem = (2 * FFN_TM * d * 4 * 2 + 2 * FFN_TM * d * 2 + FFN_TM * d * (2 + 4)
            + 2 * 3 * d * FFN_TF * 2 + 3 * FFN_TM * FFN_TF * 4)
    res = pl.pallas_call(
        functools.partial(_ffn_body, emit_next),
        grid=(t // FFN_TM, dff // FFN_TF),
        in_specs=in_specs, out_specs=out_specs, out_shape=out_shape,
        scratch_shapes=[pltpu.VMEM((FFN_TM, d), BF16), pltpu.VMEM((FFN_TM, d), F32)],
        compiler_params=pltpu.CompilerParams(
            dimension_semantics=("arbitrary", "arbitrary"), vmem_limit_bytes=_vmem_limit(vmem)),
        name="ffn_next" if emit_next else "ffn",
    )(*args)
    return res if emit_next else res[0]


def _proj_body(gate, x_ref, w_ref, *rest):
    if gate:
        b_ref, o_ref = rest
    else:
        (o_ref,) = rest
    y = jnp.dot(x_ref[...], w_ref[...], preferred_element_type=F32)
    if gate:
        y = jax.nn.sigmoid(y + b_ref[...])
    o_ref[...] = y.astype(o_ref.dtype)


def _proj(x, w, out_dtype, name, bias=None):
    t, k = x.shape
    n = w.shape[1]
    tn = min(n, 512)
    gate = bias is not None
    in_specs = [pl.BlockSpec((PROJ_TM, k), lambda i, j: (i, 0)),
                pl.BlockSpec((k, tn), lambda i, j: (0, j))]
    args = [x, w]
    if gate:
        in_specs.append(pl.BlockSpec((1, tn), lambda i, j: (0, j)))
        args.append(bias)
    vmem = 2 * PROJ_TM * k * 2 + 2 * k * tn * 2 + 3 * PROJ_TM * tn * 4
    return pl.pallas_call(
        functools.partial(_proj_body, gate),
        grid=(t // PROJ_TM, n // tn),
        in_specs=in_specs,
        out_specs=pl.BlockSpec((PROJ_TM, tn), lambda i, j: (i, j)),
        out_shape=jax.ShapeDtypeStruct((t, n), out_dtype),
        compiler_params=pltpu.CompilerParams(
            dimension_semantics=("arbitrary", "arbitrary"), vmem_limit_bytes=_vmem_limit(vmem)),
        name=name,
    )(*args)


def _proj_t_body(u_ref, wt_ref, wwt_ref, o_ref, w_ref):
    u = u_ref[...]
    for r in range(wt_ref.shape[0] // PROJT_ROWS):
        rows = slice(r * PROJT_ROWS, (r + 1) * PROJT_ROWS)
        o_ref[0, rows, :] = lax.dot_general(wt_ref[rows, :], u, _NT, preferred_element_type=F32).astype(BF16)
    w_ref[0] = lax.dot_general(wwt_ref[...], u, _NT, preferred_element_type=F32)


def _proj_t(u, wt, wwt):
    t, k = u.shape
    n, nw = wt.shape[0], wwt.shape[0]
    resident = pl.Buffered(1)
    vmem = 2 * TILE * k * 2 + (n + nw) * k * 2 + 2 * (n * 2 + nw * 4) * TILE + 2 * PROJT_ROWS * TILE * 4
    return pl.pallas_call(
        _proj_t_body,
        grid=(t // TILE,),
        in_specs=[pl.BlockSpec((TILE, k), lambda i: (i, 0)),
                  pl.BlockSpec((n, k), lambda i: (0, 0), pipeline_mode=resident),
                  pl.BlockSpec((nw, k), lambda i: (0, 0), pipeline_mode=resident)],
        out_specs=[pl.BlockSpec((1, n, TILE), lambda i: (i, 0, 0)),
                   pl.BlockSpec((1, nw, TILE), lambda i: (i, 0, 0))],
        out_shape=[jax.ShapeDtypeStruct((t // TILE, n, TILE), BF16),
                   jax.ShapeDtypeStruct((t // TILE, nw, TILE), F32)],
        compiler_params=pltpu.CompilerParams(
            dimension_semantics=("arbitrary",), vmem_limit_bytes=_vmem_limit(vmem)),
        name="proj_t",
    )(u, wt, wwt)


def _indexer_body(seq, qi_ref, w_ref, kw_ref, mask_ref, kb_ref, key_ref, hi_ref, lo_ref):
    i = pl.program_id(1)
    nch = i + 1
    n_tiles = seq // TILE

    @pl.when(i == 0)
    def _():
        kb_ref[...] = kw_ref[:, :IDX_DIM].astype(BF16)

    w = w_ref[0] * (IDX_HEADS ** -0.5 * IDX_DIM ** -0.5)
    q_pos = i * TILE + lax.broadcasted_iota(jnp.int32, (1, TILE), 1)

    def rows_of(c):
        return pl.ds(pl.multiple_of(c * TILE, TILE), TILE)

    def score_chunk(c, carry):
        ks = kb_ref[rows_of(c), :]
        acc = jnp.zeros((TILE, TILE), F32)
        for h in range(IDX_HEADS):
            d = jnp.dot(ks, qi_ref[0, h * IDX_DIM:(h + 1) * IDX_DIM, :], preferred_element_type=F32)
            acc = acc + jnp.maximum(d, 0.0) * w[h:h + 1, :]
        bits = pltpu.bitcast(acc, jnp.int32)
        key = bits ^ ((bits >> 31) & jnp.int32(0x7FFFFFFF))
        k_pos = c * TILE + lax.broadcasted_iota(jnp.int32, (TILE, 1), 0)
        key = jnp.where(k_pos <= q_pos, key, jnp.int32(INT_MIN))
        key_ref[rows_of(c), :] = key
        hi_ref[rows_of(c), :] = (key >> 16).astype(I16)
        return carry

    lax.fori_loop(0, nch, score_chunk, 0)

    n_steps = (nch + COUNT_TILES - 1) // COUNT_TILES
    step_rows = COUNT_TILES * TILE
    groups = step_rows // PACKED_ROWS

    def rows_of_step(c):
        return pl.ds(pl.multiple_of(c * step_rows, step_rows), step_rows)

    def pad_chunk(c, carry):
        key_ref[rows_of(c), :] = jnp.full((TILE, TILE), INT_MIN, jnp.int32)
        hi_ref[rows_of(c), :] = jnp.full((TILE, TILE), I16_MIN, I16)
        return carry

    lax.fori_loop(nch, n_steps * COUNT_TILES, pad_chunk, 0)

    def search16(src_ref, need):
        def search_pass(b, u):
            cand_u = u | lax.shift_left(jnp.int32(1), 15 - b)
            cand = jnp.broadcast_to(cand_u + I16_MIN, (PACKED_ROWS, TILE)).astype(I16)

            def count_chunk(c, cnt):
                x = src_ref[rows_of_step(c), :].reshape(groups, PACKED_ROWS, TILE)
                ge = jnp.where(x >= cand[None], jnp.int16(1), jnp.int16(0))
                parts = [ge[g] for g in range(4)]
                for g in range(4, groups):
                    parts[g % 4] = parts[g % 4] + ge[g]
                return cnt + ((parts[0] + parts[1]) + (parts[2] + parts[3]))

            cnt = lax.fori_loop(0, n_steps, count_chunk, jnp.zeros((PACKED_ROWS, TILE), I16))
            total = jnp.sum(cnt.astype(jnp.int32), axis=0, keepdims=True)
            return jnp.where(total >= need, cand_u, u)

        return lax.fori_loop(0, 16, search_pass, jnp.zeros((1, TILE), jnp.int32))

    t_hi = search16(hi_ref, jnp.int32(TOPK_MAX)) + I16_MIN

    def low_half_chunk(c, n_gt):
        key = key_ref[rows_of_step(c), :]
        hi = key >> 16
        lo = (key & jnp.int32(0xFFFF)) + I16_MIN
        lo_ref[rows_of_step(c), :] = jnp.where(hi == t_hi, lo, I16_MIN).astype(I16)
        gt = jnp.where(hi > t_hi, 1, 0).reshape(step_rows // SUBLANES, SUBLANES, TILE)
        return n_gt + jnp.sum(gt, axis=0)

    n_gt = lax.fori_loop(0, n_steps, low_half_chunk, jnp.zeros((SUBLANES, TILE), jnp.int32))
    need_lo = TOPK_MAX - jnp.sum(n_gt, axis=0, keepdims=True)
    u_lo = search16(lo_ref, need_lo)
    thr = jnp.maximum(t_hi * 65536 + u_lo, jnp.int32(INT_MIN + 1))

    def write_chunk(c, carry):
        mask_ref[0, 0, rows_of(c), :] = jnp.where(key_ref[rows_of(c), :] >= thr, 0.0, MASKED).astype(BF16)
        return carry

    lax.fori_loop(0, nch, write_chunk, 0)

    def fill_chunk(c, carry):
        mask_ref[0, 0, rows_of(c), :] = jnp.full((TILE, TILE), MASKED, BF16)
        return carry

    lax.fori_loop(nch, n_tiles, fill_chunk, 0)


def _indexer(qvt, wt, kw, batch, seq):
    nq = seq // TILE
    qi_block = (3 * ATTN_DIM) // (IDX_HEADS * IDX_DIM) - 1
    vmem = (2 * IDX_HEADS * IDX_DIM * TILE * 2 + 2 * seq * LANES * 4 + 2 * seq * TILE * 2
            + seq * LANES * 2 + seq * TILE * (4 + 2 + 2) + 8 * TILE * TILE * 4)
    return pl.pallas_call(
        functools.partial(_indexer_body, seq),
        grid=(batch, nq),
        in_specs=[pl.BlockSpec((1, IDX_HEADS * IDX_DIM, TILE), lambda b, i: (b * nq + i, qi_block, 0)),
                  pl.BlockSpec((1, IDX_HEADS, TILE), lambda b, i: (b * nq + i, 0, 0)),
                  pl.BlockSpec((seq, LANES), lambda b, i: (b, 0))],
        out_specs=pl.BlockSpec((1, 1, seq, TILE), lambda b, i: (b, i, 0, 0)),
        out_shape=jax.ShapeDtypeStruct((batch, nq, seq, TILE), BF16),
        scratch_shapes=[pltpu.VMEM((seq, IDX_DIM), BF16),
                        pltpu.VMEM((seq, TILE), jnp.int32),
                        pltpu.VMEM((seq, TILE), I16),
                        pltpu.VMEM((seq, TILE), I16)],
        compiler_params=pltpu.CompilerParams(
            dimension_semantics=("arbitrary", "arbitrary"), vmem_limit_bytes=_vmem_limit(vmem)),
        name="indexer",
    )(qvt, wt, kw)


def _bias_init(rb_ref, bias_ref):
    shape = (2 * TILE, TILE)
    dist = (lax.broadcasted_iota(jnp.int32, shape, 1) - lax.broadcasted_iota(jnp.int32, shape, 0) + TILE)
    max_exact = NUM_BUCKETS // 2
    n = jnp.maximum(dist, 0)
    nf = jnp.maximum(n, max_exact).astype(F32)
    large = max_exact + (jnp.log(nf / max_exact) / math.log(MAX_DISTANCE / max_exact)
                         * (NUM_BUCKETS - max_exact)).astype(jnp.int32)
    large = jnp.minimum(large, NUM_BUCKETS - 1)
    bucket = jnp.where(n < max_exact, n, large)
    for h in range(N_HEADS):
        far = rb_ref[NUM_BUCKETS - 1, h]
        acc = jnp.zeros(shape, F32)
        for b in range(NUM_BUCKETS - 1):
            acc = jnp.where(bucket == b, (rb_ref[b, h] - far) * LOG2E, acc)
        bias_ref[h] = acc


def _attn_body(rb_ref, q_ref, k_ref, vt_ref, mask_ref, o_ref, bias_ref, acc_ref, m_ref, l_ref, alpha_ref,
               s_ref, p_ref):
    b = pl.program_id(0)
    i = pl.program_id(1)

    @pl.when((b == 0) & (i == 0))
    def _():
        _bias_init(rb_ref, bias_ref)

    acc_ref[...] = jnp.zeros_like(acc_ref)
    m_ref[...] = jnp.full_like(m_ref, MASKED)
    l_ref[...] = jnp.zeros_like(l_ref)

    def tile(j, near):
        rows = pl.ds(pl.multiple_of(j * TILE, TILE), TILE)
        maskf = mask_ref[0, 0, rows, :].astype(F32)

        for h in range(N_HEADS):
            feat = slice(h * HEAD_DIM, (h + 1) * HEAD_DIM)
            s = jnp.dot(k_ref[rows, feat], q_ref[0, feat, :], preferred_element_type=F32) + maskf
            if near is not None:
                s = s + bias_ref[h, near * TILE:(near + 1) * TILE, :]
            s_ref[h] = s
        for h in range(N_HEADS):
            s = s_ref[h]
            m_old = m_ref[h:h + 1, :]
            m_new = jnp.maximum(m_old, jnp.max(s, axis=0, keepdims=True))
            alpha = jnp.exp2(m_old - m_new)
            p = jnp.exp2(s - m_new)
            l_ref[h:h + 1, :] = alpha * l_ref[h:h + 1, :] + jnp.sum(p, axis=0, keepdims=True)
            p_ref[h] = p.astype(BF16)
            alpha_ref[h:h + 1, :] = alpha
            m_ref[h:h + 1, :] = m_new
        for h in range(N_HEADS):
            feat = slice(h * HEAD_DIM, (h + 1) * HEAD_DIM)
            pv = jnp.dot(vt_ref[j, feat, :], p_ref[h], preferred_element_type=F32)
            acc_ref[feat, :] = alpha_ref[h:h + 1, :] * acc_ref[feat, :] + pv

    def far_tile(j, carry):
        tile(j, None)
        return carry

    lax.fori_loop(0, i - 1, far_tile, 0)

    @pl.when(i >= 1)
    def _():
        tile(i - 1, 0)

    tile(i, 1)

    for h in range(N_HEADS):
        feat = slice(h * HEAD_DIM, (h + 1) * HEAD_DIM)
        o_ref[:, feat] = (acc_ref[feat, :] / l_ref[h:h + 1, :]).T.astype(o_ref.dtype)


def _attention(qvt, k, mask, rel_bias, batch, seq):
    nq = seq // TILE
    vmem = (2 * ATTN_DIM * TILE * 2 * 2 + 2 * seq * ATTN_DIM * 2 + 2 * seq * TILE * 2
            + N_HEADS * 2 * TILE * TILE * 4 + ATTN_DIM * TILE * 4 + N_HEADS * TILE * TILE * (4 + 2)
            + 4 * TILE * TILE * 4)
    resident = pl.Buffered(1)
    return pl.pallas_call(
        _attn_body,
        grid=(batch, nq),
        in_specs=[pl.BlockSpec(memory_space=pltpu.SMEM),
                  pl.BlockSpec((1, ATTN_DIM, TILE), lambda b, i: (b * nq + i, 0, 0)),
                  pl.BlockSpec((seq, ATTN_DIM), lambda b, i: (b, 0), pipeline_mode=resident),
                  pl.BlockSpec((nq, ATTN_DIM, TILE), lambda b, i: (b, 1, 0), pipeline_mode=resident),
                  pl.BlockSpec((1, 1, seq, TILE), lambda b, i: (b, i, 0, 0))],
        out_specs=pl.BlockSpec((TILE, ATTN_DIM), lambda b, i: (b * nq + i, 0)),
        out_shape=jax.ShapeDtypeStruct((batch * seq, ATTN_DIM), BF16),
        scratch_shapes=[pltpu.VMEM((N_HEADS, 2 * TILE, TILE), F32),
                        pltpu.VMEM((ATTN_DIM, TILE), F32),
                        pltpu.VMEM((N_HEADS, TILE), F32),
                        pltpu.VMEM((N_HEADS, TILE), F32),
                        pltpu.VMEM((N_HEADS, TILE), F32),
                        pltpu.VMEM((N_HEADS, TILE, TILE), F32),
                        pltpu.VMEM((N_HEADS, TILE, TILE), BF16)],
        compiler_params=pltpu.CompilerParams(
            dimension_semantics=("arbitrary", "arbitrary"), vmem_limit_bytes=_vmem_limit(vmem)),
        name="attention",
    )(rel_bias, qvt, k, qvt, mask)


def _conv_body(cur_ref, halo_ref, dw_ref, dwb_ref, lng_ref, lnb_ref, o_ref, h_ref, y_ref):
    i = pl.program_id(1)

    def glu(x):
        return x[:, :CONV_CH].astype(F32) * jax.nn.sigmoid(x[:, CONV_CH:].astype(F32))

    h_ref[:CONV_HALO, :] = jnp.where(i > 0, glu(halo_ref[...]), 0.0)
    h_ref[CONV_HALO:, :] = glu(cur_ref[...])

    first = CONV_HALO - (CONV_WIDTH - 1)
    for r in range(CONV_TS // CONV_RB):
        for c in range(CONV_CH // CONV_CB):
            cols = slice(c * CONV_CB, (c + 1) * CONV_CB)
            acc = jnp.broadcast_to(dwb_ref[:, cols], (CONV_RB, CONV_CB))
            for j in range(CONV_WIDTH):
                acc = acc + h_ref[r * CONV_RB + first + j:r * CONV_RB + first + j + CONV_RB, cols] * dw_ref[j:j + 1, cols]
            y_ref[r * CONV_RB:(r + 1) * CONV_RB, cols] = acc

    y = y_ref[...]
    mu = jnp.mean(y, axis=-1, keepdims=True)
    var = jnp.mean(jnp.square(y - mu), axis=-1, keepdims=True)
    z = (y - mu) * lax.rsqrt(var + EPS) * lng_ref[...] + lnb_ref[...]
    o_ref[...] = (z * jax.nn.sigmoid(z)).astype(o_ref.dtype)


def _conv_module(glu_in, dw, dw_b, ln_g, ln_b, batch, seq):
    ns = seq // CONV_TS
    halo_per_tile = CONV_TS // CONV_HALO
    vec = pl.BlockSpec((1, CONV_CH), lambda b, i: (0, 0))
    vmem = (2 * (CONV_TS + CONV_HALO) * 2 * CONV_CH * 2 + 2 * CONV_TS * CONV_CH * 2
            + (2 * CONV_TS + CONV_HALO) * CONV_CH * 4 + 4 * CONV_TS * CONV_CH * 4)
    return pl.pallas_call(
        _conv_body,
        grid=(batch, ns),
        in_specs=[pl.BlockSpec((CONV_TS, 2 * CONV_CH), lambda b, i: (b * ns + i, 0)),
                  pl.BlockSpec((CONV_HALO, 2 * CONV_CH),
                               lambda b, i: (jnp.maximum((b * ns + i) * halo_per_tile - 1, 0), 0)),
                  pl.BlockSpec((CONV_WIDTH, CONV_CH), lambda b, i: (0, 0)),
                  vec, vec, vec],
        out_specs=pl.BlockSpec((CONV_TS, CONV_CH), lambda b, i: (b * ns + i, 0)),
        out_shape=jax.ShapeDtypeStruct((batch * seq, CONV_CH), BF16),
        scratch_shapes=[pltpu.VMEM((CONV_TS + CONV_HALO, CONV_CH), F32),
                        pltpu.VMEM((CONV_TS, CONV_CH), F32)],
        compiler_params=pltpu.CompilerParams(
            dimension_semantics=("arbitrary", "arbitrary"), vmem_limit_bytes=_vmem_limit(vmem)),
        name="conv_module",
    )(glu_in, glu_in, dw, dw_b, ln_g, ln_b)


def _mix_body(attn_ref, conv_ref, gates_ref, h_ref, wo_ref, wpw_ref, wout_ref, g_ref, o_ref):
    a = jnp.dot(attn_ref[...], wo_ref[...], preferred_element_type=F32)
    c = jnp.dot(conv_ref[...], wpw_ref[...], preferred_element_type=F32)
    mixed = (gates_ref[:, :D_MODEL].astype(F32) * a + gates_ref[:, D_MODEL:].astype(F32) * c).astype(BF16)
    y = jnp.dot(mixed, wout_ref[...], preferred_element_type=F32)
    o_ref[...] = h_ref[...] + _rms(y, g_ref[...])


def _mix_out(attn, conv, gates, h, w_o, w_pw2, w_out, post_g):
    t = h.shape[0]
    resident = pl.Buffered(1)

    def rows(n):
        return pl.BlockSpec((OUT_TM, n), lambda i: (i, 0))

    def whole(a):
        return pl.BlockSpec(a.shape, lambda i: (0, 0), pipeline_mode=resident)

    vmem = ((w_o.size + w_pw2.size + w_out.size) * 2
            + 2 * OUT_TM * (2 * ATTN_DIM * 2 + 2 * D_MODEL * 2 + 2 * D_MODEL * 4) + 5 * OUT_TM * D_MODEL * 4)
    return pl.pallas_call(
        _mix_body,
        grid=(t // OUT_TM,),
        in_specs=[rows(ATTN_DIM), rows(CONV_CH), rows(2 * D_MODEL), rows(D_MODEL),
                  whole(w_o), whole(w_pw2), whole(w_out), pl.BlockSpec((1, D_MODEL), lambda i: (0, 0))],
        out_specs=rows(D_MODEL),
        out_shape=jax.ShapeDtypeStruct((t, D_MODEL), F32),
        compiler_params=pltpu.CompilerParams(
            dimension_semantics=("arbitrary",), vmem_limit_bytes=_vmem_limit(vmem)),
        name="mix_out",
    )(attn, conv, gates, h, w_o, w_pw2, w_out, post_g)


def kernel(x, rel_bias, ffn1_pre_g, ffn1_wg, ffn1_wu, ffn1_wd, ffn1_post_g, mix_pre_g, w_in, b_gate, w_o,
           conv_dw, conv_dw_b, conv_ln_g, conv_ln_b, w_pw2, w_out, mix_post_g, ffn2_pre_g, ffn2_wg, ffn2_wu,
           ffn2_wd, ffn2_post_g):
    batch, seq, d = x.shape
    depth = ffn1_wg.shape[0]
    h = x.reshape(batch * seq, d)
    o_k = ATTN_DIM
    o_v = o_k + ATTN_DIM
    o_qi = o_v + ATTN_DIM
    o_ki = o_qi + IDX_HEADS * IDX_DIM
    o_wi = o_ki + IDX_DIM
    o_glu = o_wi + IDX_HEADS
    o_gates = o_glu + 2 * CONV_CH
    for l in range(depth):
        wl = w_in[l]
        w_t = jnp.concatenate([wl[:, :o_k] * (HEAD_DIM ** -0.5 * LOG2E), wl[:, o_v:o_qi], wl[:, o_qi:o_ki]],
                              axis=1).T.astype(BF16)
        w_wt = wl[:, o_wi:o_glu].T.astype(BF16)
        w_k = wl[:, o_k:o_v].astype(BF16)
        w_ki = jnp.pad(wl[:, o_ki:o_wi], ((0, 0), (0, LANES - IDX_DIM))).astype(BF16)
        w_glu = wl[:, o_glu:o_gates].astype(BF16)
        w_gates = wl[:, o_gates:].astype(BF16)

        h, u = _ffn(h, ffn1_pre_g[l][None], ffn1_wg[l].astype(BF16), ffn1_wu[l].astype(BF16),
                    ffn1_wd[l].astype(BF16), ffn1_post_g[l][None], mix_pre_g[l][None])

        qvt, wt = _proj_t(u, w_t, w_wt)
        k = _proj(u, w_k, BF16, "proj_k")
        kw = _proj(u, w_ki, F32, "proj_ki")
        glu_in = _proj(u, w_glu, BF16, "proj_glu")
        gates = _proj(u, w_gates, BF16, "proj_gates", bias=b_gate[l][None])

        mask = _indexer(qvt, wt, kw, batch, seq)
        attn = _attention(qvt, k, mask, rel_bias, batch, seq)
        conv = _conv_module(glu_in, conv_dw[l], conv_dw_b[l][None], conv_ln_g[l][None], conv_ln_b[l][None],
                            batch, seq)
        h = _mix_out(attn, conv, gates, h, w_o[l].astype(BF16), w_pw2[l].astype(BF16),
                     w_out[l].astype(BF16), mix_post_g[l][None])

        h = _ffn(h, ffn2_pre_g[l][None], ffn2_wg[l].astype(BF16), ffn2_wu[l].astype(BF16),
                 ffn2_wd[l].astype(BF16), ffn2_post_g[l][None])
    return h.reshape(batch, seq, d)
```

```python
import functools
import math

import jax
import jax.numpy as jnp
from jax import lax
from jax.experimental import pallas as pl
from jax.experimental.pallas import tpu as pltpu

D_MODEL = 2048
N_HEADS = 8
HEAD_DIM = 128
ATTN_DIM = N_HEADS * HEAD_DIM
IDX_HEADS = 16
IDX_DIM = 64
TOPK_MAX = 256
NUM_BUCKETS = 32
MAX_DISTANCE = 128
CONV_CH = 1024
CONV_WIDTH = 31
D_FF = 5632
FFN_RES_WEIGHT = 0.5
EPS = 1e-6

F32 = jnp.float32
BF16 = jnp.bfloat16
LANES = 128
SUBLANES = 8
V7X_VMEM_BYTES = 64 * 1024 * 1024
MASKED = -1e30
INT_MIN = -(2 ** 31)
I16 = jnp.int16
I16_MIN = -(2 ** 15)
PACKED_ROWS = 16
LOG2E = math.log2(math.e)

FFN_TM, FFN_TF = 512, 512
PROJ_TM = 1024
TILE = 256
PROJT_ROWS = 512
COUNT_TILES = 4
VT_ROWS = HEAD_DIM + PACKED_ROWS
CONV_TS = 512
CONV_HALO = 32
CONV_RB, CONV_CB = 128, 256
OUT_TM = 256

_NT = (((1,), (1,)), ((), ()))


def _vmem_limit(nbytes):
    return int(min(nbytes + (8 << 20), V7X_VMEM_BYTES - (4 << 20)))


def _rms(x, g):
    y = x * lax.rsqrt(jnp.mean(x * x, axis=-1, keepdims=True) + EPS)
    return y * g


def _ffn_body(emit_next, x_ref, pre_g_ref, wg_ref, wu_ref, wd_ref, post_g_ref, *rest):
    if emit_next:
        next_g_ref, out_ref, u_ref, xn_ref, acc_ref = rest
    else:
        out_ref, xn_ref, acc_ref = rest
    j = pl.program_id(1)

    @pl.when(j == 0)
    def _():
        xn_ref[...] = _rms(x_ref[...], pre_g_ref[...]).astype(BF16)
        acc_ref[...] = jnp.zeros_like(acc_ref)

    xn = xn_ref[...]
    g = jnp.dot(xn, wg_ref[...], preferred_element_type=F32)
    u = jnp.dot(xn, wu_ref[...], preferred_element_type=F32)
    a = (g * jax.nn.sigmoid(g) * u).astype(BF16)
    acc_ref[...] += jnp.dot(a, wd_ref[...], preferred_element_type=F32)

    @pl.when(j == pl.num_programs(1) - 1)
    def _():
        h = x_ref[...] + FFN_RES_WEIGHT * _rms(acc_ref[...], post_g_ref[...])
        out_ref[...] = h
        if emit_next:
            u_ref[...] = _rms(h, next_g_ref[...]).astype(BF16)


def _ffn(x, pre_g, wg, wu, wd, post_g, next_g=None):
    t, d = x.shape
    dff = wg.shape[1]
    emit_next = next_g is not None
    row = pl.BlockSpec((FFN_TM, d), lambda i, j: (i, 0))
    gain = pl.BlockSpec((1, d), lambda i, j: (0, 0))
    in_specs = [row, gain,
                pl.BlockSpec((d, FFN_TF), lambda i, j: (0, j)),
                pl.BlockSpec((d, FFN_TF), lambda i, j: (0, j)),
                pl.BlockSpec((FFN_TF, d), lambda i, j: (j, 0)),
                gain]
    args = [x, pre_g, wg, wu, wd, post_g]
    out_shape = [jax.ShapeDtypeStruct((t, d), F32)]
    out_specs = [row]
    if emit_next:
        in_specs.append(gain)
        args.append(next_g)
        out_shape.append(jax.ShapeDtypeStruct((t, d), BF16))
        out_specs.append(row)
    vmem = (2 * FFN_TM * d * 4 * 2 + 2 * FFN_TM * d * 2 + FFN_TM * d * (2 + 4)
            + 2 * 3 * d * FFN_TF * 2 + 3 * FFN_TM * FFN_TF * 4)
    res = pl.pallas_call(
        functools.partial(_ffn_body, emit_next),
        grid=(t // FFN_TM, dff // FFN_TF),
        in_specs=in_specs, out_specs=out_specs, out_shape=out_shape,
        scratch_shapes=[pltpu.VMEM((FFN_TM, d), BF16), pltpu.VMEM((FFN_TM, d), F32)],
        compiler_params=pltpu.CompilerParams(
            dimension_semantics=("arbitrary", "arbitrary"), vmem_limit_bytes=_vmem_limit(vmem)),
        name="ffn_next" if emit_next else "ffn",
    )(*args)
    return res if emit_next else res[0]


def _proj_body(gate, x_ref, w_ref, *rest):
    if gate:
        b_ref, o_ref = rest
    else:
        (o_ref,) = rest
    y = jnp.dot(x_ref[...], w_ref[...], preferred_element_type=F32)
    if gate:
        y = jax.nn.sigmoid(y + b_ref[...])
    o_ref[...] = y.astype(o_ref.dtype)


def _proj(x, w, out_dtype, name, bias=None):
    t, k = x.shape
    n = w.shape[1]
    tn = min(n, 512)
    gate = bias is not None
    in_specs = [pl.BlockSpec((PROJ_TM, k), lambda i, j: (i, 0)),
                pl.BlockSpec((k, tn), lambda i, j: (0, j))]
    args = [x, w]
    if gate:
        in_specs.append(pl.BlockSpec((1, tn), lambda i, j: (0, j)))
        args.append(bias)
    vmem = 2 * PROJ_TM * k * 2 + 2 * k * tn * 2 + 3 * PROJ_TM * tn * 4
    return pl.pallas_call(
        functools.partial(_proj_body, gate),
        grid=(t // PROJ_TM, n // tn),
        in_specs=in_specs,
        out_specs=pl.BlockSpec((PROJ_TM, tn), lambda i, j: (i, j)),
        out_shape=jax.ShapeDtypeStruct((t, n), out_dtype),
        compiler_params=pltpu.CompilerParams(
            dimension_semantics=("arbitrary", "arbitrary"), vmem_limit_bytes=_vmem_limit(vmem)),
        name=name,
    )(*args)


def _proj_t_body(u_ref, wt_ref, wvt_ref, wwt_ref, o_ref, v_ref, w_ref):
    u = u_ref[...]
    for r in range(wt_ref.shape[0] // PROJT_ROWS):
        rows = slice(r * PROJT_ROWS, (r + 1) * PROJT_ROWS)
        o_ref[0, rows, :] = lax.dot_general(wt_ref[rows, :], u, _NT, preferred_element_type=F32).astype(BF16)
    for h in range(N_HEADS):
        vt = lax.dot_general(wvt_ref[h * HEAD_DIM:(h + 1) * HEAD_DIM, :], u, _NT, preferred_element_type=F32)
        v_ref[0, h * VT_ROWS:h * VT_ROWS + HEAD_DIM, :] = vt.astype(BF16)
        v_ref[0, h * VT_ROWS + HEAD_DIM:(h + 1) * VT_ROWS, :] = jnp.ones((VT_ROWS - HEAD_DIM, TILE), BF16)
    w_ref[0] = lax.dot_general(wwt_ref[...], u, _NT, preferred_element_type=F32)


def _proj_t(u, wt, wvt, wwt):
    t, k = u.shape
    n, nw = wt.shape[0], wwt.shape[0]
    nv = N_HEADS * VT_ROWS
    resident = pl.Buffered(1)
    vmem = (2 * TILE * k * 2 + (n + wvt.shape[0] + nw) * k * 2 + 2 * ((n + nv) * 2 + nw * 4) * TILE
            + 2 * PROJT_ROWS * TILE * 4)
    return pl.pallas_call(
        _proj_t_body,
        grid=(t // TILE,),
        in_specs=[pl.BlockSpec((TILE, k), lambda i: (i, 0)),
                  pl.BlockSpec((n, k), lambda i: (0, 0), pipeline_mode=resident),
                  pl.BlockSpec(wvt.shape, lambda i: (0, 0), pipeline_mode=resident),
                  pl.BlockSpec((nw, k), lambda i: (0, 0), pipeline_mode=resident)],
        out_specs=[pl.BlockSpec((1, n, TILE), lambda i: (i, 0, 0)),
                   pl.BlockSpec((1, nv, TILE), lambda i: (i, 0, 0)),
                   pl.BlockSpec((1, nw, TILE), lambda i: (i, 0, 0))],
        out_shape=[jax.ShapeDtypeStruct((t // TILE, n, TILE), BF16),
                   jax.ShapeDtypeStruct((t // TILE, nv, TILE), BF16),
                   jax.ShapeDtypeStruct((t // TILE, nw, TILE), F32)],
        compiler_params=pltpu.CompilerParams(
            dimension_semantics=("arbitrary",), vmem_limit_bytes=_vmem_limit(vmem)),
        name="proj_t",
    )(u, wt, wvt, wwt)


def _indexer_body(seq, qi_ref, w_ref, kw_ref, mask_ref, kb_ref, key_ref, hi_ref, lo_ref):
    i = pl.program_id(1)
    nch = i + 1
    n_tiles = seq // TILE

    @pl.when(i == 0)
    def _():
        kb_ref[...] = kw_ref[:, :IDX_DIM].astype(BF16)

    w = w_ref[0] * (IDX_HEADS ** -0.5 * IDX_DIM ** -0.5)
    q_pos = i * TILE + lax.broadcasted_iota(jnp.int32, (1, TILE), 1)

    def rows_of(c):
        return pl.ds(pl.multiple_of(c * TILE, TILE), TILE)

    def score_chunk(c, carry):
        ks = kb_ref[rows_of(c), :]
        acc = jnp.zeros((TILE, TILE), F32)
        for h in range(IDX_HEADS):
            d = jnp.dot(ks, qi_ref[0, h * IDX_DIM:(h + 1) * IDX_DIM, :], preferred_element_type=F32)
            acc = acc + jnp.maximum(d, 0.0) * w[h:h + 1, :]
        bits = pltpu.bitcast(acc, jnp.int32)
        key = bits ^ ((bits >> 31) & jnp.int32(0x7FFFFFFF))
        k_pos = c * TILE + lax.broadcasted_iota(jnp.int32, (TILE, 1), 0)
        key = jnp.where(k_pos <= q_pos, key, jnp.int32(INT_MIN))
        key_ref[rows_of(c), :] = key
        hi_ref[rows_of(c), :] = (key >> 16).astype(I16)
        return carry

    lax.fori_loop(0, nch, score_chunk, 0)

    n_steps = (nch + COUNT_TILES - 1) // COUNT_TILES
    step_rows = COUNT_TILES * TILE
    groups = step_rows // PACKED_ROWS

    def rows_of_step(c):
        return pl.ds(pl.multiple_of(c * step_rows, step_rows), step_rows)

    def pad_chunk(c, carry):
        key_ref[rows_of(c), :] = jnp.full((TILE, TILE), INT_MIN, jnp.int32)
        hi_ref[rows_of(c), :] = jnp.full((TILE, TILE), I16_MIN, I16)
        return carry

    lax.fori_loop(nch, n_steps * COUNT_TILES, pad_chunk, 0)

    def search16(src_ref, need):
        def search_pass(b, u):
            cand_u = u | lax.shift_left(jnp.int32(1), 15 - b)
            cand = jnp.broadcast_to(cand_u + I16_MIN, (PACKED_ROWS, TILE)).astype(I16)

            def count_chunk(c, cnt):
                x = src_ref[rows_of_step(c), :].reshape(groups, PACKED_ROWS, TILE)
                ge = jnp.where(x >= cand[None], jnp.int16(1), jnp.int16(0))
                parts = [ge[g] for g in range(4)]
                for g in range(4, groups):
                    parts[g % 4] = parts[g % 4] + ge[g]
                return cnt + ((parts[0] + parts[1]) + (parts[2] + parts[3]))

            cnt = lax.fori_loop(0, n_steps, count_chunk, jnp.zeros((PACKED_ROWS, TILE), I16))
            total = jnp.sum(cnt.astype(jnp.int32), axis=0, keepdims=True)
            return jnp.where(total >= need, cand_u, u)

        return lax.fori_loop(0, 16, search_pass, jnp.zeros((1, TILE), jnp.int32))

    t_hi = search16(hi_ref, jnp.int32(TOPK_MAX)) + I16_MIN

    def low_half_chunk(c, n_gt):
        key = key_ref[rows_of_step(c), :]
        hi = key >> 16
        lo = (key & jnp.int32(0xFFFF)) + I16_MIN
        lo_ref[rows_of_step(c), :] = jnp.where(hi == t_hi, lo, I16_MIN).astype(I16)
        gt = jnp.where(hi > t_hi, 1, 0).reshape(step_rows // SUBLANES, SUBLANES, TILE)
        return n_gt + jnp.sum(gt, axis=0)

    n_gt = lax.fori_loop(0, n_steps, low_half_chunk, jnp.zeros((SUBLANES, TILE), jnp.int32))
    need_lo = TOPK_MAX - jnp.sum(n_gt, axis=0, keepdims=True)
    u_lo = search16(lo_ref, need_lo)
    thr = jnp.maximum(t_hi * 65536 + u_lo, jnp.int32(INT_MIN + 1))

    def write_chunk(c, carry):
        mask_ref[0, 0, rows_of(c), :] = jnp.where(key_ref[rows_of(c), :] >= thr, 0.0, MASKED).astype(BF16)
        return carry

    lax.fori_loop(0, nch, write_chunk, 0)

    def fill_chunk(c, carry):
        mask_ref[0, 0, rows_of(c), :] = jnp.full((TILE, TILE), MASKED, BF16)
        return carry

    lax.fori_loop(nch, n_tiles, fill_chunk, 0)


def _indexer(qt, wt, kw, batch, seq):
    nq = seq // TILE
    qi_block = 1
    vmem = (2 * IDX_HEADS * IDX_DIM * TILE * 2 + 2 * seq * LANES * 4 + 2 * seq * TILE * 2
            + seq * LANES * 2 + seq * TILE * (4 + 2 + 2) + 8 * TILE * TILE * 4)
    return pl.pallas_call(
        functools.partial(_indexer_body, seq),
        grid=(batch, nq),
        in_specs=[pl.BlockSpec((1, IDX_HEADS * IDX_DIM, TILE), lambda b, i: (b * nq + i, qi_block, 0)),
                  pl.BlockSpec((1, IDX_HEADS, TILE), lambda b, i: (b * nq + i, 0, 0)),
                  pl.BlockSpec((seq, LANES), lambda b, i: (b, 0))],
        out_specs=pl.BlockSpec((1, 1, seq, TILE), lambda b, i: (b, i, 0, 0)),
        out_shape=jax.ShapeDtypeStruct((batch, nq, seq, TILE), BF16),
        scratch_shapes=[pltpu.VMEM((seq, IDX_DIM), BF16),
                        pltpu.VMEM((seq, TILE), jnp.int32),
                        pltpu.VMEM((seq, TILE), I16),
                        pltpu.VMEM((seq, TILE), I16)],
        compiler_params=pltpu.CompilerParams(
            dimension_semantics=("arbitrary", "arbitrary"), vmem_limit_bytes=_vmem_limit(vmem)),
        name="indexer",
    )(qt, wt, kw)


def _bias_init(rb_ref, bias_ref):
    shape = (2 * TILE, TILE)
    dist = (lax.broadcasted_iota(jnp.int32, shape, 1) - lax.broadcasted_iota(jnp.int32, shape, 0) + TILE)
    max_exact = NUM_BUCKETS // 2
    n = jnp.maximum(dist, 0)
    nf = jnp.maximum(n, max_exact).astype(F32)
    large = max_exact + (jnp.log(nf / max_exact) / math.log(MAX_DISTANCE / max_exact)
                         * (NUM_BUCKETS - max_exact)).astype(jnp.int32)
    large = jnp.minimum(large, NUM_BUCKETS - 1)
    bucket = jnp.where(n < max_exact, n, large)
    for h in range(N_HEADS):
        far = rb_ref[NUM_BUCKETS - 1, h]
        acc = jnp.zeros(shape, F32)
        for b in range(NUM_BUCKETS - 1):
            acc = jnp.where(bucket == b, (rb_ref[b, h] - far) * LOG2E, acc)
        bias_ref[h] = acc


def _attn_body(rb_ref, q_ref, k_ref, vt_ref, mask_ref, o_ref, bias_ref, acc_ref, m_ref, l_ref, alpha_ref,
               s_ref, p_ref):
    b = pl.program_id(0)
    i = pl.program_id(1)

    @pl.when((b == 0) & (i == 0))
    def _():
        _bias_init(rb_ref, bias_ref)

    acc_ref[...] = jnp.zeros_like(acc_ref)
    m_ref[...] = jnp.full_like(m_ref, MASKED)
    l_ref[...] = jnp.zeros_like(l_ref)

    eye = jnp.where(lax.broadcasted_iota(jnp.int32, (HEAD_DIM, HEAD_DIM), 0)
                    == lax.broadcasted_iota(jnp.int32, (HEAD_DIM, HEAD_DIM), 1), 1.0, 0.0).astype(BF16)

    def tile(j, near):
        halves = [pl.ds(pl.multiple_of(j * TILE + r * HEAD_DIM, HEAD_DIM), HEAD_DIM)
                  for r in range(TILE // HEAD_DIM)]
        masks = [mask_ref[0, 0, rows, :] for rows in halves]

        for h in range(N_HEADS):
            feat = slice(h * HEAD_DIM, (h + 1) * HEAD_DIM)
            s_parts = []
            for r, rows in enumerate(halves):
                lhs = jnp.concatenate([k_ref[rows, feat], eye], axis=1)
                rhs = jnp.concatenate([q_ref[0, feat, :], masks[r]], axis=0)
                s = jnp.dot(lhs, rhs, preferred_element_type=F32)
                if near is not None:
                    s = s + bias_ref[h, near * TILE + r * HEAD_DIM:near * TILE + (r + 1) * HEAD_DIM, :]
                s_parts.append(s)
            for c in range(TILE // LANES):
                lanes = slice(c * LANES, (c + 1) * LANES)
                sc = jnp.concatenate([s[:, lanes] for s in s_parts], axis=0)
                m_old = m_ref[h:h + 1, lanes]
                m_new = jnp.maximum(m_old, jnp.max(sc, axis=0, keepdims=True))
                p_ref[h, :, lanes] = jnp.exp2(sc - m_new).astype(BF16)
                alpha_ref[h:h + 1, lanes] = jnp.exp2(m_old - m_new)
                m_ref[h:h + 1, lanes] = m_new
        for h in range(N_HEADS):
            feat = slice(h * HEAD_DIM, (h + 1) * HEAD_DIM)
            pv = jnp.dot(vt_ref[j, h * VT_ROWS:(h + 1) * VT_ROWS, :], p_ref[h], preferred_element_type=F32)
            alpha = alpha_ref[h:h + 1, :]
            acc_ref[feat, :] = alpha * acc_ref[feat, :] + pv[:HEAD_DIM, :]
            l_ref[h:h + 1, :] = alpha * l_ref[h:h + 1, :] + pv[HEAD_DIM:HEAD_DIM + 1, :]

    def far_tile(j, carry):
        tile(j, None)
        return carry

    lax.fori_loop(0, i - 1, far_tile, 0)

    @pl.when(i >= 1)
    def _():
        tile(i - 1, 0)

    tile(i, 1)

    for h in range(N_HEADS):
        feat = slice(h * HEAD_DIM, (h + 1) * HEAD_DIM)
        o_ref[:, feat] = (acc_ref[feat, :] / l_ref[h:h + 1, :]).T.astype(o_ref.dtype)


def _attention(qt, k, vt, mask, rel_bias, batch, seq):
    nq = seq // TILE
    vmem = (2 * ATTN_DIM * TILE * 2 * 2 + seq * (ATTN_DIM + N_HEADS * VT_ROWS) * 2 + 2 * seq * TILE * 2
            + N_HEADS * 2 * TILE * TILE * 4 + ATTN_DIM * TILE * 4 + N_HEADS * TILE * TILE * (4 + 2)
            + 4 * TILE * TILE * 4)
    resident = pl.Buffered(1)
    return pl.pallas_call(
        _attn_body,
        grid=(batch, nq),
        in_specs=[pl.BlockSpec(memory_space=pltpu.SMEM),
                  pl.BlockSpec((1, ATTN_DIM, TILE), lambda b, i: (b * nq + i, 0, 0)),
                  pl.BlockSpec((seq, ATTN_DIM), lambda b, i: (b, 0), pipeline_mode=resident),
                  pl.BlockSpec((nq, N_HEADS * VT_ROWS, TILE), lambda b, i: (b, 0, 0), pipeline_mode=resident),
                  pl.BlockSpec((1, 1, seq, TILE), lambda b, i: (b, i, 0, 0))],
        out_specs=pl.BlockSpec((TILE, ATTN_DIM), lambda b, i: (b * nq + i, 0)),
        out_shape=jax.ShapeDtypeStruct((batch * seq, ATTN_DIM), BF16),
        scratch_shapes=[pltpu.VMEM((N_HEADS, 2 * TILE, TILE), F32),
                        pltpu.VMEM((ATTN_DIM, TILE), F32),
                        pltpu.VMEM((N_HEADS, TILE), F32),
                        pltpu.VMEM((N_HEADS, TILE), F32),
                        pltpu.VMEM((N_HEADS, TILE), F32),
                        pltpu.VMEM((N_HEADS, TILE, TILE), F32),
                        pltpu.VMEM((N_HEADS, TILE, TILE), BF16)],
        compiler_params=pltpu.CompilerParams(
            dimension_semantics=("arbitrary", "arbitrary"), vmem_limit_bytes=_vmem_limit(vmem)),
        name="attention",
    )(rel_bias, qt, k, vt, mask)


def _conv_body(cur_ref, halo_ref, dw_ref, dwb_ref, lng_ref, lnb_ref, o_ref, h_ref, y_ref):
    i = pl.program_id(1)

    def glu(x):
        return x[:, :CONV_CH].astype(F32) * jax.nn.sigmoid(x[:, CONV_CH:].astype(F32))

    h_ref[:CONV_HALO, :] = jnp.where(i > 0, glu(halo_ref[...]), 0.0)
    h_ref[CONV_HALO:, :] = glu(cur_ref[...])

    first = CONV_HALO - (CONV_WIDTH - 1)
    for r in range(CONV_TS // CONV_RB):
        for c in range(CONV_CH // CONV_CB):
            cols = slice(c * CONV_CB, (c + 1) * CONV_CB)
            acc = jnp.broadcast_to(dwb_ref[:, cols], (CONV_RB, CONV_CB))
            for j in range(CONV_WIDTH):
                acc = acc + h_ref[r * CONV_RB + first + j:r * CONV_RB + first + j + CONV_RB, cols] * dw_ref[j:j + 1, cols]
            y_ref[r * CONV_RB:(r + 1) * CONV_RB, cols] = acc

    y = y_ref[...]
    mu = jnp.mean(y, axis=-1, keepdims=True)
    var = jnp.mean(jnp.square(y - mu), axis=-1, keepdims=True)
    z = (y - mu) * lax.rsqrt(var + EPS) * lng_ref[...] + lnb_ref[...]
    o_ref[...] = (z * jax.nn.sigmoid(z)).astype(o_ref.dtype)


def _conv_module(glu_in, dw, dw_b, ln_g, ln_b, batch, seq):
    ns = seq // CONV_TS
    halo_per_tile = CONV_TS // CONV_HALO
    vec = pl.BlockSpec((1, CONV_CH), lambda b, i: (0, 0))
    vmem = (2 * (CONV_TS + CONV_HALO) * 2 * CONV_CH * 2 + 2 * CONV_TS * CONV_CH * 2
            + (2 * CONV_TS + CONV_HALO) * CONV_CH * 4 + 4 * CONV_TS * CONV_CH * 4)
    return pl.pallas_call(
        _conv_body,
        grid=(batch, ns),
        in_specs=[pl.BlockSpec((CONV_TS, 2 * CONV_CH), lambda b, i: (b * ns + i, 0)),
                  pl.BlockSpec((CONV_HALO, 2 * CONV_CH),
                               lambda b, i: (jnp.maximum((b * ns + i) * halo_per_tile - 1, 0), 0)),
                  pl.BlockSpec((CONV_WIDTH, CONV_CH), lambda b, i: (0, 0)),
                  vec, vec, vec],
        out_specs=pl.BlockSpec((CONV_TS, CONV_CH), lambda b, i: (b * ns + i, 0)),
        out_shape=jax.ShapeDtypeStruct((batch * seq, CONV_CH), BF16),
        scratch_shapes=[pltpu.VMEM((CONV_TS + CONV_HALO, CONV_CH), F32),
                        pltpu.VMEM((CONV_TS, CONV_CH), F32)],
        compiler_params=pltpu.CompilerParams(
            dimension_semantics=("arbitrary", "arbitrary"), vmem_limit_bytes=_vmem_limit(vmem)),
        name="conv_module",
    )(glu_in, glu_in, dw, dw_b, ln_g, ln_b)


def _mix_body(attn_ref, conv_ref, gates_ref, h_ref, wo_ref, wpw_ref, wout_ref, g_ref, o_ref):
    a = jnp.dot(attn_ref[...], wo_ref[...], preferred_element_type=F32)
    c = jnp.dot(conv_ref[...], wpw_ref[...], preferred_element_type=F32)
    mixed = (gates_ref[:, :D_MODEL].astype(F32) * a + gates_ref[:, D_MODEL:].astype(F32) * c).astype(BF16)
    y = jnp.dot(mixed, wout_ref[...], preferred_element_type=F32)
    o_ref[...] = h_ref[...] + _rms(y, g_ref[...])


def _mix_out(attn, conv, gates, h, w_o, w_pw2, w_out, post_g):
    t = h.shape[0]
    resident = pl.Buffered(1)

    def rows(n):
        return pl.BlockSpec((OUT_TM, n), lambda i: (i, 0))

    def whole(a):
        return pl.BlockSpec(a.shape, lambda i: (0, 0), pipeline_mode=resident)

    vmem = ((w_o.size + w_pw2.size + w_out.size) * 2
            + 2 * OUT_TM * (2 * ATTN_DIM * 2 + 2 * D_MODEL * 2 + 2 * D_MODEL * 4) + 5 * OUT_TM * D_MODEL * 4)
    return pl.pallas_call(
        _mix_body,
        grid=(t // OUT_TM,),
        in_specs=[rows(ATTN_DIM), rows(CONV_CH), rows(2 * D_MODEL), rows(D_MODEL),
                  whole(w_o), whole(w_pw2), whole(w_out), pl.BlockSpec((1, D_MODEL), lambda i: (0, 0))],
        out_specs=rows(D_MODEL),
        out_shape=jax.ShapeDtypeStruct((t, D_MODEL), F32),
        compiler_params=pltpu.CompilerParams(
            dimension_semantics=("arbitrary",), vmem_limit_bytes=_vmem_limit(vmem)),
        name="mix_out",
    )(attn, conv, gates, h, w_o, w_pw2, w_out, post_g)


def kernel(x, rel_bias, ffn1_pre_g, ffn1_wg, ffn1_wu, ffn1_wd, ffn1_post_g, mix_pre_g, w_in, b_gate, w_o,
           conv_dw, conv_dw_b, conv_ln_g, conv_ln_b, w_pw2, w_out, mix_post_g, ffn2_pre_g, ffn2_wg, ffn2_wu,
           ffn2_wd, ffn2_post_g):
    batch, seq, d = x.shape
    depth = ffn1_wg.shape[0]
    h = x.reshape(batch * seq, d)
    o_k = ATTN_DIM
    o_v = o_k + ATTN_DIM
    o_qi = o_v + ATTN_DIM
    o_ki = o_qi + IDX_HEADS * IDX_DIM
    o_wi = o_ki + IDX_DIM
    o_glu = o_wi + IDX_HEADS
    o_gates = o_glu + 2 * CONV_CH
    for l in range(depth):
        wl = w_in[l]
        w_t = jnp.concatenate([wl[:, :o_k] * (HEAD_DIM ** -0.5 * LOG2E), wl[:, o_qi:o_ki]],
                              axis=1).T.astype(BF16)
        w_vt = wl[:, o_v:o_qi].T.astype(BF16)
        w_wt = wl[:, o_wi:o_glu].T.astype(BF16)
        w_k = wl[:, o_k:o_v].astype(BF16)
        w_ki = jnp.pad(wl[:, o_ki:o_wi], ((0, 0), (0, LANES - IDX_DIM))).astype(BF16)
        w_glu = wl[:, o_glu:o_gates].astype(BF16)
        w_gates = wl[:, o_gates:].astype(BF16)

        h, u = _ffn(h, ffn1_pre_g[l][None], ffn1_wg[l].astype(BF16), ffn1_wu[l].astype(BF16),
                    ffn1_wd[l].astype(BF16), ffn1_post_g[l][None], mix_pre_g[l][None])

        qt, vt, wt = _proj_t(u, w_t, w_vt, w_wt)
        k = _proj(u, w_k, BF16, "proj_k")
        kw = _proj(u, w_ki, F32, "proj_ki")
        glu_in = _proj(u, w_glu, BF16, "proj_glu")
        gates = _proj(u, w_gates, BF16, "proj_gates", bias=b_gate[l][None])

        mask = _indexer(qt, wt, kw, batch, seq)
        attn = _attention(qt, k, vt, mask, rel_bias, batch, seq)
        conv = _conv_module(glu_in, conv_dw[l], conv_dw_b[l][None], conv_ln_g[l][None], conv_ln_b[l][None],
                            batch, seq)
        h = _mix_out(attn, conv, gates, h, w_o[l].astype(BF16), w_pw2[l].astype(BF16),
                     w_out[l].astype(BF16), mix_post_g[l][None])

        h = _ffn(h, ffn2_pre_g[l][None], ffn2_wg[l].astype(BF16), ffn2_wu[l].astype(BF16),
                 ffn2_wd[l].astype(BF16), ffn2_post_g[l][None])
    return h.reshape(batch, seq, d)
```

```python
import functools
import math

import jax
import jax.numpy as jnp
from jax import lax
from jax.experimental import pallas as pl
from jax.experimental.pallas import tpu as pltpu

D_MODEL = 2048
N_HEADS = 8
HEAD_DIM = 128
ATTN_DIM = N_HEADS * HEAD_DIM
IDX_HEADS = 16
IDX_DIM = 64
TOPK_MAX = 256
NUM_BUCKETS = 32
MAX_DISTANCE = 128
CONV_CH = 1024
CONV_WIDTH = 31
D_FF = 5632
FFN_RES_WEIGHT = 0.5
EPS = 1e-6

F32 = jnp.float32
BF16 = jnp.bfloat16
LANES = 128
SUBLANES = 8
V7X_VMEM_BYTES = 64 * 1024 * 1024
MASKED = -1e30
INT_MIN = -(2 ** 31)
I16 = jnp.int16
I16_MIN = -(2 ** 15)
PACKED_ROWS = 16
LOG2E = math.log2(math.e)

FFN_TM, FFN_TF = 512, 512
PROJ_TM = 1024
TILE = 256
PROJT_ROWS = 512
COUNT_TILES = 4
VT_ROWS = HEAD_DIM + PACKED_ROWS
CONV_TS = 512
CONV_HALO = 32
CONV_RB, CONV_CB = 128, 256
OUT_TM = 256

_NT = (((1,), (1,)), ((), ()))


def _vmem_limit(nbytes):
    return int(min(nbytes + (8 << 20), V7X_VMEM_BYTES - (4 << 20)))


def _rms(x, g):
    y = x * lax.rsqrt(jnp.mean(x * x, axis=-1, keepdims=True) + EPS)
    return y * g


def _ffn_body(emit_next, x_ref, pre_g_ref, wg_ref, wu_ref, wd_ref, post_g_ref, *rest):
    if emit_next:
        next_g_ref, out_ref, u_ref, xn_ref, acc_ref = rest
    else:
        out_ref, xn_ref, acc_ref = rest
    j = pl.program_id(1)

    @pl.when(j == 0)
    def _():
        xn_ref[...] = _rms(x_ref[...], pre_g_ref[...]).astype(BF16)
        acc_ref[...] = jnp.zeros_like(acc_ref)

    xn = xn_ref[...]
    g = jnp.dot(xn, wg_ref[...], preferred_element_type=F32)
    u = jnp.dot(xn, wu_ref[...], preferred_element_type=F32)
    a = (g * jax.nn.sigmoid(g) * u).astype(BF16)
    acc_ref[...] += jnp.dot(a, wd_ref[...], preferred_element_type=F32)

    @pl.when(j == pl.num_programs(1) - 1)
    def _():
        h = x_ref[...] + FFN_RES_WEIGHT * _rms(acc_ref[...], post_g_ref[...])
        out_ref[...] = h
        if emit_next:
            u_ref[...] = _rms(h, next_g_ref[...]).astype(BF16)


def _ffn(x, pre_g, wg, wu, wd, post_g, next_g=None):
    t, d = x.shape
    dff = wg.shape[1]
    emit_next = next_g is not None
    row = pl.BlockSpec((FFN_TM, d), lambda i, j: (i, 0))
    gain = pl.BlockSpec((1, d), lambda i, j: (0, 0))
    in_specs = [row, gain,
                pl.BlockSpec((d, FFN_TF), lambda i, j: (0, j)),
                pl.BlockSpec((d, FFN_TF), lambda i, j: (0, j)),
                pl.BlockSpec((FFN_TF, d), lambda i, j: (j, 0)),
                gain]
    args = [x, pre_g, wg, wu, wd, post_g]
    out_shape = [jax.ShapeDtypeStruct((t, d), F32)]
    out_specs = [row]
    if emit_next:
        in_specs.append(gain)
        args.append(next_g)
        out_shape.append(jax.ShapeDtypeStruct((t, d), BF16))
        out_specs.append(row)
    vmem = (2 * FFN_TM * d * 4 * 2 + 2 * FFN_TM * d * 2 + FFN_TM * d * (2 + 4)
            + 2 * 3 * d * FFN_TF * 2 + 3 * FFN_TM * FFN_TF * 4)
    res = pl.pallas_call(
        functools.partial(_ffn_body, emit_next),
        grid=(t // FFN_TM, dff // FFN_TF),
        in_specs=in_specs, out_specs=out_specs, out_shape=out_shape,
        scratch_shapes=[pltpu.VMEM((FFN_TM, d), BF16), pltpu.VMEM((FFN_TM, d), F32)],
        compiler_params=pltpu.CompilerParams(
            dimension_semantics=("arbitrary", "arbitrary"), vmem_limit_bytes=_vmem_limit(vmem)),
        name="ffn_next" if emit_next else "ffn",
    )(*args)
    return res if emit_next else res[0]


def _proj_body(gate, x_ref, w_ref, *rest):
    if gate:
        b_ref, o_ref = rest
    else:
        (o_ref,) = rest
    y = jnp.dot(x_ref[...], w_ref[...], preferred_element_type=F32)
    if gate:
        y = jax.nn.sigmoid(y + b_ref[...])
    o_ref[...] = y.astype(o_ref.dtype)


def _proj(x, w, out_dtype, name, bias=None):
    t, k = x.shape
    n = w.shape[1]
    tn = min(n, 512)
    gate = bias is not None
    in_specs = [pl.BlockSpec((PROJ_TM, k), lambda i, j: (i, 0)),
                pl.BlockSpec((k, tn), lambda i, j: (0, j))]
    args = [x, w]
    if gate:
        in_specs.append(pl.BlockSpec((1, tn), lambda i, j: (0, j)))
        args.append(bias)
    vmem = 2 * PROJ_TM * k * 2 + 2 * k * tn * 2 + 3 * PROJ_TM * tn * 4
    return pl.pallas_call(
        functools.partial(_proj_body, gate),
        grid=(t // PROJ_TM, n // tn),
        in_specs=in_specs,
        out_specs=pl.BlockSpec((PROJ_TM, tn), lambda i, j: (i, j)),
        out_shape=jax.ShapeDtypeStruct((t, n), out_dtype),
        compiler_params=pltpu.CompilerParams(
            dimension_semantics=("arbitrary", "arbitrary"), vmem_limit_bytes=_vmem_limit(vmem)),
        name=name,
    )(*args)


def _proj_t_body(u_ref, wt_ref, wvt_ref, wwt_ref, o_ref, v_ref, w_ref):
    u = u_ref[...]
    for r in range(wt_ref.shape[0] // PROJT_ROWS):
        rows = slice(r * PROJT_ROWS, (r + 1) * PROJT_ROWS)
        o_ref[0, rows, :] = lax.dot_general(wt_ref[rows, :], u, _NT, preferred_element_type=F32).astype(BF16)
    heads_per_dot = PROJT_ROWS // HEAD_DIM
    for r in range(wvt_ref.shape[0] // PROJT_ROWS):
        vt = lax.dot_general(wvt_ref[r * PROJT_ROWS:(r + 1) * PROJT_ROWS, :], u, _NT,
                             preferred_element_type=F32).astype(BF16)
        for hh in range(heads_per_dot):
            h = r * heads_per_dot + hh
            v_ref[0, h * VT_ROWS:h * VT_ROWS + HEAD_DIM, :] = vt[hh * HEAD_DIM:(hh + 1) * HEAD_DIM, :]
            v_ref[0, h * VT_ROWS + HEAD_DIM:(h + 1) * VT_ROWS, :] = jnp.ones((VT_ROWS - HEAD_DIM, TILE), BF16)
    w_ref[0] = lax.dot_general(wwt_ref[...], u, _NT, preferred_element_type=F32)


def _proj_t(u, wt, wvt, wwt):
    t, k = u.shape
    n, nw = wt.shape[0], wwt.shape[0]
    nv = N_HEADS * VT_ROWS
    resident = pl.Buffered(1)
    vmem = (2 * TILE * k * 2 + (n + wvt.shape[0] + nw) * k * 2 + 2 * ((n + nv) * 2 + nw * 4) * TILE
            + 2 * PROJT_ROWS * TILE * 4)
    return pl.pallas_call(
        _proj_t_body,
        grid=(t // TILE,),
        in_specs=[pl.BlockSpec((TILE, k), lambda i: (i, 0)),
                  pl.BlockSpec((n, k), lambda i: (0, 0), pipeline_mode=resident),
                  pl.BlockSpec(wvt.shape, lambda i: (0, 0), pipeline_mode=resident),
                  pl.BlockSpec((nw, k), lambda i: (0, 0), pipeline_mode=resident)],
        out_specs=[pl.BlockSpec((1, n, TILE), lambda i: (i, 0, 0)),
                   pl.BlockSpec((1, nv, TILE), lambda i: (i, 0, 0)),
                   pl.BlockSpec((1, nw, TILE), lambda i: (i, 0, 0))],
        out_shape=[jax.ShapeDtypeStruct((t // TILE, n, TILE), BF16),
                   jax.ShapeDtypeStruct((t // TILE, nv, TILE), BF16),
                   jax.ShapeDtypeStruct((t // TILE, nw, TILE), F32)],
        compiler_params=pltpu.CompilerParams(
            dimension_semantics=("arbitrary",), vmem_limit_bytes=_vmem_limit(vmem)),
        name="proj_t",
    )(u, wt, wvt, wwt)


def _indexer_body(seq, qi_ref, w_ref, kw_ref, mask_ref, kb_ref, key_ref, hi_ref, lo_ref):
    i = pl.program_id(1)
    nch = i + 1
    n_tiles = seq // TILE

    @pl.when(i == 0)
    def _():
        kb_ref[...] = kw_ref[:, :IDX_DIM].astype(BF16)

    w = w_ref[0] * (IDX_HEADS ** -0.5 * IDX_DIM ** -0.5)
    q_pos = i * TILE + lax.broadcasted_iota(jnp.int32, (1, TILE), 1)

    def rows_of(c):
        return pl.ds(pl.multiple_of(c * TILE, TILE), TILE)

    def score_chunk(c, carry):
        ks = kb_ref[rows_of(c), :]
        acc = jnp.zeros((TILE, TILE), F32)
        for h in range(IDX_HEADS):
            d = jnp.dot(ks, qi_ref[0, h * IDX_DIM:(h + 1) * IDX_DIM, :], preferred_element_type=F32)
            acc = acc + jnp.maximum(d, 0.0) * w[h:h + 1, :]
        bits = pltpu.bitcast(acc, jnp.int32)
        key = bits ^ ((bits >> 31) & jnp.int32(0x7FFFFFFF))
        k_pos = c * TILE + lax.broadcasted_iota(jnp.int32, (TILE, 1), 0)
        key = jnp.where(k_pos <= q_pos, key, jnp.int32(INT_MIN))
        key_ref[rows_of(c), :] = key
        hi_ref[rows_of(c), :] = (key >> 16).astype(I16)
        return carry

    lax.fori_loop(0, nch, score_chunk, 0)

    n_steps = (nch + COUNT_TILES - 1) // COUNT_TILES
    step_rows = COUNT_TILES * TILE
    groups = step_rows // PACKED_ROWS

    def rows_of_step(c):
        return pl.ds(pl.multiple_of(c * step_rows, step_rows), step_rows)

    def pad_chunk(c, carry):
        key_ref[rows_of(c), :] = jnp.full((TILE, TILE), INT_MIN, jnp.int32)
        hi_ref[rows_of(c), :] = jnp.full((TILE, TILE), I16_MIN, I16)
        return carry

    lax.fori_loop(nch, n_steps * COUNT_TILES, pad_chunk, 0)

    def search16(src_ref, need):
        def search_pass(b, u):
            cand_u = u | lax.shift_left(jnp.int32(1), 15 - b)
            cand = jnp.broadcast_to(cand_u + I16_MIN, (PACKED_ROWS, TILE)).astype(I16)

            def count_chunk(c, cnt):
                x = src_ref[rows_of_step(c), :].reshape(groups, PACKED_ROWS, TILE)
                ge = jnp.where(x >= cand[None], jnp.int16(1), jnp.int16(0))
                parts = [ge[g] for g in range(4)]
                for g in range(4, groups):
                    parts[g % 4] = parts[g % 4] + ge[g]
                return cnt + ((parts[0] + parts[1]) + (parts[2] + parts[3]))

            cnt = lax.fori_loop(0, n_steps, count_chunk, jnp.zeros((PACKED_ROWS, TILE), I16))
            total = jnp.sum(cnt.astype(jnp.int32), axis=0, keepdims=True)
            return jnp.where(total >= need, cand_u, u)

        return lax.fori_loop(0, 16, search_pass, jnp.zeros((1, TILE), jnp.int32))

    t_hi = search16(hi_ref, jnp.int32(TOPK_MAX)) + I16_MIN

    def low_half_chunk(c, n_gt):
        key = key_ref[rows_of_step(c), :]
        hi = key >> 16
        lo = (key & jnp.int32(0xFFFF)) + I16_MIN
        lo_ref[rows_of_step(c), :] = jnp.where(hi == t_hi, lo, I16_MIN).astype(I16)
        gt = jnp.where(hi > t_hi, 1, 0).reshape(step_rows // SUBLANES, SUBLANES, TILE)
        return n_gt + jnp.sum(gt, axis=0)

    n_gt = lax.fori_loop(0, n_steps, low_half_chunk, jnp.zeros((SUBLANES, TILE), jnp.int32))
    need_lo = TOPK_MAX - jnp.sum(n_gt, axis=0, keepdims=True)
    u_lo = search16(lo_ref, need_lo)
    thr = jnp.maximum(t_hi * 65536 + u_lo, jnp.int32(INT_MIN + 1))

    def count32(pred):
        def count_chunk(c, cnt):
            hit = jnp.where(pred(key_ref[rows_of_step(c), :]), 1, 0)
            return cnt + jnp.sum(hit.reshape(step_rows // SUBLANES, SUBLANES, TILE), axis=0)

        cnt = lax.fori_loop(0, n_steps, count_chunk, jnp.zeros((SUBLANES, TILE), jnp.int32))
        return jnp.sum(cnt, axis=0, keepdims=True)

    has_ties = jnp.max(count32(lambda key: key >= thr)) > TOPK_MAX

    @pl.when(jnp.logical_not(has_ties))
    def _():
        def write_chunk(c, carry):
            mask_ref[0, 0, rows_of(c), :] = jnp.where(key_ref[rows_of(c), :] >= thr, 0.0, MASKED).astype(BF16)
            return carry

        lax.fori_loop(0, nch, write_chunk, 0)

    @pl.when(has_ties)
    def _():
        keep = (TOPK_MAX - count32(lambda key: key > thr)).astype(F32)
        below = jnp.where(lax.broadcasted_iota(jnp.int32, (TILE, TILE), 0)
                          >= lax.broadcasted_iota(jnp.int32, (TILE, TILE), 1), 1.0, 0.0).astype(BF16)

        def write_chunk(c, seen):
            key = key_ref[rows_of(c), :]
            tied = key == thr
            rank = jnp.dot(below, jnp.where(tied, 1.0, 0.0).astype(BF16), preferred_element_type=F32) + seen
            m = jnp.where(key > thr, 0.0, jnp.where(tied, jnp.where(rank <= keep, 0.0, MASKED), MASKED))
            mask_ref[0, 0, rows_of(c), :] = m.astype(BF16)
            return rank[TILE - 1:TILE, :]

        lax.fori_loop(0, nch, write_chunk, jnp.zeros((1, TILE), F32))

    def fill_chunk(c, carry):
        mask_ref[0, 0, rows_of(c), :] = jnp.full((TILE, TILE), MASKED, BF16)
        return carry

    lax.fori_loop(nch, n_tiles, fill_chunk, 0)


def _indexer(qt, wt, kw, batch, seq):
    nq = seq // TILE
    qi_block = 1
    vmem = (2 * IDX_HEADS * IDX_DIM * TILE * 2 + 2 * seq * LANES * 4 + 2 * seq * TILE * 2
            + seq * LANES * 2 + seq * TILE * (4 + 2 + 2) + 8 * TILE * TILE * 4)
    return pl.pallas_call(
        functools.partial(_indexer_body, seq),
        grid=(batch, nq),
        in_specs=[pl.BlockSpec((1, IDX_HEADS * IDX_DIM, TILE), lambda b, i: (b * nq + i, qi_block, 0)),
                  pl.BlockSpec((1, IDX_HEADS, TILE), lambda b, i: (b * nq + i, 0, 0)),
                  pl.BlockSpec((seq, LANES), lambda b, i: (b, 0))],
        out_specs=pl.BlockSpec((1, 1, seq, TILE), lambda b, i: (b, i, 0, 0)),
        out_shape=jax.ShapeDtypeStruct((batch, nq, seq, TILE), BF16),
        scratch_shapes=[pltpu.VMEM((seq, IDX_DIM), BF16),
                        pltpu.VMEM((seq, TILE), jnp.int32),
                        pltpu.VMEM((seq, TILE), I16),
                        pltpu.VMEM((seq, TILE), I16)],
        compiler_params=pltpu.CompilerParams(
            dimension_semantics=("arbitrary", "arbitrary"), vmem_limit_bytes=_vmem_limit(vmem)),
        name="indexer",
    )(qt, wt, kw)


def _bias_init(rb_ref, bias_ref):
    shape = (2 * TILE, TILE)
    dist = (lax.broadcasted_iota(jnp.int32, shape, 1) - lax.broadcasted_iota(jnp.int32, shape, 0) + TILE)
    max_exact = NUM_BUCKETS // 2
    n = jnp.maximum(dist, 0)
    nf = jnp.maximum(n, max_exact).astype(F32)
    large = max_exact + (jnp.log(nf / max_exact) / math.log(MAX_DISTANCE / max_exact)
                         * (NUM_BUCKETS - max_exact)).astype(jnp.int32)
    large = jnp.minimum(large, NUM_BUCKETS - 1)
    bucket = jnp.where(n < max_exact, n, large)
    for h in range(N_HEADS):
        far = rb_ref[NUM_BUCKETS - 1, h]
        acc = jnp.zeros(shape, F32)
        for b in range(NUM_BUCKETS - 1):
            acc = jnp.where(bucket == b, (rb_ref[b, h] - far) * LOG2E, acc)
        bias_ref[h] = acc


def _attn_body(rb_ref, q_ref, k_ref, vt_ref, mask_ref, o_ref, bias_ref, acc_ref, m_ref, l_ref, alpha_ref,
               s_ref, p_ref):
    b = pl.program_id(0)
    i = pl.program_id(1)

    @pl.when((b == 0) & (i == 0))
    def _():
        _bias_init(rb_ref, bias_ref)

    acc_ref[...] = jnp.zeros_like(acc_ref)
    m_ref[...] = jnp.full_like(m_ref, MASKED)
    l_ref[...] = jnp.zeros_like(l_ref)

    eye = jnp.where(lax.broadcasted_iota(jnp.int32, (HEAD_DIM, HEAD_DIM), 0)
                    == lax.broadcasted_iota(jnp.int32, (HEAD_DIM, HEAD_DIM), 1), 1.0, 0.0).astype(BF16)

    def tile(j, near):
        halves = [pl.ds(pl.multiple_of(j * TILE + r * HEAD_DIM, HEAD_DIM), HEAD_DIM)
                  for r in range(TILE // HEAD_DIM)]
        masks = [mask_ref[0, 0, rows, :] for rows in halves]

        for h in range(N_HEADS):
            feat = slice(h * HEAD_DIM, (h + 1) * HEAD_DIM)
            s_parts = []
            for r, rows in enumerate(halves):
                lhs = jnp.concatenate([k_ref[rows, feat], eye], axis=1)
                rhs = jnp.concatenate([q_ref[0, feat, :], masks[r]], axis=0)
                s = jnp.dot(lhs, rhs, preferred_element_type=F32)
                if near is not None:
                    s = s + bias_ref[h, near * TILE + r * HEAD_DIM:near * TILE + (r + 1) * HEAD_DIM, :]
                s_parts.append(s)
            for c in range(TILE // LANES):
                lanes = slice(c * LANES, (c + 1) * LANES)
                sc = jnp.concatenate([s[:, lanes] for s in s_parts], axis=0)
                m_old = m_ref[h:h + 1, lanes]
                m_new = jnp.maximum(m_old, jnp.max(sc, axis=0, keepdims=True))
                p_ref[h, :, lanes] = jnp.exp2(sc - m_new).astype(BF16)
                alpha_ref[h:h + 1, lanes] = jnp.exp2(m_old - m_new)
                m_ref[h:h + 1, lanes] = m_new
        for h in range(N_HEADS):
            feat = slice(h * HEAD_DIM, (h + 1) * HEAD_DIM)
            pv = jnp.dot(vt_ref[j, h * VT_ROWS:(h + 1) * VT_ROWS, :], p_ref[h], preferred_element_type=F32)
            alpha = alpha_ref[h:h + 1, :]
            acc_ref[feat, :] = alpha * acc_ref[feat, :] + pv[:HEAD_DIM, :]
            l_ref[h:h + 1, :] = alpha * l_ref[h:h + 1, :] + pv[HEAD_DIM:HEAD_DIM + 1, :]

    def far_tile(j, carry):
        tile(j, None)
        return carry

    lax.fori_loop(0, i - 1, far_tile, 0)

    @pl.when(i >= 1)
    def _():
        tile(i - 1, 0)

    tile(i, 1)

    for h in range(N_HEADS):
        feat = slice(h * HEAD_DIM, (h + 1) * HEAD_DIM)
        o_ref[:, feat] = (acc_ref[feat, :] / l_ref[h:h + 1, :]).T.astype(o_ref.dtype)


def _attention(qt, k, vt, mask, rel_bias, batch, seq):
    nq = seq // TILE
    vmem = (2 * ATTN_DIM * TILE * 2 * 2 + seq * (ATTN_DIM + N_HEADS * VT_ROWS) * 2 + 2 * seq * TILE * 2
            + N_HEADS * 2 * TILE * TILE * 4 + ATTN_DIM * TILE * 4 + N_HEADS * TILE * TILE * (4 + 2)
            + 4 * TILE * TILE * 4)
    resident = pl.Buffered(1)
    return pl.pallas_call(
        _attn_body,
        grid=(batch, nq),
        in_specs=[pl.BlockSpec(memory_space=pltpu.SMEM),
                  pl.BlockSpec((1, ATTN_DIM, TILE), lambda b, i: (b * nq + i, 0, 0)),
                  pl.BlockSpec((seq, ATTN_DIM), lambda b, i: (b, 0), pipeline_mode=resident),
                  pl.BlockSpec((nq, N_HEADS * VT_ROWS, TILE), lambda b, i: (b, 0, 0), pipeline_mode=resident),
                  pl.BlockSpec((1, 1, seq, TILE), lambda b, i: (b, i, 0, 0))],
        out_specs=pl.BlockSpec((TILE, ATTN_DIM), lambda b, i: (b * nq + i, 0)),
        out_shape=jax.ShapeDtypeStruct((batch * seq, ATTN_DIM), BF16),
        scratch_shapes=[pltpu.VMEM((N_HEADS, 2 * TILE, TILE), F32),
                        pltpu.VMEM((ATTN_DIM, TILE), F32),
                        pltpu.VMEM((N_HEADS, TILE), F32),
                        pltpu.VMEM((N_HEADS, TILE), F32),
                        pltpu.VMEM((N_HEADS, TILE), F32),
                        pltpu.VMEM((N_HEADS, TILE, TILE), F32),
                        pltpu.VMEM((N_HEADS, TILE, TILE), BF16)],
        compiler_params=pltpu.CompilerParams(
            dimension_semantics=("arbitrary", "arbitrary"), vmem_limit_bytes=_vmem_limit(vmem)),
        name="attention",
    )(rel_bias, qt, k, vt, mask)


def _conv_body(cur_ref, halo_ref, dw_ref, dwb_ref, lng_ref, lnb_ref, o_ref, h_ref, y_ref):
    i = pl.program_id(1)

    def glu(x):
        return x[:, :CONV_CH].astype(F32) * jax.nn.sigmoid(x[:, CONV_CH:].astype(F32))

    h_ref[:CONV_HALO, :] = jnp.where(i > 0, glu(halo_ref[...]), 0.0)
    h_ref[CONV_HALO:, :] = glu(cur_ref[...])

    first = CONV_HALO - (CONV_WIDTH - 1)
    for r in range(CONV_TS // CONV_RB):
        for c in range(CONV_CH // CONV_CB):
            cols = slice(c * CONV_CB, (c + 1) * CONV_CB)
            acc = jnp.broadcast_to(dwb_ref[:, cols], (CONV_RB, CONV_CB))
            for j in range(CONV_WIDTH):
                acc = acc + h_ref[r * CONV_RB + first + j:r * CONV_RB + first + j + CONV_RB, cols] * dw_ref[j:j + 1, cols]
            y_ref[r * CONV_RB:(r + 1) * CONV_RB, cols] = acc

    y = y_ref[...]
    mu = jnp.mean(y, axis=-1, keepdims=True)
    var = jnp.mean(jnp.square(y - mu), axis=-1, keepdims=True)
    z = (y - mu) * lax.rsqrt(var + EPS) * lng_ref[...] + lnb_ref[...]
    o_ref[...] = (z * jax.nn.sigmoid(z)).astype(o_ref.dtype)


def _conv_module(glu_in, dw, dw_b, ln_g, ln_b, batch, seq):
    ns = seq // CONV_TS
    halo_per_tile = CONV_TS // CONV_HALO
    vec = pl.BlockSpec((1, CONV_CH), lambda b, i: (0, 0))
    vmem = (2 * (CONV_TS + CONV_HALO) * 2 * CONV_CH * 2 + 2 * CONV_TS * CONV_CH * 2
            + (2 * CONV_TS + CONV_HALO) * CONV_CH * 4 + 4 * CONV_TS * CONV_CH * 4)
    return pl.pallas_call(
        _conv_body,
        grid=(batch, ns),
        in_specs=[pl.BlockSpec((CONV_TS, 2 * CONV_CH), lambda b, i: (b * ns + i, 0)),
                  pl.BlockSpec((CONV_HALO, 2 * CONV_CH),
                               lambda b, i: (jnp.maximum((b * ns + i) * halo_per_tile - 1, 0), 0)),
                  pl.BlockSpec((CONV_WIDTH, CONV_CH), lambda b, i: (0, 0)),
                  vec, vec, vec],
        out_specs=pl.BlockSpec((CONV_TS, CONV_CH), lambda b, i: (b * ns + i, 0)),
        out_shape=jax.ShapeDtypeStruct((batch * seq, CONV_CH), BF16),
        scratch_shapes=[pltpu.VMEM((CONV_TS + CONV_HALO, CONV_CH), F32),
                        pltpu.VMEM((CONV_TS, CONV_CH), F32)],
        compiler_params=pltpu.CompilerParams(
            dimension_semantics=("arbitrary", "arbitrary"), vmem_limit_bytes=_vmem_limit(vmem)),
        name="conv_module",
    )(glu_in, glu_in, dw, dw_b, ln_g, ln_b)


def _mix_body(attn_ref, conv_ref, gates_ref, h_ref, wo_ref, wpw_ref, wout_ref, g_ref, o_ref):
    a = jnp.dot(attn_ref[...], wo_ref[...], preferred_element_type=F32)
    c = jnp.dot(conv_ref[...], wpw_ref[...], preferred_element_type=F32)
    mixed = (gates_ref[:, :D_MODEL].astype(F32) * a + gates_ref[:, D_MODEL:].astype(F32) * c).astype(BF16)
    y = jnp.dot(mixed, wout_ref[...], preferred_element_type=F32)
    o_ref[...] = h_ref[...] + _rms(y, g_ref[...])


def _mix_out(attn, conv, gates, h, w_o, w_pw2, w_out, post_g):
    t = h.shape[0]
    resident = pl.Buffered(1)

    def rows(n):
        return pl.BlockSpec((OUT_TM, n), lambda i: (i, 0))

    def whole(a):
        return pl.BlockSpec(a.shape, lambda i: (0, 0), pipeline_mode=resident)

    vmem = ((w_o.size + w_pw2.size + w_out.size) * 2
            + 2 * OUT_TM * (2 * ATTN_DIM * 2 + 2 * D_MODEL * 2 + 2 * D_MODEL * 4) + 5 * OUT_TM * D_MODEL * 4)
    return pl.pallas_call(
        _mix_body,
        grid=(t // OUT_TM,),
        in_specs=[rows(ATTN_DIM), rows(CONV_CH), rows(2 * D_MODEL), rows(D_MODEL),
                  whole(w_o), whole(w_pw2), whole(w_out), pl.BlockSpec((1, D_MODEL), lambda i: (0, 0))],
        out_specs=rows(D_MODEL),
        out_shape=jax.ShapeDtypeStruct((t, D_MODEL), F32),
        compiler_params=pltpu.CompilerParams(
            dimension_semantics=("arbitrary",), vmem_limit_bytes=_vmem_limit(vmem)),
        name="mix_out",
    )(attn, conv, gates, h, w_o, w_pw2, w_out, post_g)


def kernel(x, rel_bias, ffn1_pre_g, ffn1_wg, ffn1_wu, ffn1_wd, ffn1_post_g, mix_pre_g, w_in, b_gate, w_o,
           conv_dw, conv_dw_b, conv_ln_g, conv_ln_b, w_pw2, w_out, mix_post_g, ffn2_pre_g, ffn2_wg, ffn2_wu,
           ffn2_wd, ffn2_post_g):
    batch, seq, d = x.shape
    depth = ffn1_wg.shape[0]
    h = x.reshape(batch * seq, d)
    o_k = ATTN_DIM
    o_v = o_k + ATTN_DIM
    o_qi = o_v + ATTN_DIM
    o_ki = o_qi + IDX_HEADS * IDX_DIM
    o_wi = o_ki + IDX_DIM
    o_glu = o_wi + IDX_HEADS
    o_gates = o_glu + 2 * CONV_CH
    for l in range(depth):
        wl = w_in[l]
        w_t = jnp.concatenate([wl[:, :o_k] * (HEAD_DIM ** -0.5 * LOG2E), wl[:, o_qi:o_ki]],
                              axis=1).T.astype(BF16)
        w_vt = wl[:, o_v:o_qi].T.astype(BF16)
        w_wt = wl[:, o_wi:o_glu].T.astype(BF16)
        w_k = wl[:, o_k:o_v].astype(BF16)
        w_ki = jnp.pad(wl[:, o_ki:o_wi], ((0, 0), (0, LANES - IDX_DIM))).astype(BF16)
        w_glu = wl[:, o_glu:o_gates].astype(BF16)
        w_gates = wl[:, o_gates:].astype(BF16)

        h, u = _ffn(h, ffn1_pre_g[l][None], ffn1_wg[l].astype(BF16), ffn1_wu[l].astype(BF16),
                    ffn1_wd[l].astype(BF16), ffn1_post_g[l][None], mix_pre_g[l][None])

        qt, vt, wt = _proj_t(u, w_t, w_vt, w_wt)
        k = _proj(u, w_k, BF16, "proj_k")
        kw = _proj(u, w_ki, F32, "proj_ki")
        glu_in = _proj(u, w_glu, BF16, "proj_glu")
        gates = _proj(u, w_gates, BF16, "proj_gates", bias=b_gate[l][None])

        mask = _indexer(qt, wt, kw, batch, seq)
        attn = _attention(qt, k, vt, mask, rel_bias, batch, seq)
        conv = _conv_module(glu_in, conv_dw[l], conv_dw_b[l][None], conv_ln_g[l][None], conv_ln_b[l][None],
                            batch, seq)
        h = _mix_out(attn, conv, gates, h, w_o[l].astype(BF16), w_pw2[l].astype(BF16),
                     w_out[l].astype(BF16), mix_post_g[l][None])

        h = _ffn(h, ffn2_pre_g[l][None], ffn2_wg[l].astype(BF16), ffn2_wu[l].astype(BF16),
                 ffn2_wd[l].astype(BF16), ffn2_post_g[l][None])
    return h.reshape(batch, seq, d)
```

```python
import functools
import math

import jax
import jax.numpy as jnp
from jax import lax
from jax.experimental import pallas as pl
from jax.experimental.pallas import tpu as pltpu

D_MODEL = 2048
N_HEADS = 8
HEAD_DIM = 128
ATTN_DIM = N_HEADS * HEAD_DIM
IDX_HEADS = 16
IDX_DIM = 64
TOPK_MAX = 256
NUM_BUCKETS = 32
MAX_DISTANCE = 128
CONV_CH = 1024
CONV_WIDTH = 31
D_FF = 5632
FFN_RES_WEIGHT = 0.5
EPS = 1e-6

F32 = jnp.float32
BF16 = jnp.bfloat16
LANES = 128
SUBLANES = 8
V7X_VMEM_BYTES = 64 * 1024 * 1024
MASKED = -1e30
INT_MIN = -(2 ** 31)
I16 = jnp.int16
I16_MIN = -(2 ** 15)
PACKED_ROWS = 16
LOG2E = math.log2(math.e)

FFN_TM, FFN_TF = 512, 512
PROJ_TM = 1024
TILE = 256
PROJT_ROWS = 512
COUNT_TILES = 4
VT_ROWS = HEAD_DIM + PACKED_ROWS
CONV_TS = 512
CONV_HALO = 32
CONV_RB, CONV_CB = 128, 256
OUT_TM = 256

_NT = (((1,), (1,)), ((), ()))


def _vmem_limit(nbytes):
    return int(min(nbytes + (8 << 20), V7X_VMEM_BYTES - (4 << 20)))


def _rms(x, g):
    y = x * lax.rsqrt(jnp.mean(x * x, axis=-1, keepdims=True) + EPS)
    return y * g


def _ffn_body(emit_next, x_ref, pre_g_ref, wg_ref, wu_ref, wd_ref, post_g_ref, *rest):
    if emit_next:
        next_g_ref, out_ref, u_ref, xn_ref, acc_ref = rest
    else:
        out_ref, xn_ref, acc_ref = rest
    j = pl.program_id(1)

    @pl.when(j == 0)
    def _():
        xn_ref[...] = _rms(x_ref[...], pre_g_ref[...]).astype(BF16)
        acc_ref[...] = jnp.zeros_like(acc_ref)

    xn = xn_ref[...]
    g = jnp.dot(xn, wg_ref[...], preferred_element_type=F32)
    u = jnp.dot(xn, wu_ref[...], preferred_element_type=F32)
    a = (g * jax.nn.sigmoid(g) * u).astype(BF16)
    acc_ref[...] += jnp.dot(a, wd_ref[...], preferred_element_type=F32)

    @pl.when(j == pl.num_programs(1) - 1)
    def _():
        h = x_ref[...] + FFN_RES_WEIGHT * _rms(acc_ref[...], post_g_ref[...])
        out_ref[...] = h
        if emit_next:
            u_ref[...] = _rms(h, next_g_ref[...]).astype(BF16)


def _ffn(x, pre_g, wg, wu, wd, post_g, next_g=None):
    t, d = x.shape
    dff = wg.shape[1]
    emit_next = next_g is not None
    row = pl.BlockSpec((FFN_TM, d), lambda i, j: (i, 0))
    gain = pl.BlockSpec((1, d), lambda i, j: (0, 0))
    in_specs = [row, gain,
                pl.BlockSpec((d, FFN_TF), lambda i, j: (0, j)),
                pl.BlockSpec((d, FFN_TF), lambda i, j: (0, j)),
                pl.BlockSpec((FFN_TF, d), lambda i, j: (j, 0)),
                gain]
    args = [x, pre_g, wg, wu, wd, post_g]
    out_shape = [jax.ShapeDtypeStruct((t, d), F32)]
    out_specs = [row]
    if emit_next:
        in_specs.append(gain)
        args.append(next_g)
        out_shape.append(jax.ShapeDtypeStruct((t, d), BF16))
        out_specs.append(row)
    vmem = (2 * FFN_TM * d * 4 * 2 + 2 * FFN_TM * d * 2 + FFN_TM * d * (2 + 4)
            + 2 * 3 * d * FFN_TF * 2 + 3 * FFN_TM * FFN_TF * 4)
    res = pl.pallas_call(
        functools.partial(_ffn_body, emit_next),
        grid=(t // FFN_TM, dff // FFN_TF),
        in_specs=in_specs, out_specs=out_specs, out_shape=out_shape,
        scratch_shapes=[pltpu.VMEM((FFN_TM, d), BF16), pltpu.VMEM((FFN_TM, d), F32)],
        compiler_params=pltpu.CompilerParams(
            dimension_semantics=("arbitrary", "arbitrary"), vmem_limit_bytes=_vmem_limit(vmem)),
        name="ffn_next" if emit_next else "ffn",
    )(*args)
    return res if emit_next else res[0]


def _proj_body(gate, x_ref, w_ref, *rest):
    if gate:
        b_ref, o_ref = rest
    else:
        (o_ref,) = rest
    y = jnp.dot(x_ref[...], w_ref[...], preferred_element_type=F32)
    if gate:
        y = jax.nn.sigmoid(y + b_ref[...])
    o_ref[...] = y.astype(o_ref.dtype)


def _proj(x, w, out_dtype, name, bias=None):
    t, k = x.shape
    n = w.shape[1]
    tn = min(n, 512)
    gate = bias is not None
    in_specs = [pl.BlockSpec((PROJ_TM, k), lambda i, j: (i, 0)),
                pl.BlockSpec((k, tn), lambda i, j: (0, j))]
    args = [x, w]
    if gate:
        in_specs.append(pl.BlockSpec((1, tn), lambda i, j: (0, j)))
        args.append(bias)
    vmem = 2 * PROJ_TM * k * 2 + 2 * k * tn * 2 + 3 * PROJ_TM * tn * 4
    return pl.pallas_call(
        functools.partial(_proj_body, gate),
        grid=(t // PROJ_TM, n // tn),
        in_specs=in_specs,
        out_specs=pl.BlockSpec((PROJ_TM, tn), lambda i, j: (i, j)),
        out_shape=jax.ShapeDtypeStruct((t, n), out_dtype),
        compiler_params=pltpu.CompilerParams(
            dimension_semantics=("arbitrary", "arbitrary"), vmem_limit_bytes=_vmem_limit(vmem)),
        name=name,
    )(*args)


def _proj_t_body(u_ref, wt_ref, wvt_ref, wwt_ref, o_ref, v_ref, w_ref):
    u = u_ref[...]
    for r in range(wt_ref.shape[0] // PROJT_ROWS):
        rows = slice(r * PROJT_ROWS, (r + 1) * PROJT_ROWS)
        o_ref[0, rows, :] = lax.dot_general(wt_ref[rows, :], u, _NT, preferred_element_type=F32).astype(BF16)
    heads_per_dot = PROJT_ROWS // HEAD_DIM
    for r in range(wvt_ref.shape[0] // PROJT_ROWS):
        vt = lax.dot_general(wvt_ref[r * PROJT_ROWS:(r + 1) * PROJT_ROWS, :], u, _NT,
                             preferred_element_type=F32).astype(BF16)
        for hh in range(heads_per_dot):
            h = r * heads_per_dot + hh
            v_ref[0, h * VT_ROWS:h * VT_ROWS + HEAD_DIM, :] = vt[hh * HEAD_DIM:(hh + 1) * HEAD_DIM, :]
            v_ref[0, h * VT_ROWS + HEAD_DIM:(h + 1) * VT_ROWS, :] = jnp.ones((VT_ROWS - HEAD_DIM, TILE), BF16)
    w_ref[0] = lax.dot_general(wwt_ref[...], u, _NT, preferred_element_type=F32)


def _proj_t(u, wt, wvt, wwt):
    t, k = u.shape
    n, nw = wt.shape[0], wwt.shape[0]
    nv = N_HEADS * VT_ROWS
    resident = pl.Buffered(1)
    vmem = (2 * TILE * k * 2 + (n + wvt.shape[0] + nw) * k * 2 + 2 * ((n + nv) * 2 + nw * 4) * TILE
            + 2 * PROJT_ROWS * TILE * 4)
    return pl.pallas_call(
        _proj_t_body,
        grid=(t // TILE,),
        in_specs=[pl.BlockSpec((TILE, k), lambda i: (i, 0)),
                  pl.BlockSpec((n, k), lambda i: (0, 0), pipeline_mode=resident),
                  pl.BlockSpec(wvt.shape, lambda i: (0, 0), pipeline_mode=resident),
                  pl.BlockSpec((nw, k), lambda i: (0, 0), pipeline_mode=resident)],
        out_specs=[pl.BlockSpec((1, n, TILE), lambda i: (i, 0, 0)),
                   pl.BlockSpec((1, nv, TILE), lambda i: (i, 0, 0)),
                   pl.BlockSpec((1, nw, TILE), lambda i: (i, 0, 0))],
        out_shape=[jax.ShapeDtypeStruct((t // TILE, n, TILE), BF16),
                   jax.ShapeDtypeStruct((t // TILE, nv, TILE), BF16),
                   jax.ShapeDtypeStruct((t // TILE, nw, TILE), F32)],
        compiler_params=pltpu.CompilerParams(
            dimension_semantics=("arbitrary",), vmem_limit_bytes=_vmem_limit(vmem)),
        name="proj_t",
    )(u, wt, wvt, wwt)


def _indexer_body(seq, qi_ref, w_ref, kw_ref, mask_ref, kb_ref, key_ref, hi_ref, lo_ref):
    i = pl.program_id(1)
    nch = i + 1
    n_tiles = seq // TILE

    @pl.when(i == 0)
    def _():
        kb_ref[...] = kw_ref[:, :IDX_DIM].astype(BF16)

    w = w_ref[0] * (IDX_HEADS ** -0.5 * IDX_DIM ** -0.5)
    q_pos = i * TILE + lax.broadcasted_iota(jnp.int32, (1, TILE), 1)

    def rows_of(c):
        return pl.ds(pl.multiple_of(c * TILE, TILE), TILE)

    def score_chunk(c, carry):
        ks = kb_ref[rows_of(c), :]
        acc = jnp.zeros((TILE, TILE), F32)
        for h in range(IDX_HEADS):
            d = jnp.dot(ks, qi_ref[0, h * IDX_DIM:(h + 1) * IDX_DIM, :], preferred_element_type=F32)
            acc = acc + jnp.maximum(d, 0.0) * w[h:h + 1, :]
        bits = pltpu.bitcast(acc, jnp.int32)
        key = bits ^ ((bits >> 31) & jnp.int32(0x7FFFFFFF))
        k_pos = c * TILE + lax.broadcasted_iota(jnp.int32, (TILE, 1), 0)
        key = jnp.where(k_pos <= q_pos, key, jnp.int32(INT_MIN))
        key_ref[rows_of(c), :] = key
        hi_ref[rows_of(c), :] = (key >> 16).astype(I16)
        return carry

    lax.fori_loop(0, nch, score_chunk, 0)

    n_steps = (nch + COUNT_TILES - 1) // COUNT_TILES
    step_rows = COUNT_TILES * TILE
    groups = step_rows // PACKED_ROWS

    def rows_of_step(c):
        return pl.ds(pl.multiple_of(c * step_rows, step_rows), step_rows)

    def pad_chunk(c, carry):
        key_ref[rows_of(c), :] = jnp.full((TILE, TILE), INT_MIN, jnp.int32)
        hi_ref[rows_of(c), :] = jnp.full((TILE, TILE), I16_MIN, I16)
        return carry

    lax.fori_loop(nch, n_steps * COUNT_TILES, pad_chunk, 0)

    def search16(src_ref, need):
        def search_pass(b, u):
            cand_u = u | lax.shift_left(jnp.int32(1), 15 - b)
            cand = jnp.broadcast_to(cand_u + I16_MIN, (PACKED_ROWS, TILE)).astype(I16)

            def count_chunk(c, cnt):
                x = src_ref[rows_of_step(c), :].reshape(groups, PACKED_ROWS, TILE)
                ge = jnp.where(x >= cand[None], jnp.int16(1), jnp.int16(0))
                parts = [ge[g] for g in range(4)]
                for g in range(4, groups):
                    parts[g % 4] = parts[g % 4] + ge[g]
                return cnt + ((parts[0] + parts[1]) + (parts[2] + parts[3]))

            cnt = lax.fori_loop(0, n_steps, count_chunk, jnp.zeros((PACKED_ROWS, TILE), I16))
            total = jnp.sum(cnt.astype(jnp.int32), axis=0, keepdims=True)
            return jnp.where(total >= need, cand_u, u)

        return lax.fori_loop(0, 16, search_pass, jnp.zeros((1, TILE), jnp.int32))

    t_hi = search16(hi_ref, jnp.int32(TOPK_MAX)) + I16_MIN

    def low_half_chunk(c, n_gt):
        key = key_ref[rows_of_step(c), :]
        hi = key >> 16
        lo = (key & jnp.int32(0xFFFF)) + I16_MIN
        lo_ref[rows_of_step(c), :] = jnp.where(hi == t_hi, lo, I16_MIN).astype(I16)
        gt = jnp.where(hi > t_hi, 1, 0).reshape(step_rows // SUBLANES, SUBLANES, TILE)
        return n_gt + jnp.sum(gt, axis=0)

    n_gt = lax.fori_loop(0, n_steps, low_half_chunk, jnp.zeros((SUBLANES, TILE), jnp.int32))
    need_lo = TOPK_MAX - jnp.sum(n_gt, axis=0, keepdims=True)
    u_lo = search16(lo_ref, need_lo)
    thr = jnp.maximum(t_hi * 65536 + u_lo, jnp.int32(INT_MIN + 1))

    def count32(pred):
        def count_chunk(c, cnt):
            hit = jnp.where(pred(key_ref[rows_of_step(c), :]), 1, 0)
            return cnt + jnp.sum(hit.reshape(step_rows // SUBLANES, SUBLANES, TILE), axis=0)

        cnt = lax.fori_loop(0, n_steps, count_chunk, jnp.zeros((SUBLANES, TILE), jnp.int32))
        return jnp.sum(cnt, axis=0, keepdims=True)

    has_ties = jnp.max(count32(lambda key: key >= thr)) > TOPK_MAX

    @pl.when(jnp.logical_not(has_ties))
    def _():
        def write_chunk(c, carry):
            mask_ref[0, 0, rows_of(c), :] = jnp.where(key_ref[rows_of(c), :] >= thr, 0.0, MASKED).astype(BF16)
            return carry

        lax.fori_loop(0, nch, write_chunk, 0)

    @pl.when(has_ties)
    def _():
        keep = (TOPK_MAX - count32(lambda key: key > thr)).astype(F32)
        below = jnp.where(lax.broadcasted_iota(jnp.int32, (TILE, TILE), 0)
                          >= lax.broadcasted_iota(jnp.int32, (TILE, TILE), 1), 1.0, 0.0).astype(BF16)

        def write_chunk(c, seen):
            key = key_ref[rows_of(c), :]
            tied = key == thr
            rank = jnp.dot(below, jnp.where(tied, 1.0, 0.0).astype(BF16), preferred_element_type=F32) + seen
            m = jnp.where(key > thr, 0.0, jnp.where(tied, jnp.where(rank <= keep, 0.0, MASKED), MASKED))
            mask_ref[0, 0, rows_of(c), :] = m.astype(BF16)
            return rank[TILE - 1:TILE, :]

        lax.fori_loop(0, nch, write_chunk, jnp.zeros((1, TILE), F32))

    def fill_chunk(c, carry):
        mask_ref[0, 0, rows_of(c), :] = jnp.full((TILE, TILE), MASKED, BF16)
        return carry

    lax.fori_loop(nch, n_tiles, fill_chunk, 0)


def _indexer(qt, wt, kw, batch, seq):
    nq = seq // TILE
    qi_block = 1
    vmem = (2 * IDX_HEADS * IDX_DIM * TILE * 2 + 2 * seq * LANES * 4 + 2 * seq * TILE * 2
            + seq * LANES * 2 + seq * TILE * (4 + 2 + 2) + 8 * TILE * TILE * 4)
    return pl.pallas_call(
        functools.partial(_indexer_body, seq),
        grid=(batch, nq),
        in_specs=[pl.BlockSpec((1, IDX_HEADS * IDX_DIM, TILE), lambda b, i: (b * nq + i, qi_block, 0)),
                  pl.BlockSpec((1, IDX_HEADS, TILE), lambda b, i: (b * nq + i, 0, 0)),
                  pl.BlockSpec((seq, LANES), lambda b, i: (b, 0))],
        out_specs=pl.BlockSpec((1, 1, seq, TILE), lambda b, i: (b, i, 0, 0)),
        out_shape=jax.ShapeDtypeStruct((batch, nq, seq, TILE), BF16),
        scratch_shapes=[pltpu.VMEM((seq, IDX_DIM), BF16),
                        pltpu.VMEM((seq, TILE), jnp.int32),
                        pltpu.VMEM((seq, TILE), I16),
                        pltpu.VMEM((seq, TILE), I16)],
        compiler_params=pltpu.CompilerParams(
            dimension_semantics=("arbitrary", "arbitrary"), vmem_limit_bytes=_vmem_limit(vmem)),
        name="indexer",
    )(qt, wt, kw)


def _bias_init(rb_ref, bias_ref):
    shape = (2 * TILE, TILE)
    dist = (lax.broadcasted_iota(jnp.int32, shape, 1) - lax.broadcasted_iota(jnp.int32, shape, 0) + TILE)
    max_exact = NUM_BUCKETS // 2
    n = jnp.maximum(dist, 0)
    nf = jnp.maximum(n, max_exact).astype(F32)
    large = max_exact + (jnp.log(nf / max_exact) / math.log(MAX_DISTANCE / max_exact)
                         * (NUM_BUCKETS - max_exact)).astype(jnp.int32)
    large = jnp.minimum(large, NUM_BUCKETS - 1)
    bucket = jnp.where(n < max_exact, n, large)
    for h in range(N_HEADS):
        far = rb_ref[NUM_BUCKETS - 1, h]
        acc = jnp.zeros(shape, F32)
        for b in range(NUM_BUCKETS - 1):
            acc = jnp.where(bucket == b, (rb_ref[b, h] - far) * LOG2E, acc)
        bias_ref[h] = acc


def _attn_body(rb_ref, q_ref, k_ref, vt_ref, mask_ref, o_ref, bias_ref, acc_ref, m_ref, l_ref, alpha_ref,
               p_ref):
    b = pl.program_id(0)
    i = pl.program_id(1)

    @pl.when((b == 0) & (i == 0))
    def _():
        _bias_init(rb_ref, bias_ref)

    acc_ref[...] = jnp.zeros_like(acc_ref)
    m_ref[...] = jnp.full_like(m_ref, MASKED)
    l_ref[...] = jnp.zeros_like(l_ref)

    eye = jnp.where(lax.broadcasted_iota(jnp.int32, (HEAD_DIM, HEAD_DIM), 0)
                    == lax.broadcasted_iota(jnp.int32, (HEAD_DIM, HEAD_DIM), 1), 1.0, 0.0).astype(BF16)

    def numerators(j, near, slot):
        halves = [pl.ds(pl.multiple_of(j * TILE + r * HEAD_DIM, HEAD_DIM), HEAD_DIM)
                  for r in range(TILE // HEAD_DIM)]
        masks = [mask_ref[0, 0, rows, :] for rows in halves]
        for h in range(N_HEADS):
            feat = slice(h * HEAD_DIM, (h + 1) * HEAD_DIM)
            s_parts = []
            for r, rows in enumerate(halves):
                lhs = jnp.concatenate([k_ref[rows, feat], eye], axis=1)
                rhs = jnp.concatenate([q_ref[0, feat, :], masks[r]], axis=0)
                s = jnp.dot(lhs, rhs, preferred_element_type=F32)
                if near is not None:
                    s = s + bias_ref[h, near * TILE + r * HEAD_DIM:near * TILE + (r + 1) * HEAD_DIM, :]
                s_parts.append(s)
            for c in range(TILE // LANES):
                lanes = slice(c * LANES, (c + 1) * LANES)
                sc = jnp.concatenate([s[:, lanes] for s in s_parts], axis=0)
                m_old = m_ref[h:h + 1, lanes]
                m_new = jnp.maximum(m_old, jnp.max(sc, axis=0, keepdims=True))
                p_ref[slot, h, :, lanes] = jnp.exp2(sc - m_new).astype(BF16)
                alpha_ref[slot, h:h + 1, lanes] = jnp.exp2(m_old - m_new)
                m_ref[h:h + 1, lanes] = m_new

    def values(j, slot):
        for h in range(N_HEADS):
            feat = slice(h * HEAD_DIM, (h + 1) * HEAD_DIM)
            pv = jnp.dot(vt_ref[j, h * VT_ROWS:(h + 1) * VT_ROWS, :], p_ref[slot, h], preferred_element_type=F32)
            alpha = alpha_ref[slot, h:h + 1, :]
            acc_ref[feat, :] = alpha * acc_ref[feat, :] + pv[:HEAD_DIM, :]
            l_ref[h:h + 1, :] = alpha * l_ref[h:h + 1, :] + pv[HEAD_DIM:HEAD_DIM + 1, :]

    def pair(j0, near0, j1, near1):
        numerators(j0, near0, 0)
        numerators(j1, near1, 1)
        values(j0, 0)
        values(j1, 1)

    n_far = jnp.maximum(i - 1, 0)

    def far_pair(t, carry):
        pair(2 * t, None, 2 * t + 1, None)
        return carry

    lax.fori_loop(0, n_far // 2, far_pair, 0)

    @pl.when(n_far % 2 == 1)
    def _():
        numerators(n_far - 1, None, 0)
        values(n_far - 1, 0)

    @pl.when(i >= 1)
    def _():
        pair(i - 1, 0, i, 1)

    @pl.when(i == 0)
    def _():
        numerators(i, 1, 0)
        values(i, 0)

    for h in range(N_HEADS):
        feat = slice(h * HEAD_DIM, (h + 1) * HEAD_DIM)
        o_ref[:, feat] = (acc_ref[feat, :] / l_ref[h:h + 1, :]).T.astype(o_ref.dtype)


def _attention(qt, k, vt, mask, rel_bias, batch, seq):
    nq = seq // TILE
    vmem = (2 * ATTN_DIM * TILE * 2 * 2 + seq * (ATTN_DIM + N_HEADS * VT_ROWS) * 2 + 2 * seq * TILE * 2
            + N_HEADS * 2 * TILE * TILE * 4 + ATTN_DIM * TILE * 4 + N_HEADS * TILE * TILE * (4 + 2)
            + 4 * TILE * TILE * 4)
    resident = pl.Buffered(1)
    return pl.pallas_call(
        _attn_body,
        grid=(batch, nq),
        in_specs=[pl.BlockSpec(memory_space=pltpu.SMEM),
                  pl.BlockSpec((1, ATTN_DIM, TILE), lambda b, i: (b * nq + i, 0, 0)),
                  pl.BlockSpec((seq, ATTN_DIM), lambda b, i: (b, 0), pipeline_mode=resident),
                  pl.BlockSpec((nq, N_HEADS * VT_ROWS, TILE), lambda b, i: (b, 0, 0), pipeline_mode=resident),
                  pl.BlockSpec((1, 1, seq, TILE), lambda b, i: (b, i, 0, 0))],
        out_specs=pl.BlockSpec((TILE, ATTN_DIM), lambda b, i: (b * nq + i, 0)),
        out_shape=jax.ShapeDtypeStruct((batch * seq, ATTN_DIM), BF16),
        scratch_shapes=[pltpu.VMEM((N_HEADS, 2 * TILE, TILE), F32),
                        pltpu.VMEM((ATTN_DIM, TILE), F32),
                        pltpu.VMEM((N_HEADS, TILE), F32),
                        pltpu.VMEM((N_HEADS, TILE), F32),
                        pltpu.VMEM((2, N_HEADS, TILE), F32),
                        pltpu.VMEM((2, N_HEADS, TILE, TILE), BF16)],
        compiler_params=pltpu.CompilerParams(
            dimension_semantics=("arbitrary", "arbitrary"), vmem_limit_bytes=_vmem_limit(vmem)),
        name="attention",
    )(rel_bias, qt, k, vt, mask)


def _conv_body(cur_ref, halo_ref, dw_ref, dwb_ref, lng_ref, lnb_ref, o_ref, h_ref, y_ref):
    i = pl.program_id(1)

    def glu(x):
        return x[:, :CONV_CH].astype(F32) * jax.nn.sigmoid(x[:, CONV_CH:].astype(F32))

    h_ref[0, :CONV_HALO, :] = jnp.where(i > 0, glu(halo_ref[...]), 0.0)
    h_ref[0, CONV_HALO:, :] = glu(cur_ref[...])
    n_rows = CONV_TS + CONV_HALO
    for r in range(1, SUBLANES):
        h_ref[r, SUBLANES:, :] = h_ref[0, SUBLANES - r:n_rows - r, :]

    for rb in range(CONV_TS // CONV_RB):
        for c in range(CONV_CH // CONV_CB):
            cols = slice(c * CONV_CB, (c + 1) * CONV_CB)
            acc = jnp.broadcast_to(dwb_ref[:, cols], (CONV_RB, CONV_CB))
            for j in range(CONV_WIDTH):
                groups, r = divmod(CONV_WIDTH - 1 - j, SUBLANES)
                start = CONV_HALO - groups * SUBLANES + rb * CONV_RB
                acc = acc + h_ref[r, start:start + CONV_RB, cols] * dw_ref[j:j + 1, cols]
            y_ref[rb * CONV_RB:(rb + 1) * CONV_RB, cols] = acc

    y = y_ref[...]
    mu = jnp.mean(y, axis=-1, keepdims=True)
    var = jnp.mean(jnp.square(y - mu), axis=-1, keepdims=True)
    z = (y - mu) * lax.rsqrt(var + EPS) * lng_ref[...] + lnb_ref[...]
    o_ref[...] = (z * jax.nn.sigmoid(z)).astype(o_ref.dtype)


def _conv_module(glu_in, dw, dw_b, ln_g, ln_b, batch, seq):
    ns = seq // CONV_TS
    halo_per_tile = CONV_TS // CONV_HALO
    vec = pl.BlockSpec((1, CONV_CH), lambda b, i: (0, 0))
    vmem = (2 * (CONV_TS + CONV_HALO) * 2 * CONV_CH * 2 + 2 * CONV_TS * CONV_CH * 2
            + ((SUBLANES + 1) * CONV_TS + SUBLANES * CONV_HALO) * CONV_CH * 4 + 4 * CONV_TS * CONV_CH * 4)
    return pl.pallas_call(
        _conv_body,
        grid=(batch, ns),
        in_specs=[pl.BlockSpec((CONV_TS, 2 * CONV_CH), lambda b, i: (b * ns + i, 0)),
                  pl.BlockSpec((CONV_HALO, 2 * CONV_CH),
                               lambda b, i: (jnp.maximum((b * ns + i) * halo_per_tile - 1, 0), 0)),
                  pl.BlockSpec((CONV_WIDTH, CONV_CH), lambda b, i: (0, 0)),
                  vec, vec, vec],
        out_specs=pl.BlockSpec((CONV_TS, CONV_CH), lambda b, i: (b * ns + i, 0)),
        out_shape=jax.ShapeDtypeStruct((batch * seq, CONV_CH), BF16),
        scratch_shapes=[pltpu.VMEM((SUBLANES, CONV_TS + CONV_HALO, CONV_CH), F32),
                        pltpu.VMEM((CONV_TS, CONV_CH), F32)],
        compiler_params=pltpu.CompilerParams(
            dimension_semantics=("arbitrary", "arbitrary"), vmem_limit_bytes=_vmem_limit(vmem)),
        name="conv_module",
    )(glu_in, glu_in, dw, dw_b, ln_g, ln_b)


def _mix_body(attn_ref, conv_ref, gates_ref, h_ref, wo_ref, wpw_ref, wout_ref, g_ref, o_ref):
    a = jnp.dot(attn_ref[...], wo_ref[...], preferred_element_type=F32)
    c = jnp.dot(conv_ref[...], wpw_ref[...], preferred_element_type=F32)
    mixed = (gates_ref[:, :D_MODEL].astype(F32) * a + gates_ref[:, D_MODEL:].astype(F32) * c).astype(BF16)
    y = jnp.dot(mixed, wout_ref[...], preferred_element_type=F32)
    o_ref[...] = h_ref[...] + _rms(y, g_ref[...])


def _mix_out(attn, conv, gates, h, w_o, w_pw2, w_out, post_g):
    t = h.shape[0]
    resident = pl.Buffered(1)

    def rows(n):
        return pl.BlockSpec((OUT_TM, n), lambda i: (i, 0))

    def whole(a):
        return pl.BlockSpec(a.shape, lambda i: (0, 0), pipeline_mode=resident)

    vmem = ((w_o.size + w_pw2.size + w_out.size) * 2
            + 2 * OUT_TM * (2 * ATTN_DIM * 2 + 2 * D_MODEL * 2 + 2 * D_MODEL * 4) + 5 * OUT_TM * D_MODEL * 4)
    return pl.pallas_call(
        _mix_body,
        grid=(t // OUT_TM,),
        in_specs=[rows(ATTN_DIM), rows(CONV_CH), rows(2 * D_MODEL), rows(D_MODEL),
                  whole(w_o), whole(w_pw2), whole(w_out), pl.BlockSpec((1, D_MODEL), lambda i: (0, 0))],
        out_specs=rows(D_MODEL),
        out_shape=jax.ShapeDtypeStruct((t, D_MODEL), F32),
        compiler_params=pltpu.CompilerParams(
            dimension_semantics=("arbitrary",), vmem_limit_bytes=_vmem_limit(vmem)),
        name="mix_out",
    )(attn, conv, gates, h, w_o, w_pw2, w_out, post_g)


def kernel(x, rel_bias, ffn1_pre_g, ffn1_wg, ffn1_wu, ffn1_wd, ffn1_post_g, mix_pre_g, w_in, b_gate, w_o,
           conv_dw, conv_dw_b, conv_ln_g, conv_ln_b, w_pw2, w_out, mix_post_g, ffn2_pre_g, ffn2_wg, ffn2_wu,
           ffn2_wd, ffn2_post_g):
    batch, seq, d = x.shape
    depth = ffn1_wg.shape[0]
    h = x.reshape(batch * seq, d)
    o_k = ATTN_DIM
    o_v = o_k + ATTN_DIM
    o_qi = o_v + ATTN_DIM
    o_ki = o_qi + IDX_HEADS * IDX_DIM
    o_wi = o_ki + IDX_DIM
    o_glu = o_wi + IDX_HEADS
    o_gates = o_glu + 2 * CONV_CH
    for l in range(depth):
        wl = w_in[l]
        w_t = jnp.concatenate([wl[:, :o_k] * (HEAD_DIM ** -0.5 * LOG2E), wl[:, o_qi:o_ki]],
                              axis=1).T.astype(BF16)
        w_vt = wl[:, o_v:o_qi].T.astype(BF16)
        w_wt = wl[:, o_wi:o_glu].T.astype(BF16)
        w_k = wl[:, o_k:o_v].astype(BF16)
        w_ki = jnp.pad(wl[:, o_ki:o_wi], ((0, 0), (0, LANES - IDX_DIM))).astype(BF16)
        w_glu = wl[:, o_glu:o_gates].astype(BF16)
        w_gates = wl[:, o_gates:].astype(BF16)

        h, u = _ffn(h, ffn1_pre_g[l][None], ffn1_wg[l].astype(BF16), ffn1_wu[l].astype(BF16),
                    ffn1_wd[l].astype(BF16), ffn1_post_g[l][None], mix_pre_g[l][None])

        qt, vt, wt = _proj_t(u, w_t, w_vt, w_wt)
        k = _proj(u, w_k, BF16, "proj_k")
        kw = _proj(u, w_ki, F32, "proj_ki")
        glu_in = _proj(u, w_glu, BF16, "proj_glu")
        gates = _proj(u, w_gates, BF16, "proj_gates", bias=b_gate[l][None])

        mask = _indexer(qt, wt, kw, batch, seq)
        attn = _attention(qt, k, vt, mask, rel_bias, batch, seq)
        conv = _conv_module(glu_in, conv_dw[l], conv_dw_b[l][None], conv_ln_g[l][None], conv_ln_b[l][None],
                            batch, seq)
        h = _mix_out(attn, conv, gates, h, w_o[l].astype(BF16), w_pw2[l].astype(BF16),
                     w_out[l].astype(BF16), mix_post_g[l][None])

        h = _ffn(h, ffn2_pre_g[l][None], ffn2_wg[l].astype(BF16), ffn2_wu[l].astype(BF16),
                 ffn2_wd[l].astype(BF16), ffn2_post_g[l][None])
    return h.reshape(batch, seq, d)
```

```python
import functools
import math

import jax
import jax.numpy as jnp
from jax import lax
from jax.experimental import pallas as pl
from jax.experimental.pallas import tpu as pltpu

D_MODEL = 2048
N_HEADS = 8
HEAD_DIM = 128
ATTN_DIM = N_HEADS * HEAD_DIM
IDX_HEADS = 16
IDX_DIM = 64
TOPK_MAX = 256
NUM_BUCKETS = 32
MAX_DISTANCE = 128
CONV_CH = 1024
CONV_WIDTH = 31
D_FF = 5632
FFN_RES_WEIGHT = 0.5
EPS = 1e-6

F32 = jnp.float32
BF16 = jnp.bfloat16
LANES = 128
SUBLANES = 8
V7X_VMEM_BYTES = 64 * 1024 * 1024
MASKED = -1e30
INT_MIN = -(2 ** 31)
I16 = jnp.int16
I16_MIN = -(2 ** 15)
PACKED_ROWS = 16
LOG2E = math.log2(math.e)

FFN_TM, FFN_TF = 512, 512
PROJ_TM = 1024
TILE = 256
PROJT_ROWS = 512
COUNT_TILES = 4
VT_ROWS = HEAD_DIM + PACKED_ROWS
CONV_TS = 512
CONV_HALO = 32
CONV_RB, CONV_CB = 128, 256
OUT_TM = 256

_NT = (((1,), (1,)), ((), ()))


def _vmem_limit(nbytes):
    return int(min(nbytes + (8 << 20), V7X_VMEM_BYTES - (4 << 20)))


def _rms(x, g):
    y = x * lax.rsqrt(jnp.mean(x * x, axis=-1, keepdims=True) + EPS)
    return y * g


def _ffn_body(emit_next, x_ref, pre_g_ref, wg_ref, wu_ref, wd_ref, post_g_ref, *rest):
    if emit_next:
        next_g_ref, out_ref, u_ref, xn_ref, acc_ref = rest
    else:
        out_ref, xn_ref, acc_ref = rest
    j = pl.program_id(1)

    @pl.when(j == 0)
    def _():
        xn_ref[...] = _rms(x_ref[...], pre_g_ref[...]).astype(BF16)
        acc_ref[...] = jnp.zeros_like(acc_ref)

    xn = xn_ref[...]
    g = jnp.dot(xn, wg_ref[...], preferred_element_type=F32)
    u = jnp.dot(xn, wu_ref[...], preferred_element_type=F32)
    a = (g * jax.nn.sigmoid(g) * u).astype(BF16)
    acc_ref[...] += jnp.dot(a, wd_ref[...], preferred_element_type=F32)

    @pl.when(j == pl.num_programs(1) - 1)
    def _():
        h = x_ref[...] + FFN_RES_WEIGHT * _rms(acc_ref[...], post_g_ref[...])
        out_ref[...] = h
        if emit_next:
            u_ref[...] = _rms(h, next_g_ref[...]).astype(BF16)


def _ffn(x, pre_g, wg, wu, wd, post_g, next_g=None):
    t, d = x.shape
    dff = wg.shape[1]
    emit_next = next_g is not None
    row = pl.BlockSpec((FFN_TM, d), lambda i, j: (i, 0))
    gain = pl.BlockSpec((1, d), lambda i, j: (0, 0))
    in_specs = [row, gain,
                pl.BlockSpec((d, FFN_TF), lambda i, j: (0, j)),
                pl.BlockSpec((d, FFN_TF), lambda i, j: (0, j)),
                pl.BlockSpec((FFN_TF, d), lambda i, j: (j, 0)),
                gain]
    args = [x, pre_g, wg, wu, wd, post_g]
    out_shape = [jax.ShapeDtypeStruct((t, d), F32)]
    out_specs = [row]
    if emit_next:
        in_specs.append(gain)
        args.append(next_g)
        out_shape.append(jax.ShapeDtypeStruct((t, d), BF16))
        out_specs.append(row)
    vmem = (2 * FFN_TM * d * 4 * 2 + 2 * FFN_TM * d * 2 + FFN_TM * d * (2 + 4)
            + 2 * 3 * d * FFN_TF * 2 + 3 * FFN_TM * FFN_TF * 4)
    res = pl.pallas_call(
        functools.partial(_ffn_body, emit_next),
        grid=(t // FFN_TM, dff // FFN_TF),
        in_specs=in_specs, out_specs=out_specs, out_shape=out_shape,
        scratch_shapes=[pltpu.VMEM((FFN_TM, d), BF16), pltpu.VMEM((FFN_TM, d), F32)],
        compiler_params=pltpu.CompilerParams(
            dimension_semantics=("arbitrary", "arbitrary"), vmem_limit_bytes=_vmem_limit(vmem)),
        name="ffn_next" if emit_next else "ffn",
    )(*args)
    return res if emit_next else res[0]


def _proj_body(gate, x_ref, w_ref, *rest):
    if gate:
        b_ref, o_ref = rest
    else:
        (o_ref,) = rest
    y = jnp.dot(x_ref[...], w_ref[...], preferred_element_type=F32)
    if gate:
        y = jax.nn.sigmoid(y + b_ref[...])
    o_ref[...] = y.astype(o_ref.dtype)


def _proj(x, w, out_dtype, name, bias=None):
    t, k = x.shape
    n = w.shape[1]
    tn = min(n, 512)
    gate = bias is not None
    in_specs = [pl.BlockSpec((PROJ_TM, k), lambda i, j: (i, 0)),
                pl.BlockSpec((k, tn), lambda i, j: (0, j))]
    args = [x, w]
    if gate:
        in_specs.append(pl.BlockSpec((1, tn), lambda i, j: (0, j)))
        args.append(bias)
    vmem = 2 * PROJ_TM * k * 2 + 2 * k * tn * 2 + 3 * PROJ_TM * tn * 4
    return pl.pallas_call(
        functools.partial(_proj_body, gate),
        grid=(t // PROJ_TM, n // tn),
        in_specs=in_specs,
        out_specs=pl.BlockSpec((PROJ_TM, tn), lambda i, j: (i, j)),
        out_shape=jax.ShapeDtypeStruct((t, n), out_dtype),
        compiler_params=pltpu.CompilerParams(
            dimension_semantics=("arbitrary", "arbitrary"), vmem_limit_bytes=_vmem_limit(vmem)),
        name=name,
    )(*args)


def _proj_t_body(u_ref, wt_ref, wvt_ref, wwt_ref, o_ref, v_ref, w_ref):
    u = u_ref[...]
    for r in range(wt_ref.shape[0] // PROJT_ROWS):
        rows = slice(r * PROJT_ROWS, (r + 1) * PROJT_ROWS)
        o_ref[0, rows, :] = lax.dot_general(wt_ref[rows, :], u, _NT, preferred_element_type=F32).astype(BF16)
    heads_per_dot = PROJT_ROWS // HEAD_DIM
    for r in range(wvt_ref.shape[0] // PROJT_ROWS):
        vt = lax.dot_general(wvt_ref[r * PROJT_ROWS:(r + 1) * PROJT_ROWS, :], u, _NT,
                             preferred_element_type=F32).astype(BF16)
        for hh in range(heads_per_dot):
            h = r * heads_per_dot + hh
            v_ref[0, h * VT_ROWS:h * VT_ROWS + HEAD_DIM, :] = vt[hh * HEAD_DIM:(hh + 1) * HEAD_DIM, :]
            v_ref[0, h * VT_ROWS + HEAD_DIM:(h + 1) * VT_ROWS, :] = jnp.ones((VT_ROWS - HEAD_DIM, TILE), BF16)
    w_ref[0] = lax.dot_general(wwt_ref[...], u, _NT, preferred_element_type=F32)


def _proj_t(u, wt, wvt, wwt):
    t, k = u.shape
    n, nw = wt.shape[0], wwt.shape[0]
    nv = N_HEADS * VT_ROWS
    resident = pl.Buffered(1)
    vmem = (2 * TILE * k * 2 + (n + wvt.shape[0] + nw) * k * 2 + 2 * ((n + nv) * 2 + nw * 4) * TILE
            + 2 * PROJT_ROWS * TILE * 4)
    return pl.pallas_call(
        _proj_t_body,
        grid=(t // TILE,),
        in_specs=[pl.BlockSpec((TILE, k), lambda i: (i, 0)),
                  pl.BlockSpec((n, k), lambda i: (0, 0), pipeline_mode=resident),
                  pl.BlockSpec(wvt.shape, lambda i: (0, 0), pipeline_mode=resident),
                  pl.BlockSpec((nw, k), lambda i: (0, 0), pipeline_mode=resident)],
        out_specs=[pl.BlockSpec((1, n, TILE), lambda i: (i, 0, 0)),
                   pl.BlockSpec((1, nv, TILE), lambda i: (i, 0, 0)),
                   pl.BlockSpec((1, nw, TILE), lambda i: (i, 0, 0))],
        out_shape=[jax.ShapeDtypeStruct((t // TILE, n, TILE), BF16),
                   jax.ShapeDtypeStruct((t // TILE, nv, TILE), BF16),
                   jax.ShapeDtypeStruct((t // TILE, nw, TILE), F32)],
        compiler_params=pltpu.CompilerParams(
            dimension_semantics=("arbitrary",), vmem_limit_bytes=_vmem_limit(vmem)),
        name="proj_t",
    )(u, wt, wvt, wwt)


def _indexer_body(seq, qi_ref, w_ref, kw_ref, mask_ref, kb_ref, key_ref, hi_ref, lo_ref, mem_ref):
    i = pl.program_id(1)
    nch = i + 1
    n_tiles = seq // TILE

    @pl.when(i == 0)
    def _():
        kb_ref[...] = kw_ref[:, :IDX_DIM].astype(BF16)

    w = w_ref[0] * (IDX_HEADS ** -0.5 * IDX_DIM ** -0.5)
    q_pos = i * TILE + lax.broadcasted_iota(jnp.int32, (1, TILE), 1)

    def rows_of(c):
        return pl.ds(pl.multiple_of(c * TILE, TILE), TILE)

    def score_tile(c):
        ks = kb_ref[rows_of(c), :]
        acc = jnp.zeros((TILE, TILE), F32)
        for h in range(IDX_HEADS):
            d = jnp.dot(ks, qi_ref[0, h * IDX_DIM:(h + 1) * IDX_DIM, :], preferred_element_type=F32)
            acc = acc + jnp.maximum(d, 0.0) * w[h:h + 1, :]
        bits = pltpu.bitcast(acc, jnp.int32)
        key = bits ^ ((bits >> 31) & jnp.int32(0x7FFFFFFF))
        k_pos = c * TILE + lax.broadcasted_iota(jnp.int32, (TILE, 1), 0)
        key = jnp.where(k_pos <= q_pos, key, jnp.int32(INT_MIN))
        key_ref[rows_of(c), :] = key
        hi_ref[rows_of(c), :] = (key >> 16).astype(I16)
        lo_ref[rows_of(c), :] = ((key & jnp.int32(0xFFFF)) + I16_MIN).astype(I16)

    def score_pair(t, carry):
        score_tile(2 * t)
        score_tile(2 * t + 1)
        return carry

    lax.fori_loop(0, nch // 2, score_pair, 0)

    @pl.when(nch % 2 == 1)
    def _():
        score_tile(nch - 1)

    n_steps = (nch + COUNT_TILES - 1) // COUNT_TILES
    step_rows = COUNT_TILES * TILE
    groups = step_rows // PACKED_ROWS

    def rows_of_step(c):
        return pl.ds(pl.multiple_of(c * step_rows, step_rows), step_rows)

    def pad_chunk(c, carry):
        key_ref[rows_of(c), :] = jnp.full((TILE, TILE), INT_MIN, jnp.int32)
        hi_ref[rows_of(c), :] = jnp.full((TILE, TILE), I16_MIN, I16)
        lo_ref[rows_of(c), :] = jnp.full((TILE, TILE), I16_MIN, I16)
        return carry

    lax.fori_loop(nch, n_steps * COUNT_TILES, pad_chunk, 0)

    def packed(row):
        return jnp.broadcast_to(row, (PACKED_ROWS, TILE)).astype(I16)

    def count16(hit):
        ones = jnp.where(hit, jnp.int16(1), jnp.int16(0))
        parts = [ones[g] for g in range(4)]
        for g in range(4, groups):
            parts[g % 4] = parts[g % 4] + ones[g]
        return (parts[0] + parts[1]) + (parts[2] + parts[3])

    def total16(cnt):
        return jnp.sum(cnt.astype(jnp.int32), axis=0, keepdims=True)

    def load16(ref, c):
        return ref[rows_of_step(c), :].reshape(groups, PACKED_ROWS, TILE)

    def search16(src_ref, need):
        def search_pass(b, carry):
            u, n_u = carry
            cand_u = u | lax.shift_left(jnp.int32(1), 15 - b)
            cand = packed(cand_u + I16_MIN)
            cnt = lax.fori_loop(0, n_steps, lambda c, cnt: cnt + count16(load16(src_ref, c) >= cand[None]),
                                jnp.zeros((PACKED_ROWS, TILE), I16))
            total = total16(cnt)
            ok = total >= need
            return jnp.where(ok, cand_u, u), jnp.where(ok, total, n_u)

        zero = jnp.zeros((1, TILE), jnp.int32)
        return lax.fori_loop(0, 16, search_pass, (zero, zero))

    u_hi, _ = search16(hi_ref, jnp.int32(TOPK_MAX))
    t_hi = u_hi + I16_MIN
    t_hi16 = packed(t_hi)

    def member_chunk(c, carry):
        n_gt, n_mem = carry
        hi = load16(hi_ref, c)
        member = hi == t_hi16[None]
        mem_ref[rows_of_step(c), :] = jnp.where(member, load16(lo_ref, c), jnp.int16(I16_MIN)).reshape(step_rows, TILE)
        return n_gt + count16(hi > t_hi16[None]), n_mem + count16(member)

    zero16 = jnp.zeros((PACKED_ROWS, TILE), I16)
    n_gt, n_mem = lax.fori_loop(0, n_steps, member_chunk, (zero16, zero16))
    n_gt, n_mem = total16(n_gt), total16(n_mem)
    u_lo, n_lo = search16(mem_ref, TOPK_MAX - n_gt)
    thr = jnp.maximum(t_hi * 65536 + u_lo, jnp.int32(INT_MIN + 1))
    n_ge = jnp.where(u_hi > 0, n_gt + jnp.where(u_lo > 0, n_lo, n_mem), 0)

    def count32(pred):
        def count_chunk(c, cnt):
            hit = jnp.where(pred(key_ref[rows_of_step(c), :]), 1, 0)
            return cnt + jnp.sum(hit.reshape(step_rows // SUBLANES, SUBLANES, TILE), axis=0)

        cnt = lax.fori_loop(0, n_steps, count_chunk, jnp.zeros((SUBLANES, TILE), jnp.int32))
        return jnp.sum(cnt, axis=0, keepdims=True)

    has_ties = jnp.max(n_ge) > TOPK_MAX

    @pl.when(jnp.logical_not(has_ties))
    def _():
        def write_chunk(c, carry):
            mask_ref[0, 0, rows_of(c), :] = jnp.where(key_ref[rows_of(c), :] >= thr, 0.0, MASKED).astype(BF16)
            return carry

        lax.fori_loop(0, nch, write_chunk, 0)

    @pl.when(has_ties)
    def _():
        keep = (TOPK_MAX - count32(lambda key: key > thr)).astype(F32)
        below = jnp.where(lax.broadcasted_iota(jnp.int32, (TILE, TILE), 0)
                          >= lax.broadcasted_iota(jnp.int32, (TILE, TILE), 1), 1.0, 0.0).astype(BF16)

        def write_chunk(c, seen):
            key = key_ref[rows_of(c), :]
            tied = key == thr
            rank = jnp.dot(below, jnp.where(tied, 1.0, 0.0).astype(BF16), preferred_element_type=F32) + seen
            m = jnp.where(key > thr, 0.0, jnp.where(tied, jnp.where(rank <= keep, 0.0, MASKED), MASKED))
            mask_ref[0, 0, rows_of(c), :] = m.astype(BF16)
            return rank[TILE - 1:TILE, :]

        lax.fori_loop(0, nch, write_chunk, jnp.zeros((1, TILE), F32))

    def fill_chunk(c, carry):
        mask_ref[0, 0, rows_of(c), :] = jnp.full((TILE, TILE), MASKED, BF16)
        return carry

    lax.fori_loop(nch, n_tiles, fill_chunk, 0)


def _indexer(qt, wt, kw, batch, seq):
    nq = seq // TILE
    qi_block = 1
    vmem = (2 * IDX_HEADS * IDX_DIM * TILE * 2 + 2 * seq * LANES * 4 + 2 * seq * TILE * 2
            + seq * LANES * 2 + seq * TILE * (4 + 2 + 2 + 2) + 8 * TILE * TILE * 4)
    return pl.pallas_call(
        functools.partial(_indexer_body, seq),
        grid=(batch, nq),
        in_specs=[pl.BlockSpec((1, IDX_HEADS * IDX_DIM, TILE), lambda b, i: (b * nq + i, qi_block, 0)),
                  pl.BlockSpec((1, IDX_HEADS, TILE), lambda b, i: (b * nq + i, 0, 0)),
                  pl.BlockSpec((seq, LANES), lambda b, i: (b, 0))],
        out_specs=pl.BlockSpec((1, 1, seq, TILE), lambda b, i: (b, i, 0, 0)),
        out_shape=jax.ShapeDtypeStruct((batch, nq, seq, TILE), BF16),
        scratch_shapes=[pltpu.VMEM((seq, IDX_DIM), BF16),
                        pltpu.VMEM((seq, TILE), jnp.int32),
                        pltpu.VMEM((seq, TILE), I16),
                        pltpu.VMEM((seq, TILE), I16),
                        pltpu.VMEM((seq, TILE), I16)],
        compiler_params=pltpu.CompilerParams(
            dimension_semantics=("arbitrary", "arbitrary"), vmem_limit_bytes=_vmem_limit(vmem)),
        name="indexer",
    )(qt, wt, kw)


def _bias_init(rb_ref, bias_ref):
    shape = (2 * TILE, TILE)
    dist = (lax.broadcasted_iota(jnp.int32, shape, 1) - lax.broadcasted_iota(jnp.int32, shape, 0) + TILE)
    max_exact = NUM_BUCKETS // 2
    n = jnp.maximum(dist, 0)
    nf = jnp.maximum(n, max_exact).astype(F32)
    large = max_exact + (jnp.log(nf / max_exact) / math.log(MAX_DISTANCE / max_exact)
                         * (NUM_BUCKETS - max_exact)).astype(jnp.int32)
    large = jnp.minimum(large, NUM_BUCKETS - 1)
    bucket = jnp.where(n < max_exact, n, large)
    for h in range(N_HEADS):
        far = rb_ref[NUM_BUCKETS - 1, h]
        acc = jnp.zeros(shape, F32)
        for b in range(NUM_BUCKETS - 1):
            acc = jnp.where(bucket == b, (rb_ref[b, h] - far) * LOG2E, acc)
        bias_ref[h] = acc


def _attn_body(rb_ref, q_ref, k_ref, vt_ref, mask_ref, o_ref, bias_ref, acc_ref, m_ref, l_ref, alpha_ref,
               p_ref):
    b = pl.program_id(0)
    i = pl.program_id(1)

    @pl.when((b == 0) & (i == 0))
    def _():
        _bias_init(rb_ref, bias_ref)

    acc_ref[...] = jnp.zeros_like(acc_ref)
    m_ref[...] = jnp.full_like(m_ref, MASKED)
    l_ref[...] = jnp.zeros_like(l_ref)

    eye = jnp.where(lax.broadcasted_iota(jnp.int32, (HEAD_DIM, HEAD_DIM), 0)
                    == lax.broadcasted_iota(jnp.int32, (HEAD_DIM, HEAD_DIM), 1), 1.0, 0.0).astype(BF16)

    def numerators(j, near, slot):
        halves = [pl.ds(pl.multiple_of(j * TILE + r * HEAD_DIM, HEAD_DIM), HEAD_DIM)
                  for r in range(TILE // HEAD_DIM)]
        masks = [mask_ref[0, 0, rows, :] for rows in halves]
        for h in range(N_HEADS):
            feat = slice(h * HEAD_DIM, (h + 1) * HEAD_DIM)
            s_parts = []
            for r, rows in enumerate(halves):
                lhs = jnp.concatenate([k_ref[rows, feat], eye], axis=1)
                rhs = jnp.concatenate([q_ref[0, feat, :], masks[r]], axis=0)
                s = jnp.dot(lhs, rhs, preferred_element_type=F32)
                if near is not None:
                    s = s + bias_ref[h, near * TILE + r * HEAD_DIM:near * TILE + (r + 1) * HEAD_DIM, :]
                s_parts.append(s)
            for c in range(TILE // LANES):
                lanes = slice(c * LANES, (c + 1) * LANES)
                sc = jnp.concatenate([s[:, lanes] for s in s_parts], axis=0)
                m_old = m_ref[h:h + 1, lanes]
                m_new = jnp.maximum(m_old, jnp.max(sc, axis=0, keepdims=True))
                p_ref[slot, h, :, lanes] = jnp.exp2(sc - m_new).astype(BF16)
                alpha_ref[slot, h:h + 1, lanes] = jnp.exp2(m_old - m_new)
                m_ref[h:h + 1, lanes] = m_new

    def values(j, slot):
        for h in range(N_HEADS):
            feat = slice(h * HEAD_DIM, (h + 1) * HEAD_DIM)
            pv = jnp.dot(vt_ref[j, h * VT_ROWS:(h + 1) * VT_ROWS, :], p_ref[slot, h], preferred_element_type=F32)
            alpha = alpha_ref[slot, h:h + 1, :]
            acc_ref[feat, :] = alpha * acc_ref[feat, :] + pv[:HEAD_DIM, :]
            l_ref[h:h + 1, :] = alpha * l_ref[h:h + 1, :] + pv[HEAD_DIM:HEAD_DIM + 1, :]

    def pair(j0, near0, j1, near1):
        numerators(j0, near0, 0)
        numerators(j1, near1, 1)
        values(j0, 0)
        values(j1, 1)

    n_far = jnp.maximum(i - 1, 0)

    def far_pair(t, carry):
        pair(2 * t, None, 2 * t + 1, None)
        return carry

    lax.fori_loop(0, n_far // 2, far_pair, 0)

    @pl.when(n_far % 2 == 1)
    def _():
        numerators(n_far - 1, None, 0)
        values(n_far - 1, 0)

    @pl.when(i >= 1)
    def _():
        pair(i - 1, 0, i, 1)

    @pl.when(i == 0)
    def _():
        numerators(i, 1, 0)
        values(i, 0)

    for h in range(N_HEADS):
        feat = slice(h * HEAD_DIM, (h + 1) * HEAD_DIM)
        o_ref[:, feat] = (acc_ref[feat, :] / l_ref[h:h + 1, :]).T.astype(o_ref.dtype)


def _attention(qt, k, vt, mask, rel_bias, batch, seq):
    nq = seq // TILE
    vmem = (2 * ATTN_DIM * TILE * 2 * 2 + seq * (ATTN_DIM + N_HEADS * VT_ROWS) * 2 + 2 * seq * TILE * 2
            + N_HEADS * 2 * TILE * TILE * 4 + ATTN_DIM * TILE * 4 + N_HEADS * TILE * TILE * (4 + 2)
            + 4 * TILE * TILE * 4)
    resident = pl.Buffered(1)
    return pl.pallas_call(
        _attn_body,
        grid=(batch, nq),
        in_specs=[pl.BlockSpec(memory_space=pltpu.SMEM),
                  pl.BlockSpec((1, ATTN_DIM, TILE), lambda b, i: (b * nq + i, 0, 0)),
                  pl.BlockSpec((seq, ATTN_DIM), lambda b, i: (b, 0), pipeline_mode=resident),
                  pl.BlockSpec((nq, N_HEADS * VT_ROWS, TILE), lambda b, i: (b, 0, 0), pipeline_mode=resident),
                  pl.BlockSpec((1, 1, seq, TILE), lambda b, i: (b, i, 0, 0))],
        out_specs=pl.BlockSpec((TILE, ATTN_DIM), lambda b, i: (b * nq + i, 0)),
        out_shape=jax.ShapeDtypeStruct((batch * seq, ATTN_DIM), BF16),
        scratch_shapes=[pltpu.VMEM((N_HEADS, 2 * TILE, TILE), F32),
                        pltpu.VMEM((ATTN_DIM, TILE), F32),
                        pltpu.VMEM((N_HEADS, TILE), F32),
                        pltpu.VMEM((N_HEADS, TILE), F32),
                        pltpu.VMEM((2, N_HEADS, TILE), F32),
                        pltpu.VMEM((2, N_HEADS, TILE, TILE), BF16)],
        compiler_params=pltpu.CompilerParams(
            dimension_semantics=("arbitrary", "arbitrary"), vmem_limit_bytes=_vmem_limit(vmem)),
        name="attention",
    )(rel_bias, qt, k, vt, mask)


def _conv_body(cur_ref, halo_ref, dw_ref, dwb_ref, lng_ref, lnb_ref, o_ref, h_ref, y_ref):
    i = pl.program_id(1)

    def glu(x):
        return x[:, :CONV_CH].astype(F32) * jax.nn.sigmoid(x[:, CONV_CH:].astype(F32))

    h_ref[0, :CONV_HALO, :] = jnp.where(i > 0, glu(halo_ref[...]), 0.0)
    h_ref[0, CONV_HALO:, :] = glu(cur_ref[...])
    n_rows = CONV_TS + CONV_HALO
    for r in range(1, SUBLANES):
        h_ref[r, SUBLANES:, :] = h_ref[0, SUBLANES - r:n_rows - r, :]

    for rb in range(CONV_TS // CONV_RB):
        for c in range(CONV_CH // CONV_CB):
            cols = slice(c * CONV_CB, (c + 1) * CONV_CB)
            acc = jnp.broadcast_to(dwb_ref[:, cols], (CONV_RB, CONV_CB))
            for j in range(CONV_WIDTH):
                groups, r = divmod(CONV_WIDTH - 1 - j, SUBLANES)
                start = CONV_HALO - groups * SUBLANES + rb * CONV_RB
                acc = acc + h_ref[r, start:start + CONV_RB, cols] * dw_ref[j:j + 1, cols]
            y_ref[rb * CONV_RB:(rb + 1) * CONV_RB, cols] = acc

    y = y_ref[...]
    mu = jnp.mean(y, axis=-1, keepdims=True)
    var = jnp.mean(jnp.square(y - mu), axis=-1, keepdims=True)
    z = (y - mu) * lax.rsqrt(var + EPS) * lng_ref[...] + lnb_ref[...]
    o_ref[...] = (z * jax.nn.sigmoid(z)).astype(o_ref.dtype)


def _conv_module(glu_in, dw, dw_b, ln_g, ln_b, batch, seq):
    ns = seq // CONV_TS
    halo_per_tile = CONV_TS // CONV_HALO
    vec = pl.BlockSpec((1, CONV_CH), lambda b, i: (0, 0))
    vmem = (2 * (CONV_TS + CONV_HALO) * 2 * CONV_CH * 2 + 2 * CONV_TS * CONV_CH * 2
            + ((SUBLANES + 1) * CONV_TS + SUBLANES * CONV_HALO) * CONV_CH * 4 + 4 * CONV_TS * CONV_CH * 4)
    return pl.pallas_call(
        _conv_body,
        grid=(batch, ns),
        in_specs=[pl.BlockSpec((CONV_TS, 2 * CONV_CH), lambda b, i: (b * ns + i, 0)),
                  pl.BlockSpec((CONV_HALO, 2 * CONV_CH),
                               lambda b, i: (jnp.maximum((b * ns + i) * halo_per_tile - 1, 0), 0)),
                  pl.BlockSpec((CONV_WIDTH, CONV_CH), lambda b, i: (0, 0)),
                  vec, vec, vec],
        out_specs=pl.BlockSpec((CONV_TS, CONV_CH), lambda b, i: (b * ns + i, 0)),
        out_shape=jax.ShapeDtypeStruct((batch * seq, CONV_CH), BF16),
        scratch_shapes=[pltpu.VMEM((SUBLANES, CONV_TS + CONV_HALO, CONV_CH), F32),
                        pltpu.VMEM((CONV_TS, CONV_CH), F32)],
        compiler_params=pltpu.CompilerParams(
            dimension_semantics=("arbitrary", "arbitrary"), vmem_limit_bytes=_vmem_limit(vmem)),
        name="conv_module",
    )(glu_in, glu_in, dw, dw_b, ln_g, ln_b)


def _mix_body(attn_ref, conv_ref, gates_ref, h_ref, wo_ref, wpw_ref, wout_ref, g_ref, o_ref):
    a = jnp.dot(attn_ref[...], wo_ref[...], preferred_element_type=F32)
    c = jnp.dot(conv_ref[...], wpw_ref[...], preferred_element_type=F32)
    mixed = (gates_ref[:, :D_MODEL].astype(F32) * a + gates_ref[:, D_MODEL:].astype(F32) * c).astype(BF16)
    y = jnp.dot(mixed, wout_ref[...], preferred_element_type=F32)
    o_ref[...] = h_ref[...] + _rms(y, g_ref[...])


def _mix_out(attn, conv, gates, h, w_o, w_pw2, w_out, post_g):
    t = h.shape[0]
    resident = pl.Buffered(1)

    def rows(n):
        return pl.BlockSpec((OUT_TM, n), lambda i: (i, 0))

    def whole(a):
        return pl.BlockSpec(a.shape, lambda i: (0, 0), pipeline_mode=resident)

    vmem = ((w_o.size + w_pw2.size + w_out.size) * 2
            + 2 * OUT_TM * (2 * ATTN_DIM * 2 + 2 * D_MODEL * 2 + 2 * D_MODEL * 4) + 5 * OUT_TM * D_MODEL * 4)
    return pl.pallas_call(
        _mix_body,
        grid=(t // OUT_TM,),
        in_specs=[rows(ATTN_DIM), rows(CONV_CH), rows(2 * D_MODEL), rows(D_MODEL),
                  whole(w_o), whole(w_pw2), whole(w_out), pl.BlockSpec((1, D_MODEL), lambda i: (0, 0))],
        out_specs=rows(D_MODEL),
        out_shape=jax.ShapeDtypeStruct((t, D_MODEL), F32),
        compiler_params=pltpu.CompilerParams(
            dimension_semantics=("arbitrary",), vmem_limit_bytes=_vmem_limit(vmem)),
        name="mix_out",
    )(attn, conv, gates, h, w_o, w_pw2, w_out, post_g)


def kernel(x, rel_bias, ffn1_pre_g, ffn1_wg, ffn1_wu, ffn1_wd, ffn1_post_g, mix_pre_g, w_in, b_gate, w_o,
           conv_dw, conv_dw_b, conv_ln_g, conv_ln_b, w_pw2, w_out, mix_post_g, ffn2_pre_g, ffn2_wg, ffn2_wu,
           ffn2_wd, ffn2_post_g):
    batch, seq, d = x.shape
    depth = ffn1_wg.shape[0]
    h = x.reshape(batch * seq, d)
    o_k = ATTN_DIM
    o_v = o_k + ATTN_DIM
    o_qi = o_v + ATTN_DIM
    o_ki = o_qi + IDX_HEADS * IDX_DIM
    o_wi = o_ki + IDX_DIM
    o_glu = o_wi + IDX_HEADS
    o_gates = o_glu + 2 * CONV_CH
    for l in range(depth):
        wl = w_in[l]
        w_t = jnp.concatenate([wl[:, :o_k] * (HEAD_DIM ** -0.5 * LOG2E), wl[:, o_qi:o_ki]],
                              axis=1).T.astype(BF16)
        w_vt = wl[:, o_v:o_qi].T.astype(BF16)
        w_wt = wl[:, o_wi:o_glu].T.astype(BF16)
        w_k = wl[:, o_k:o_v].astype(BF16)
        w_ki = jnp.pad(wl[:, o_ki:o_wi], ((0, 0), (0, LANES - IDX_DIM))).astype(BF16)
        w_glu = wl[:, o_glu:o_gates].astype(BF16)
        w_gates = wl[:, o_gates:].astype(BF16)

        h, u = _ffn(h, ffn1_pre_g[l][None], ffn1_wg[l].astype(BF16), ffn1_wu[l].astype(BF16),
                    ffn1_wd[l].astype(BF16), ffn1_post_g[l][None], mix_pre_g[l][None])

        qt, vt, wt = _proj_t(u, w_t, w_vt, w_wt)
        k = _proj(u, w_k, BF16, "proj_k")
        kw = _proj(u, w_ki, F32, "proj_ki")
        glu_in = _proj(u, w_glu, BF16, "proj_glu")
        gates = _proj(u, w_gates, BF16, "proj_gates", bias=b_gate[l][None])

        mask = _indexer(qt, wt, kw, batch, seq)
        attn = _attention(qt, k, vt, mask, rel_bias, batch, seq)
        conv = _conv_module(glu_in, conv_dw[l], conv_dw_b[l][None], conv_ln_g[l][None], conv_ln_b[l][None],
                            batch, seq)
        h = _mix_out(attn, conv, gates, h, w_o[l].astype(BF16), w_pw2[l].astype(BF16),
                     w_out[l].astype(BF16), mix_post_g[l][None])

        h = _ffn(h, ffn2_pre_g[l][None], ffn2_wg[l].astype(BF16), ffn2_wu[l].astype(BF16),
                 ffn2_wd[l].astype(BF16), ffn2_post_g[l][None])
    return h.reshape(batch, seq, d)
```

```python
import functools
import math

import jax
import jax.numpy as jnp
from jax import lax
from jax.experimental import pallas as pl
from jax.experimental.pallas import tpu as pltpu

D_MODEL = 2048
N_HEADS = 8
HEAD_DIM = 128
ATTN_DIM = N_HEADS * HEAD_DIM
IDX_HEADS = 16
IDX_DIM = 64
TOPK_MAX = 256
NUM_BUCKETS = 32
MAX_DISTANCE = 128
CONV_CH = 1024
CONV_WIDTH = 31
D_FF = 5632
FFN_RES_WEIGHT = 0.5
EPS = 1e-6

F32 = jnp.float32
BF16 = jnp.bfloat16
LANES = 128
SUBLANES = 8
V7X_VMEM_BYTES = 64 * 1024 * 1024
MASKED = -1e30
INT_MIN = -(2 ** 31)
I16 = jnp.int16
I16_MIN = -(2 ** 15)
PACKED_ROWS = 16
LOG2E = math.log2(math.e)

FFN_TM, FFN_TF = 512, 512
PROJ_TM, PROJ_TN = 1024, 1024
TILE = 256
PROJT_ROWS = 512
WT_COLS = 512
COUNT_TILES = 4
VT_ROWS = HEAD_DIM + PACKED_ROWS
CONV_TS = 512
CONV_HALO = 32
CONV_RB, CONV_CB = 128, 256
OUT_TM = 256

_NT = (((1,), (1,)), ((), ()))


def _vmem_limit(nbytes):
    return int(min(nbytes + (8 << 20), V7X_VMEM_BYTES - (4 << 20)))


def _rms(x, g):
    y = x * lax.rsqrt(jnp.mean(x * x, axis=-1, keepdims=True) + EPS)
    return y * g


def _ffn_body(emit_next, x_ref, pre_g_ref, wg_ref, wu_ref, wd_ref, post_g_ref, *rest):
    if emit_next:
        next_g_ref, out_ref, u_ref, xn_ref, acc_ref = rest
    else:
        out_ref, xn_ref, acc_ref = rest
    j = pl.program_id(1)

    @pl.when(j == 0)
    def _():
        xn_ref[...] = _rms(x_ref[...], pre_g_ref[...]).astype(BF16)
        acc_ref[...] = jnp.zeros_like(acc_ref)

    xn = xn_ref[...]
    g = jnp.dot(xn, wg_ref[...], preferred_element_type=F32)
    u = jnp.dot(xn, wu_ref[...], preferred_element_type=F32)
    a = (g * jax.nn.sigmoid(g) * u).astype(BF16)
    acc_ref[...] += jnp.dot(a, wd_ref[...], preferred_element_type=F32)

    @pl.when(j == pl.num_programs(1) - 1)
    def _():
        h = x_ref[...] + FFN_RES_WEIGHT * _rms(acc_ref[...], post_g_ref[...])
        out_ref[...] = h
        if emit_next:
            u_ref[...] = _rms(h, next_g_ref[...]).astype(BF16)


def _ffn(x, pre_g, wg, wu, wd, post_g, next_g=None):
    t, d = x.shape
    dff = wg.shape[1]
    emit_next = next_g is not None
    row = pl.BlockSpec((FFN_TM, d), lambda i, j: (i, 0))
    gain = pl.BlockSpec((1, d), lambda i, j: (0, 0))
    in_specs = [row, gain,
                pl.BlockSpec((d, FFN_TF), lambda i, j: (0, j)),
                pl.BlockSpec((d, FFN_TF), lambda i, j: (0, j)),
                pl.BlockSpec((FFN_TF, d), lambda i, j: (j, 0)),
                gain]
    args = [x, pre_g, wg, wu, wd, post_g]
    out_shape = [jax.ShapeDtypeStruct((t, d), F32)]
    out_specs = [row]
    if emit_next:
        in_specs.append(gain)
        args.append(next_g)
        out_shape.append(jax.ShapeDtypeStruct((t, d), BF16))
        out_specs.append(row)
    vmem = (2 * FFN_TM * d * 4 * 2 + 2 * FFN_TM * d * 2 + FFN_TM * d * (2 + 4)
            + 2 * 3 * d * FFN_TF * 2 + 3 * FFN_TM * FFN_TF * 4)
    res = pl.pallas_call(
        functools.partial(_ffn_body, emit_next),
        grid=(t // FFN_TM, dff // FFN_TF),
        in_specs=in_specs, out_specs=out_specs, out_shape=out_shape,
        scratch_shapes=[pltpu.VMEM((FFN_TM, d), BF16), pltpu.VMEM((FFN_TM, d), F32)],
        compiler_params=pltpu.CompilerParams(
            dimension_semantics=("arbitrary", "arbitrary"), vmem_limit_bytes=_vmem_limit(vmem)),
        name="ffn_next" if emit_next else "ffn",
    )(*args)
    return res if emit_next else res[0]


def _proj_body(gate, x_ref, w_ref, *rest):
    if gate:
        b_ref, o_ref = rest
    else:
        (o_ref,) = rest
    y = jnp.dot(x_ref[...], w_ref[...], preferred_element_type=F32)
    if gate:
        y = jax.nn.sigmoid(y + b_ref[...])
    o_ref[...] = y.astype(o_ref.dtype)


def _proj(x, w, out_dtype, name, bias=None):
    t, k = x.shape
    n = w.shape[1]
    tn = min(n, PROJ_TN)
    gate = bias is not None
    in_specs = [pl.BlockSpec((PROJ_TM, k), lambda i, j: (i, 0)),
                pl.BlockSpec((k, tn), lambda i, j: (0, j))]
    args = [x, w]
    if gate:
        in_specs.append(pl.BlockSpec((1, tn), lambda i, j: (0, j)))
        args.append(bias)
    vmem = 2 * PROJ_TM * k * 2 + 2 * k * tn * 2 + 3 * PROJ_TM * tn * 4
    return pl.pallas_call(
        functools.partial(_proj_body, gate),
        grid=(t // PROJ_TM, n // tn),
        in_specs=in_specs,
        out_specs=pl.BlockSpec((PROJ_TM, tn), lambda i, j: (i, j)),
        out_shape=jax.ShapeDtypeStruct((t, n), out_dtype),
        compiler_params=pltpu.CompilerParams(
            dimension_semantics=("arbitrary", "arbitrary"), vmem_limit_bytes=_vmem_limit(vmem)),
        name=name,
    )(*args)


def _weight_t_body(scale, w_ref, o_ref):
    o_ref[...] = (w_ref[...] * scale).T.astype(BF16)


def _weight_t(w, col0, ncols, scale=1.0):
    k = w.shape[0]
    assert col0 % WT_COLS == 0 and ncols % WT_COLS == 0
    vmem = 2 * k * WT_COLS * (4 + 2) + 2 * k * WT_COLS * 4
    return pl.pallas_call(
        functools.partial(_weight_t_body, scale),
        grid=(ncols // WT_COLS,),
        in_specs=[pl.BlockSpec((k, WT_COLS), lambda j: (0, col0 // WT_COLS + j))],
        out_specs=pl.BlockSpec((WT_COLS, k), lambda j: (j, 0)),
        out_shape=jax.ShapeDtypeStruct((ncols, k), BF16),
        compiler_params=pltpu.CompilerParams(
            dimension_semantics=("arbitrary",), vmem_limit_bytes=_vmem_limit(vmem)),
        name="weight_t",
    )(w)


def _proj_t_body(u_ref, wt_ref, wvt_ref, wwt_ref, o_ref, v_ref, w_ref):
    u = u_ref[...]
    for r in range(wt_ref.shape[0] // PROJT_ROWS):
        rows = slice(r * PROJT_ROWS, (r + 1) * PROJT_ROWS)
        o_ref[0, rows, :] = lax.dot_general(wt_ref[rows, :], u, _NT, preferred_element_type=F32).astype(BF16)
    heads_per_dot = PROJT_ROWS // HEAD_DIM
    for r in range(wvt_ref.shape[0] // PROJT_ROWS):
        vt = lax.dot_general(wvt_ref[r * PROJT_ROWS:(r + 1) * PROJT_ROWS, :], u, _NT,
                             preferred_element_type=F32).astype(BF16)
        for hh in range(heads_per_dot):
            h = r * heads_per_dot + hh
            v_ref[0, h * VT_ROWS:h * VT_ROWS + HEAD_DIM, :] = vt[hh * HEAD_DIM:(hh + 1) * HEAD_DIM, :]
            v_ref[0, h * VT_ROWS + HEAD_DIM:(h + 1) * VT_ROWS, :] = jnp.ones((VT_ROWS - HEAD_DIM, TILE), BF16)
    w_ref[0] = lax.dot_general(wwt_ref[...], u, _NT, preferred_element_type=F32)


def _proj_t(u, wt, wvt, wwt):
    t, k = u.shape
    n, nw = wt.shape[0], wwt.shape[0]
    nv = N_HEADS * VT_ROWS
    resident = pl.Buffered(1)
    vmem = (2 * TILE * k * 2 + (n + wvt.shape[0] + nw) * k * 2 + 2 * ((n + nv) * 2 + nw * 4) * TILE
            + 2 * PROJT_ROWS * TILE * 4)
    return pl.pallas_call(
        _proj_t_body,
        grid=(t // TILE,),
        in_specs=[pl.BlockSpec((TILE, k), lambda i: (i, 0)),
                  pl.BlockSpec((n, k), lambda i: (0, 0), pipeline_mode=resident),
                  pl.BlockSpec(wvt.shape, lambda i: (0, 0), pipeline_mode=resident),
                  pl.BlockSpec((nw, k), lambda i: (0, 0), pipeline_mode=resident)],
        out_specs=[pl.BlockSpec((1, n, TILE), lambda i: (i, 0, 0)),
                   pl.BlockSpec((1, nv, TILE), lambda i: (i, 0, 0)),
                   pl.BlockSpec((1, nw, TILE), lambda i: (i, 0, 0))],
        out_shape=[jax.ShapeDtypeStruct((t // TILE, n, TILE), BF16),
                   jax.ShapeDtypeStruct((t // TILE, nv, TILE), BF16),
                   jax.ShapeDtypeStruct((t // TILE, nw, TILE), F32)],
        compiler_params=pltpu.CompilerParams(
            dimension_semantics=("arbitrary",), vmem_limit_bytes=_vmem_limit(vmem)),
        name="proj_t",
    )(u, wt, wvt, wwt)


def _indexer_body(seq, qi_ref, w_ref, kw_ref, mask_ref, kb_ref, key_ref, hi_ref, lo_ref, mem_ref):
    i = pl.program_id(1)
    nch = i + 1
    n_tiles = seq // TILE

    @pl.when(i == 0)
    def _():
        kb_ref[...] = kw_ref[:, :IDX_DIM].astype(BF16)

    w = w_ref[0] * (IDX_HEADS ** -0.5 * IDX_DIM ** -0.5)
    q_pos = i * TILE + lax.broadcasted_iota(jnp.int32, (1, TILE), 1)

    def rows_of(c):
        return pl.ds(pl.multiple_of(c * TILE, TILE), TILE)

    def score_tile(c):
        ks = kb_ref[rows_of(c), :]
        acc = jnp.zeros((TILE, TILE), F32)
        for h in range(IDX_HEADS):
            d = jnp.dot(ks, qi_ref[0, h * IDX_DIM:(h + 1) * IDX_DIM, :], preferred_element_type=F32)
            acc = acc + jnp.maximum(d, 0.0) * w[h:h + 1, :]
        bits = pltpu.bitcast(acc, jnp.int32)
        key = bits ^ ((bits >> 31) & jnp.int32(0x7FFFFFFF))
        k_pos = c * TILE + lax.broadcasted_iota(jnp.int32, (TILE, 1), 0)
        key = jnp.where(k_pos <= q_pos, key, jnp.int32(INT_MIN))
        key_ref[rows_of(c), :] = key
        hi_ref[rows_of(c), :] = (key >> 16).astype(I16)
        lo_ref[rows_of(c), :] = ((key & jnp.int32(0xFFFF)) + I16_MIN).astype(I16)

    def score_pair(t, carry):
        score_tile(2 * t)
        score_tile(2 * t + 1)
        return carry

    lax.fori_loop(0, nch // 2, score_pair, 0)

    @pl.when(nch % 2 == 1)
    def _():
        score_tile(nch - 1)

    n_steps = (nch + COUNT_TILES - 1) // COUNT_TILES
    step_rows = COUNT_TILES * TILE

    def rows_of_step(c):
        return pl.ds(pl.multiple_of(c * step_rows, step_rows), step_rows)

    def pad_chunk(c, carry):
        key_ref[rows_of(c), :] = jnp.full((TILE, TILE), INT_MIN, jnp.int32)
        hi_ref[rows_of(c), :] = jnp.full((TILE, TILE), I16_MIN, I16)
        lo_ref[rows_of(c), :] = jnp.full((TILE, TILE), I16_MIN, I16)
        return carry

    lax.fori_loop(nch, n_steps * COUNT_TILES, pad_chunk, 0)

    def packed(row):
        return jnp.broadcast_to(row, (PACKED_ROWS, TILE)).astype(I16)

    tile_groups = TILE // PACKED_ROWS

    def load16(ref, tile):
        return ref[rows_of(tile), :].reshape(tile_groups, PACKED_ROWS, TILE)

    def count16(c, hit_of_tile):
        parts = [None] * 4
        for t in range(COUNT_TILES):
            ones = jnp.where(hit_of_tile(c * COUNT_TILES + t), jnp.int16(1), jnp.int16(0))
            for g in range(tile_groups):
                parts[g % 4] = ones[g] if parts[g % 4] is None else parts[g % 4] + ones[g]
        return (parts[0] + parts[1]) + (parts[2] + parts[3])

    def total16(cnt):
        return jnp.sum(cnt.astype(jnp.int32), axis=0, keepdims=True)

    def search16(src_ref, need):
        def search_pass(b, carry):
            u, n_u = carry
            cand_u = u | lax.shift_left(jnp.int32(1), 15 - b)
            cand = packed(cand_u + I16_MIN)
            cnt = lax.fori_loop(
                0, n_steps, lambda c, cnt: cnt + count16(c, lambda tile: load16(src_ref, tile) >= cand[None]),
                jnp.zeros((PACKED_ROWS, TILE), I16))
            total = total16(cnt)
            ok = total >= need
            return jnp.where(ok, cand_u, u), jnp.where(ok, total, n_u)

        zero = jnp.zeros((1, TILE), jnp.int32)
        return lax.fori_loop(0, 16, search_pass, (zero, zero))

    u_hi, _ = search16(hi_ref, jnp.int32(TOPK_MAX))
    t_hi = u_hi + I16_MIN
    t_hi16 = packed(t_hi)

    def member_chunk(c, carry):
        n_gt, n_mem = carry

        def members(tile):
            member = load16(hi_ref, tile) == t_hi16[None]
            mem_ref[rows_of(tile), :] = jnp.where(member, load16(lo_ref, tile), jnp.int16(I16_MIN)).reshape(TILE, TILE)
            return member

        n_mem = n_mem + count16(c, members)
        return n_gt + count16(c, lambda tile: load16(hi_ref, tile) > t_hi16[None]), n_mem

    zero16 = jnp.zeros((PACKED_ROWS, TILE), I16)
    n_gt, n_mem = lax.fori_loop(0, n_steps, member_chunk, (zero16, zero16))
    n_gt, n_mem = total16(n_gt), total16(n_mem)
    u_lo, n_lo = search16(mem_ref, TOPK_MAX - n_gt)
    thr = jnp.maximum(t_hi * 65536 + u_lo, jnp.int32(INT_MIN + 1))
    n_ge = jnp.where(u_hi > 0, n_gt + jnp.where(u_lo > 0, n_lo, n_mem), 0)

    def count32(pred):
        def count_chunk(c, cnt):
            hit = jnp.where(pred(key_ref[rows_of_step(c), :]), 1, 0)
            return cnt + jnp.sum(hit.reshape(step_rows // SUBLANES, SUBLANES, TILE), axis=0)

        cnt = lax.fori_loop(0, n_steps, count_chunk, jnp.zeros((SUBLANES, TILE), jnp.int32))
        return jnp.sum(cnt, axis=0, keepdims=True)

    has_ties = jnp.max(n_ge) > TOPK_MAX

    @pl.when(jnp.logical_not(has_ties))
    def _():
        def write_chunk(c, carry):
            mask_ref[0, 0, rows_of(c), :] = jnp.where(key_ref[rows_of(c), :] >= thr, 0.0, MASKED).astype(BF16)
            return carry

        lax.fori_loop(0, nch, write_chunk, 0)

    @pl.when(has_ties)
    def _():
        keep = (TOPK_MAX - count32(lambda key: key > thr)).astype(F32)
        below = jnp.where(lax.broadcasted_iota(jnp.int32, (TILE, TILE), 0)
                          >= lax.broadcasted_iota(jnp.int32, (TILE, TILE), 1), 1.0, 0.0).astype(BF16)

        def write_chunk(c, seen):
            key = key_ref[rows_of(c), :]
            tied = key == thr
            rank = jnp.dot(below, jnp.where(tied, 1.0, 0.0).astype(BF16), preferred_element_type=F32) + seen
            m = jnp.where(key > thr, 0.0, jnp.where(tied, jnp.where(rank <= keep, 0.0, MASKED), MASKED))
            mask_ref[0, 0, rows_of(c), :] = m.astype(BF16)
            return rank[TILE - 1:TILE, :]

        lax.fori_loop(0, nch, write_chunk, jnp.zeros((1, TILE), F32))

    def fill_chunk(c, carry):
        mask_ref[0, 0, rows_of(c), :] = jnp.full((TILE, TILE), MASKED, BF16)
        return carry

    lax.fori_loop(nch, n_tiles, fill_chunk, 0)


def _indexer(qt, wt, kw, batch, seq):
    nq = seq // TILE
    qi_block = 1
    vmem = (2 * IDX_HEADS * IDX_DIM * TILE * 2 + 2 * seq * LANES * 4 + 2 * seq * TILE * 2
            + seq * LANES * 2 + seq * TILE * (4 + 2 + 2 + 2) + 8 * TILE * TILE * 4)
    return pl.pallas_call(
        functools.partial(_indexer_body, seq),
        grid=(batch, nq),
        in_specs=[pl.BlockSpec((1, IDX_HEADS * IDX_DIM, TILE), lambda b, i: (b * nq + i, qi_block, 0)),
                  pl.BlockSpec((1, IDX_HEADS, TILE), lambda b, i: (b * nq + i, 0, 0)),
                  pl.BlockSpec((seq, LANES), lambda b, i: (b, 0))],
        out_specs=pl.BlockSpec((1, 1, seq, TILE), lambda b, i: (b, i, 0, 0)),
        out_shape=jax.ShapeDtypeStruct((batch, nq, seq, TILE), BF16),
        scratch_shapes=[pltpu.VMEM((seq, IDX_DIM), BF16),
                        pltpu.VMEM((seq, TILE), jnp.int32),
                        pltpu.VMEM((seq, TILE), I16),
                        pltpu.VMEM((seq, TILE), I16),
                        pltpu.VMEM((seq, TILE), I16)],
        compiler_params=pltpu.CompilerParams(
            dimension_semantics=("arbitrary", "arbitrary"), vmem_limit_bytes=_vmem_limit(vmem)),
        name="indexer",
    )(qt, wt, kw)


def _bias_init(rb_ref, bias_ref):
    shape = (2 * TILE, TILE)
    dist = (lax.broadcasted_iota(jnp.int32, shape, 1) - lax.broadcasted_iota(jnp.int32, shape, 0) + TILE)
    max_exact = NUM_BUCKETS // 2
    n = jnp.maximum(dist, 0)
    nf = jnp.maximum(n, max_exact).astype(F32)
    large = max_exact + (jnp.log(nf / max_exact) / math.log(MAX_DISTANCE / max_exact)
                         * (NUM_BUCKETS - max_exact)).astype(jnp.int32)
    large = jnp.minimum(large, NUM_BUCKETS - 1)
    bucket = jnp.where(n < max_exact, n, large)
    for h in range(N_HEADS):
        far = rb_ref[NUM_BUCKETS - 1, h]
        acc = jnp.zeros(shape, F32)
        for b in range(NUM_BUCKETS - 1):
            acc = jnp.where(bucket == b, (rb_ref[b, h] - far) * LOG2E, acc)
        bias_ref[h] = acc


def _attn_body(rb_ref, q_ref, k_ref, vt_ref, mask_ref, o_ref, bias_ref, acc_ref, m_ref, l_ref, alpha_ref,
               p_ref):
    b = pl.program_id(0)
    i = pl.program_id(1)

    @pl.when((b == 0) & (i == 0))
    def _():
        _bias_init(rb_ref, bias_ref)

    acc_ref[...] = jnp.zeros_like(acc_ref)
    m_ref[...] = jnp.full_like(m_ref, MASKED)
    l_ref[...] = jnp.zeros_like(l_ref)

    eye = jnp.where(lax.broadcasted_iota(jnp.int32, (HEAD_DIM, HEAD_DIM), 0)
                    == lax.broadcasted_iota(jnp.int32, (HEAD_DIM, HEAD_DIM), 1), 1.0, 0.0).astype(BF16)

    def numerators(j, near, slot):
        halves = [pl.ds(pl.multiple_of(j * TILE + r * HEAD_DIM, HEAD_DIM), HEAD_DIM)
                  for r in range(TILE // HEAD_DIM)]
        masks = [mask_ref[0, 0, rows, :] for rows in halves]
        for h in range(N_HEADS):
            feat = slice(h * HEAD_DIM, (h + 1) * HEAD_DIM)
            s_parts = []
            for r, rows in enumerate(halves):
                lhs = jnp.concatenate([k_ref[rows, feat], eye], axis=1)
                rhs = jnp.concatenate([q_ref[0, feat, :], masks[r]], axis=0)
                s = jnp.dot(lhs, rhs, preferred_element_type=F32)
                if near is not None:
                    s = s + bias_ref[h, near * TILE + r * HEAD_DIM:near * TILE + (r + 1) * HEAD_DIM, :]
                s_parts.append(s)
            for c in range(TILE // LANES):
                lanes = slice(c * LANES, (c + 1) * LANES)
                sc = jnp.concatenate([s[:, lanes] for s in s_parts], axis=0)
                m_old = m_ref[h:h + 1, lanes]
                m_new = jnp.maximum(m_old, jnp.max(sc, axis=0, keepdims=True))
                p_ref[slot, h, :, lanes] = jnp.exp2(sc - m_new).astype(BF16)
                alpha_ref[slot, h:h + 1, lanes] = jnp.exp2(m_old - m_new)
                m_ref[h:h + 1, lanes] = m_new

    def values(j, slot):
        for h in range(N_HEADS):
            feat = slice(h * HEAD_DIM, (h + 1) * HEAD_DIM)
            pv = jnp.dot(vt_ref[j, h * VT_ROWS:(h + 1) * VT_ROWS, :], p_ref[slot, h], preferred_element_type=F32)
            alpha = alpha_ref[slot, h:h + 1, :]
            acc_ref[feat, :] = alpha * acc_ref[feat, :] + pv[:HEAD_DIM, :]
            l_ref[h:h + 1, :] = alpha * l_ref[h:h + 1, :] + pv[HEAD_DIM:HEAD_DIM + 1, :]

    def pair(j0, near0, j1, near1):
        numerators(j0, near0, 0)
        numerators(j1, near1, 1)
        values(j0, 0)
        values(j1, 1)

    n_far = jnp.maximum(i - 1, 0)

    def far_pair(t, carry):
        pair(2 * t, None, 2 * t + 1, None)
        return carry

    lax.fori_loop(0, n_far // 2, far_pair, 0)

    @pl.when(n_far % 2 == 1)
    def _():
        numerators(n_far - 1, None, 0)
        values(n_far - 1, 0)

    @pl.when(i >= 1)
    def _():
        pair(i - 1, 0, i, 1)

    @pl.when(i == 0)
    def _():
        numerators(i, 1, 0)
        values(i, 0)

    for h in range(N_HEADS):
        feat = slice(h * HEAD_DIM, (h + 1) * HEAD_DIM)
        o_ref[:, feat] = (acc_ref[feat, :] / l_ref[h:h + 1, :]).T.astype(o_ref.dtype)


def _attention(qt, k, vt, mask, rel_bias, batch, seq):
    nq = seq // TILE
    vmem = (2 * ATTN_DIM * TILE * 2 * 2 + seq * (ATTN_DIM + N_HEADS * VT_ROWS) * 2 + 2 * seq * TILE * 2
            + N_HEADS * 2 * TILE * TILE * 4 + ATTN_DIM * TILE * 4 + N_HEADS * TILE * TILE * (4 + 2)
            + 4 * TILE * TILE * 4)
    resident = pl.Buffered(1)
    return pl.pallas_call(
        _attn_body,
        grid=(batch, nq),
        in_specs=[pl.BlockSpec(memory_space=pltpu.SMEM),
                  pl.BlockSpec((1, ATTN_DIM, TILE), lambda b, i: (b * nq + i, 0, 0)),
                  pl.BlockSpec((seq, ATTN_DIM), lambda b, i: (b, 0), pipeline_mode=resident),
                  pl.BlockSpec((nq, N_HEADS * VT_ROWS, TILE), lambda b, i: (b, 0, 0), pipeline_mode=resident),
                  pl.BlockSpec((1, 1, seq, TILE), lambda b, i: (b, i, 0, 0))],
        out_specs=pl.BlockSpec((TILE, ATTN_DIM), lambda b, i: (b * nq + i, 0)),
        out_shape=jax.ShapeDtypeStruct((batch * seq, ATTN_DIM), BF16),
        scratch_shapes=[pltpu.VMEM((N_HEADS, 2 * TILE, TILE), F32),
                        pltpu.VMEM((ATTN_DIM, TILE), F32),
                        pltpu.VMEM((N_HEADS, TILE), F32),
                        pltpu.VMEM((N_HEADS, TILE), F32),
                        pltpu.VMEM((2, N_HEADS, TILE), F32),
                        pltpu.VMEM((2, N_HEADS, TILE, TILE), BF16)],
        compiler_params=pltpu.CompilerParams(
            dimension_semantics=("arbitrary", "arbitrary"), vmem_limit_bytes=_vmem_limit(vmem)),
        name="attention",
    )(rel_bias, qt, k, vt, mask)


def _conv_body(cur_ref, halo_ref, dw_ref, dwb_ref, lng_ref, lnb_ref, o_ref, h_ref, y_ref):
    i = pl.program_id(1)

    def glu(x):
        return x[:, :CONV_CH].astype(F32) * jax.nn.sigmoid(x[:, CONV_CH:].astype(F32))

    h_ref[0, :CONV_HALO, :] = jnp.where(i > 0, glu(halo_ref[...]), 0.0)
    h_ref[0, CONV_HALO:, :] = glu(cur_ref[...])
    n_rows = CONV_TS + CONV_HALO
    for r in range(1, SUBLANES):
        h_ref[r, SUBLANES:, :] = h_ref[0, SUBLANES - r:n_rows - r, :]

    for rb in range(CONV_TS // CONV_RB):
        for c in range(CONV_CH // CONV_CB):
            cols = slice(c * CONV_CB, (c + 1) * CONV_CB)
            acc = jnp.broadcast_to(dwb_ref[:, cols], (CONV_RB, CONV_CB))
            for j in range(CONV_WIDTH):
                groups, r = divmod(CONV_WIDTH - 1 - j, SUBLANES)
                start = CONV_HALO - groups * SUBLANES + rb * CONV_RB
                acc = acc + h_ref[r, start:start + CONV_RB, cols] * dw_ref[j:j + 1, cols]
            y_ref[rb * CONV_RB:(rb + 1) * CONV_RB, cols] = acc

    y = y_ref[...]
    mu = jnp.mean(y, axis=-1, keepdims=True)
    var = jnp.mean(jnp.square(y - mu), axis=-1, keepdims=True)
    z = (y - mu) * lax.rsqrt(var + EPS) * lng_ref[...] + lnb_ref[...]
    o_ref[...] = (z * jax.nn.sigmoid(z)).astype(o_ref.dtype)


def _conv_module(glu_in, dw, dw_b, ln_g, ln_b, batch, seq):
    ns = seq // CONV_TS
    halo_per_tile = CONV_TS // CONV_HALO
    vec = pl.BlockSpec((1, CONV_CH), lambda b, i: (0, 0))
    vmem = (2 * (CONV_TS + CONV_HALO) * 2 * CONV_CH * 2 + 2 * CONV_TS * CONV_CH * 2
            + ((SUBLANES + 1) * CONV_TS + SUBLANES * CONV_HALO) * CONV_CH * 4 + 4 * CONV_TS * CONV_CH * 4)
    return pl.pallas_call(
        _conv_body,
        grid=(batch, ns),
        in_specs=[pl.BlockSpec((CONV_TS, 2 * CONV_CH), lambda b, i: (b * ns + i, 0)),
                  pl.BlockSpec((CONV_HALO, 2 * CONV_CH),
                               lambda b, i: (jnp.maximum((b * ns + i) * halo_per_tile - 1, 0), 0)),
                  pl.BlockSpec((CONV_WIDTH, CONV_CH), lambda b, i: (0, 0)),
                  vec, vec, vec],
        out_specs=pl.BlockSpec((CONV_TS, CONV_CH), lambda b, i: (b * ns + i, 0)),
        out_shape=jax.ShapeDtypeStruct((batch * seq, CONV_CH), BF16),
        scratch_shapes=[pltpu.VMEM((SUBLANES, CONV_TS + CONV_HALO, CONV_CH), F32),
                        pltpu.VMEM((CONV_TS, CONV_CH), F32)],
        compiler_params=pltpu.CompilerParams(
            dimension_semantics=("arbitrary", "arbitrary"), vmem_limit_bytes=_vmem_limit(vmem)),
        name="conv_module",
    )(glu_in, glu_in, dw, dw_b, ln_g, ln_b)


def _mix_body(attn_ref, conv_ref, gates_ref, h_ref, wo_ref, wpw_ref, wout_ref, g_ref, o_ref):
    a = jnp.dot(attn_ref[...], wo_ref[...], preferred_element_type=F32)
    c = jnp.dot(conv_ref[...], wpw_ref[...], preferred_element_type=F32)
    mixed = (gates_ref[:, :D_MODEL].astype(F32) * a + gates_ref[:, D_MODEL:].astype(F32) * c).astype(BF16)
    y = jnp.dot(mixed, wout_ref[...], preferred_element_type=F32)
    o_ref[...] = h_ref[...] + _rms(y, g_ref[...])


def _mix_out(attn, conv, gates, h, w_o, w_pw2, w_out, post_g):
    t = h.shape[0]
    resident = pl.Buffered(1)

    def rows(n):
        return pl.BlockSpec((OUT_TM, n), lambda i: (i, 0))

    def whole(a):
        return pl.BlockSpec(a.shape, lambda i: (0, 0), pipeline_mode=resident)

    vmem = ((w_o.size + w_pw2.size + w_out.size) * 2
            + 2 * OUT_TM * (2 * ATTN_DIM * 2 + 2 * D_MODEL * 2 + 2 * D_MODEL * 4) + 5 * OUT_TM * D_MODEL * 4)
    return pl.pallas_call(
        _mix_body,
        grid=(t // OUT_TM,),
        in_specs=[rows(ATTN_DIM), rows(CONV_CH), rows(2 * D_MODEL), rows(D_MODEL),
                  whole(w_o), whole(w_pw2), whole(w_out), pl.BlockSpec((1, D_MODEL), lambda i: (0, 0))],
        out_specs=rows(D_MODEL),
        out_shape=jax.ShapeDtypeStruct((t, D_MODEL), F32),
        compiler_params=pltpu.CompilerParams(
            dimension_semantics=("arbitrary",), vmem_limit_bytes=_vmem_limit(vmem)),
        name="mix_out",
    )(attn, conv, gates, h, w_o, w_pw2, w_out, post_g)


def kernel(x, rel_bias, ffn1_pre_g, ffn1_wg, ffn1_wu, ffn1_wd, ffn1_post_g, mix_pre_g, w_in, b_gate, w_o,
           conv_dw, conv_dw_b, conv_ln_g, conv_ln_b, w_pw2, w_out, mix_post_g, ffn2_pre_g, ffn2_wg, ffn2_wu,
           ffn2_wd, ffn2_post_g):
    batch, seq, d = x.shape
    depth = ffn1_wg.shape[0]
    h = x.reshape(batch * seq, d)
    o_k = ATTN_DIM
    o_v = o_k + ATTN_DIM
    o_qi = o_v + ATTN_DIM
    o_ki = o_qi + IDX_HEADS * IDX_DIM
    o_wi = o_ki + IDX_DIM
    o_glu = o_wi + IDX_HEADS
    o_gates = o_glu + 2 * CONV_CH
    for l in range(depth):
        wl = w_in[l]
        w_t = jnp.concatenate([_weight_t(wl, 0, o_k, HEAD_DIM ** -0.5 * LOG2E),
                               _weight_t(wl, o_qi, o_ki - o_qi)], axis=0)
        w_vt = _weight_t(wl, o_v, o_qi - o_v)
        w_wt = wl[:, o_wi:o_glu].T.astype(BF16)
        w_k = wl[:, o_k:o_v].astype(BF16)
        w_ki = jnp.pad(wl[:, o_ki:o_wi], ((0, 0), (0, LANES - IDX_DIM))).astype(BF16)
        w_glu = wl[:, o_glu:o_gates].astype(BF16)
        w_gates = wl[:, o_gates:].astype(BF16)

        h, u = _ffn(h, ffn1_pre_g[l][None], ffn1_wg[l].astype(BF16), ffn1_wu[l].astype(BF16),
                    ffn1_wd[l].astype(BF16), ffn1_post_g[l][None], mix_pre_g[l][None])

        qt, vt, wt = _proj_t(u, w_t, w_vt, w_wt)
        k = _proj(u, w_k, BF16, "proj_k")
        kw = _proj(u, w_ki, F32, "proj_ki")
        glu_in = _proj(u, w_glu, BF16, "proj_glu")
        gates = _proj(u, w_gates, BF16, "proj_gates", bias=b_gate[l][None])

        mask = _indexer(qt, wt, kw, batch, seq)
        attn = _attention(qt, k, vt, mask, rel_bias, batch, seq)
        conv = _conv_module(glu_in, conv_dw[l], conv_dw_b[l][None], conv_ln_g[l][None], conv_ln_b[l][None],
                            batch, seq)
        h = _mix_out(attn, conv, gates, h, w_o[l].astype(BF16), w_pw2[l].astype(BF16),
                     w_out[l].astype(BF16), mix_post_g[l][None])

        h = _ffn(h, ffn2_pre_g[l][None], ffn2_wg[l].astype(BF16), ffn2_wu[l].astype(BF16),
                 ffn2_wd[l].astype(BF16), ffn2_post_g[l][None])
    return h.reshape(batch, seq, d)
```

```python
import functools
import math

import jax
import jax.numpy as jnp
from jax import lax
from jax.experimental import pallas as pl
from jax.experimental.pallas import tpu as pltpu

D_MODEL = 2048
N_HEADS = 8
HEAD_DIM = 128
ATTN_DIM = N_HEADS * HEAD_DIM
IDX_HEADS = 16
IDX_DIM = 64
TOPK_MAX = 256
NUM_BUCKETS = 32
MAX_DISTANCE = 128
CONV_CH = 1024
CONV_WIDTH = 31
D_FF = 5632
FFN_RES_WEIGHT = 0.5
EPS = 1e-6

F32 = jnp.float32
BF16 = jnp.bfloat16
LANES = 128
SUBLANES = 8
V7X_VMEM_BYTES = 64 * 1024 * 1024
MASKED = -1e30
INT_MIN = -(2 ** 31)
I16 = jnp.int16
I16_MIN = -(2 ** 15)
PACKED_ROWS = 16
LOG2E = math.log2(math.e)

FFN_TM, FFN_TF = 512, 512
PROJ_TM, PROJ_TN = 1024, 1024
TILE = 256
PROJT_ROWS = 512
WT_COLS = 512
ATT_GROUP = 4
COUNT_TILES = 4
VT_ROWS = HEAD_DIM + PACKED_ROWS
CONV_TS = 512
CONV_HALO = 32
CONV_RB, CONV_CB = 128, 256
OUT_TM = 256

_NT = (((1,), (1,)), ((), ()))


def _vmem_limit(nbytes):
    return int(min(nbytes + (8 << 20), V7X_VMEM_BYTES - (4 << 20)))


def _rms(x, g):
    y = x * lax.rsqrt(jnp.mean(x * x, axis=-1, keepdims=True) + EPS)
    return y * g


def _ffn_body(emit_next, x_ref, pre_g_ref, wg_ref, wu_ref, wd_ref, post_g_ref, *rest):
    if emit_next:
        next_g_ref, out_ref, u_ref, xn_ref, acc_ref = rest
    else:
        out_ref, xn_ref, acc_ref = rest
    j = pl.program_id(1)

    @pl.when(j == 0)
    def _():
        xn_ref[...] = _rms(x_ref[...], pre_g_ref[...]).astype(BF16)
        acc_ref[...] = jnp.zeros_like(acc_ref)

    xn = xn_ref[...]
    g = jnp.dot(xn, wg_ref[...], preferred_element_type=F32)
    u = jnp.dot(xn, wu_ref[...], preferred_element_type=F32)
    a = (g * jax.nn.sigmoid(g) * u).astype(BF16)
    acc_ref[...] += jnp.dot(a, wd_ref[...], preferred_element_type=F32)

    @pl.when(j == pl.num_programs(1) - 1)
    def _():
        h = x_ref[...] + FFN_RES_WEIGHT * _rms(acc_ref[...], post_g_ref[...])
        out_ref[...] = h
        if emit_next:
            u_ref[...] = _rms(h, next_g_ref[...]).astype(BF16)


def _ffn(x, pre_g, wg, wu, wd, post_g, next_g=None):
    t, d = x.shape
    dff = wg.shape[1]
    emit_next = next_g is not None
    row = pl.BlockSpec((FFN_TM, d), lambda i, j: (i, 0))
    gain = pl.BlockSpec((1, d), lambda i, j: (0, 0))
    in_specs = [row, gain,
                pl.BlockSpec((d, FFN_TF), lambda i, j: (0, j)),
                pl.BlockSpec((d, FFN_TF), lambda i, j: (0, j)),
                pl.BlockSpec((FFN_TF, d), lambda i, j: (j, 0)),
                gain]
    args = [x, pre_g, wg, wu, wd, post_g]
    out_shape = [jax.ShapeDtypeStruct((t, d), F32)]
    out_specs = [row]
    if emit_next:
        in_specs.append(gain)
        args.append(next_g)
        out_shape.append(jax.ShapeDtypeStruct((t, d), BF16))
        out_specs.append(row)
    vmem = (2 * FFN_TM * d * 4 * 2 + 2 * FFN_TM * d * 2 + FFN_TM * d * (2 + 4)
            + 2 * 3 * d * FFN_TF * 2 + 3 * FFN_TM * FFN_TF * 4)
    res = pl.pallas_call(
        functools.partial(_ffn_body, emit_next),
        grid=(t // FFN_TM, dff // FFN_TF),
        in_specs=in_specs, out_specs=out_specs, out_shape=out_shape,
        scratch_shapes=[pltpu.VMEM((FFN_TM, d), BF16), pltpu.VMEM((FFN_TM, d), F32)],
        compiler_params=pltpu.CompilerParams(
            dimension_semantics=("arbitrary", "arbitrary"), vmem_limit_bytes=_vmem_limit(vmem)),
        name="ffn_next" if emit_next else "ffn",
    )(*args)
    return res if emit_next else res[0]


def _proj_body(gate, x_ref, w_ref, *rest):
    if gate:
        b_ref, o_ref = rest
    else:
        (o_ref,) = rest
    y = jnp.dot(x_ref[...], w_ref[...], preferred_element_type=F32)
    if gate:
        y = jax.nn.sigmoid(y + b_ref[...])
    o_ref[...] = y.astype(o_ref.dtype)


def _proj(x, w, out_dtype, name, bias=None):
    t, k = x.shape
    n = w.shape[1]
    tn = min(n, PROJ_TN)
    gate = bias is not None
    in_specs = [pl.BlockSpec((PROJ_TM, k), lambda i, j: (i, 0)),
                pl.BlockSpec((k, tn), lambda i, j: (0, j))]
    args = [x, w]
    if gate:
        in_specs.append(pl.BlockSpec((1, tn), lambda i, j: (0, j)))
        args.append(bias)
    vmem = 2 * PROJ_TM * k * 2 + 2 * k * tn * 2 + 3 * PROJ_TM * tn * 4
    return pl.pallas_call(
        functools.partial(_proj_body, gate),
        grid=(t // PROJ_TM, n // tn),
        in_specs=in_specs,
        out_specs=pl.BlockSpec((PROJ_TM, tn), lambda i, j: (i, j)),
        out_shape=jax.ShapeDtypeStruct((t, n), out_dtype),
        compiler_params=pltpu.CompilerParams(
            dimension_semantics=("arbitrary", "arbitrary"), vmem_limit_bytes=_vmem_limit(vmem)),
        name=name,
    )(*args)


def _weight_t_body(scale, w_ref, o_ref):
    o_ref[...] = (w_ref[...] * scale).T.astype(BF16)


def _weight_t(w, col0, ncols, scale=1.0):
    k = w.shape[0]
    assert col0 % WT_COLS == 0 and ncols % WT_COLS == 0
    vmem = 2 * k * WT_COLS * (4 + 2) + 2 * k * WT_COLS * 4
    return pl.pallas_call(
        functools.partial(_weight_t_body, scale),
        grid=(ncols // WT_COLS,),
        in_specs=[pl.BlockSpec((k, WT_COLS), lambda j: (0, col0 // WT_COLS + j))],
        out_specs=pl.BlockSpec((WT_COLS, k), lambda j: (j, 0)),
        out_shape=jax.ShapeDtypeStruct((ncols, k), BF16),
        compiler_params=pltpu.CompilerParams(
            dimension_semantics=("arbitrary",), vmem_limit_bytes=_vmem_limit(vmem)),
        name="weight_t",
    )(w)


def _proj_t_body(u_ref, wt_ref, wvt_ref, wwt_ref, o_ref, v_ref, w_ref):
    u = u_ref[...]
    for r in range(wt_ref.shape[0] // PROJT_ROWS):
        rows = slice(r * PROJT_ROWS, (r + 1) * PROJT_ROWS)
        o_ref[0, rows, :] = lax.dot_general(wt_ref[rows, :], u, _NT, preferred_element_type=F32).astype(BF16)
    heads_per_dot = PROJT_ROWS // HEAD_DIM
    for r in range(wvt_ref.shape[0] // PROJT_ROWS):
        vt = lax.dot_general(wvt_ref[r * PROJT_ROWS:(r + 1) * PROJT_ROWS, :], u, _NT,
                             preferred_element_type=F32).astype(BF16)
        for hh in range(heads_per_dot):
            h = r * heads_per_dot + hh
            v_ref[0, h * VT_ROWS:h * VT_ROWS + HEAD_DIM, :] = vt[hh * HEAD_DIM:(hh + 1) * HEAD_DIM, :]
            v_ref[0, h * VT_ROWS + HEAD_DIM:(h + 1) * VT_ROWS, :] = jnp.ones((VT_ROWS - HEAD_DIM, TILE), BF16)
    w_ref[0] = lax.dot_general(wwt_ref[...], u, _NT, preferred_element_type=F32)


def _proj_t(u, wt, wvt, wwt):
    t, k = u.shape
    n, nw = wt.shape[0], wwt.shape[0]
    nv = N_HEADS * VT_ROWS
    resident = pl.Buffered(1)
    vmem = (2 * TILE * k * 2 + (n + wvt.shape[0] + nw) * k * 2 + 2 * ((n + nv) * 2 + nw * 4) * TILE
            + 2 * PROJT_ROWS * TILE * 4)
    return pl.pallas_call(
        _proj_t_body,
        grid=(t // TILE,),
        in_specs=[pl.BlockSpec((TILE, k), lambda i: (i, 0)),
                  pl.BlockSpec((n, k), lambda i: (0, 0), pipeline_mode=resident),
                  pl.BlockSpec(wvt.shape, lambda i: (0, 0), pipeline_mode=resident),
                  pl.BlockSpec((nw, k), lambda i: (0, 0), pipeline_mode=resident)],
        out_specs=[pl.BlockSpec((1, n, TILE), lambda i: (i, 0, 0)),
                   pl.BlockSpec((1, nv, TILE), lambda i: (i, 0, 0)),
                   pl.BlockSpec((1, nw, TILE), lambda i: (i, 0, 0))],
        out_shape=[jax.ShapeDtypeStruct((t // TILE, n, TILE), BF16),
                   jax.ShapeDtypeStruct((t // TILE, nv, TILE), BF16),
                   jax.ShapeDtypeStruct((t // TILE, nw, TILE), F32)],
        compiler_params=pltpu.CompilerParams(
            dimension_semantics=("arbitrary",), vmem_limit_bytes=_vmem_limit(vmem)),
        name="proj_t",
    )(u, wt, wvt, wwt)


def _indexer_body(seq, qi_ref, w_ref, kw_ref, mask_ref, kb_ref, key_ref, hi_ref, lo_ref, mem_ref):
    i = pl.program_id(1)
    nch = i + 1
    n_tiles = seq // TILE

    @pl.when(i == 0)
    def _():
        kb_ref[...] = kw_ref[:, :IDX_DIM].astype(BF16)

    w = w_ref[0] * (IDX_HEADS ** -0.5 * IDX_DIM ** -0.5)
    q_pos = i * TILE + lax.broadcasted_iota(jnp.int32, (1, TILE), 1)

    def rows_of(c):
        return pl.ds(pl.multiple_of(c * TILE, TILE), TILE)

    def score_tile(c):
        ks = kb_ref[rows_of(c), :]
        acc = jnp.zeros((TILE, TILE), F32)
        for h in range(IDX_HEADS):
            d = jnp.dot(ks, qi_ref[0, h * IDX_DIM:(h + 1) * IDX_DIM, :], preferred_element_type=F32)
            acc = acc + jnp.maximum(d, 0.0) * w[h:h + 1, :]
        bits = pltpu.bitcast(acc, jnp.int32)
        key = bits ^ ((bits >> 31) & jnp.int32(0x7FFFFFFF))
        k_pos = c * TILE + lax.broadcasted_iota(jnp.int32, (TILE, 1), 0)
        key = jnp.where(k_pos <= q_pos, key, jnp.int32(INT_MIN))
        key_ref[rows_of(c), :] = key
        hi_ref[rows_of(c), :] = (key >> 16).astype(I16)
        lo_ref[rows_of(c), :] = ((key & jnp.int32(0xFFFF)) + I16_MIN).astype(I16)

    def score_pair(t, carry):
        score_tile(2 * t)
        score_tile(2 * t + 1)
        return carry

    lax.fori_loop(0, nch // 2, score_pair, 0)

    @pl.when(nch % 2 == 1)
    def _():
        score_tile(nch - 1)

    n_steps = (nch + COUNT_TILES - 1) // COUNT_TILES
    step_rows = COUNT_TILES * TILE

    def rows_of_step(c):
        return pl.ds(pl.multiple_of(c * step_rows, step_rows), step_rows)

    def pad_chunk(c, carry):
        key_ref[rows_of(c), :] = jnp.full((TILE, TILE), INT_MIN, jnp.int32)
        hi_ref[rows_of(c), :] = jnp.full((TILE, TILE), I16_MIN, I16)
        lo_ref[rows_of(c), :] = jnp.full((TILE, TILE), I16_MIN, I16)
        return carry

    lax.fori_loop(nch, n_steps * COUNT_TILES, pad_chunk, 0)

    def packed(row):
        return jnp.broadcast_to(row, (PACKED_ROWS, TILE)).astype(I16)

    tile_groups = TILE // PACKED_ROWS

    def load16(ref, tile):
        return ref[rows_of(tile), :].reshape(tile_groups, PACKED_ROWS, TILE)

    def count16(c, hit_of_tile):
        parts = [None] * 4
        for t in range(COUNT_TILES):
            ones = jnp.where(hit_of_tile(c * COUNT_TILES + t), jnp.int16(1), jnp.int16(0))
            for g in range(tile_groups):
                parts[g % 4] = ones[g] if parts[g % 4] is None else parts[g % 4] + ones[g]
        return (parts[0] + parts[1]) + (parts[2] + parts[3])

    def total16(cnt):
        return jnp.sum(cnt.astype(jnp.int32), axis=0, keepdims=True)

    def search16(src_ref, need):
        def search_pass(b, carry):
            u, n_u = carry
            cand_u = u | lax.shift_left(jnp.int32(1), 15 - b)
            cand = packed(cand_u + I16_MIN)
            cnt = lax.fori_loop(
                0, n_steps, lambda c, cnt: cnt + count16(c, lambda tile: load16(src_ref, tile) >= cand[None]),
                jnp.zeros((PACKED_ROWS, TILE), I16))
            total = total16(cnt)
            ok = total >= need
            return jnp.where(ok, cand_u, u), jnp.where(ok, total, n_u)

        zero = jnp.zeros((1, TILE), jnp.int32)
        return lax.fori_loop(0, 16, search_pass, (zero, zero))

    u_hi, _ = search16(hi_ref, jnp.int32(TOPK_MAX))
    t_hi = u_hi + I16_MIN
    t_hi16 = packed(t_hi)

    def member_chunk(c, carry):
        n_gt, n_mem = carry

        def members(tile):
            member = load16(hi_ref, tile) == t_hi16[None]
            mem_ref[rows_of(tile), :] = jnp.where(member, load16(lo_ref, tile), jnp.int16(I16_MIN)).reshape(TILE, TILE)
            return member

        n_mem = n_mem + count16(c, members)
        return n_gt + count16(c, lambda tile: load16(hi_ref, tile) > t_hi16[None]), n_mem

    zero16 = jnp.zeros((PACKED_ROWS, TILE), I16)
    n_gt, n_mem = lax.fori_loop(0, n_steps, member_chunk, (zero16, zero16))
    n_gt, n_mem = total16(n_gt), total16(n_mem)
    u_lo, n_lo = search16(mem_ref, TOPK_MAX - n_gt)
    thr = jnp.maximum(t_hi * 65536 + u_lo, jnp.int32(INT_MIN + 1))
    n_ge = jnp.where(u_hi > 0, n_gt + jnp.where(u_lo > 0, n_lo, n_mem), 0)

    def count32(pred):
        def count_chunk(c, cnt):
            hit = jnp.where(pred(key_ref[rows_of_step(c), :]), 1, 0)
            return cnt + jnp.sum(hit.reshape(step_rows // SUBLANES, SUBLANES, TILE), axis=0)

        cnt = lax.fori_loop(0, n_steps, count_chunk, jnp.zeros((SUBLANES, TILE), jnp.int32))
        return jnp.sum(cnt, axis=0, keepdims=True)

    has_ties = jnp.max(n_ge) > TOPK_MAX

    @pl.when(jnp.logical_not(has_ties))
    def _():
        def write_chunk(c, carry):
            mask_ref[0, 0, rows_of(c), :] = jnp.where(key_ref[rows_of(c), :] >= thr, 0.0, MASKED).astype(BF16)
            return carry

        lax.fori_loop(0, nch, write_chunk, 0)

    @pl.when(has_ties)
    def _():
        keep = (TOPK_MAX - count32(lambda key: key > thr)).astype(F32)
        below = jnp.where(lax.broadcasted_iota(jnp.int32, (TILE, TILE), 0)
                          >= lax.broadcasted_iota(jnp.int32, (TILE, TILE), 1), 1.0, 0.0).astype(BF16)

        def write_chunk(c, seen):
            key = key_ref[rows_of(c), :]
            tied = key == thr
            rank = jnp.dot(below, jnp.where(tied, 1.0, 0.0).astype(BF16), preferred_element_type=F32) + seen
            m = jnp.where(key > thr, 0.0, jnp.where(tied, jnp.where(rank <= keep, 0.0, MASKED), MASKED))
            mask_ref[0, 0, rows_of(c), :] = m.astype(BF16)
            return rank[TILE - 1:TILE, :]

        lax.fori_loop(0, nch, write_chunk, jnp.zeros((1, TILE), F32))

    def fill_chunk(c, carry):
        mask_ref[0, 0, rows_of(c), :] = jnp.full((TILE, TILE), MASKED, BF16)
        return carry

    lax.fori_loop(nch, n_tiles, fill_chunk, 0)


def _indexer(qt, wt, kw, batch, seq):
    nq = seq // TILE
    qi_block = 1
    vmem = (2 * IDX_HEADS * IDX_DIM * TILE * 2 + 2 * seq * LANES * 4 + 2 * seq * TILE * 2
            + seq * LANES * 2 + seq * TILE * (4 + 2 + 2 + 2) + 8 * TILE * TILE * 4)
    return pl.pallas_call(
        functools.partial(_indexer_body, seq),
        grid=(batch, nq),
        in_specs=[pl.BlockSpec((1, IDX_HEADS * IDX_DIM, TILE), lambda b, i: (b * nq + i, qi_block, 0)),
                  pl.BlockSpec((1, IDX_HEADS, TILE), lambda b, i: (b * nq + i, 0, 0)),
                  pl.BlockSpec((seq, LANES), lambda b, i: (b, 0))],
        out_specs=pl.BlockSpec((1, 1, seq, TILE), lambda b, i: (b, i, 0, 0)),
        out_shape=jax.ShapeDtypeStruct((batch, nq, seq, TILE), BF16),
        scratch_shapes=[pltpu.VMEM((seq, IDX_DIM), BF16),
                        pltpu.VMEM((seq, TILE), jnp.int32),
                        pltpu.VMEM((seq, TILE), I16),
                        pltpu.VMEM((seq, TILE), I16),
                        pltpu.VMEM((seq, TILE), I16)],
        compiler_params=pltpu.CompilerParams(
            dimension_semantics=("arbitrary", "arbitrary"), vmem_limit_bytes=_vmem_limit(vmem)),
        name="indexer",
    )(qt, wt, kw)


def _bias_init(rb_ref, bias_ref):
    shape = (2 * TILE, TILE)
    dist = (lax.broadcasted_iota(jnp.int32, shape, 1) - lax.broadcasted_iota(jnp.int32, shape, 0) + TILE)
    max_exact = NUM_BUCKETS // 2
    n = jnp.maximum(dist, 0)
    nf = jnp.maximum(n, max_exact).astype(F32)
    large = max_exact + (jnp.log(nf / max_exact) / math.log(MAX_DISTANCE / max_exact)
                         * (NUM_BUCKETS - max_exact)).astype(jnp.int32)
    large = jnp.minimum(large, NUM_BUCKETS - 1)
    bucket = jnp.where(n < max_exact, n, large)
    for h in range(N_HEADS):
        far = rb_ref[NUM_BUCKETS - 1, h]
        acc = jnp.zeros(shape, F32)
        for b in range(NUM_BUCKETS - 1):
            acc = jnp.where(bucket == b, (rb_ref[b, h] - far) * LOG2E, acc)
        bias_ref[h] = acc


def _attn_body(rb_ref, q_ref, k_ref, vt_ref, mask_ref, o_ref, bias_ref, acc_ref, m_ref, l_ref, alpha_ref,
               p_ref):
    b = pl.program_id(0)
    i = pl.program_id(1)

    @pl.when((b == 0) & (i == 0))
    def _():
        _bias_init(rb_ref, bias_ref)

    acc_ref[...] = jnp.zeros_like(acc_ref)
    m_ref[...] = jnp.full_like(m_ref, MASKED)
    l_ref[...] = jnp.zeros_like(l_ref)

    eye = jnp.where(lax.broadcasted_iota(jnp.int32, (HEAD_DIM, HEAD_DIM), 0)
                    == lax.broadcasted_iota(jnp.int32, (HEAD_DIM, HEAD_DIM), 1), 1.0, 0.0).astype(BF16)

    def numerators(j, near, slot):
        halves = [pl.ds(pl.multiple_of(j * TILE + r * HEAD_DIM, HEAD_DIM), HEAD_DIM)
                  for r in range(TILE // HEAD_DIM)]
        masks = [mask_ref[0, 0, rows, :] for rows in halves]
        for h in range(N_HEADS):
            feat = slice(h * HEAD_DIM, (h + 1) * HEAD_DIM)
            s_parts = []
            for r, rows in enumerate(halves):
                lhs = jnp.concatenate([k_ref[rows, feat], eye], axis=1)
                rhs = jnp.concatenate([q_ref[0, feat, :], masks[r]], axis=0)
                s = jnp.dot(lhs, rhs, preferred_element_type=F32)
                if near is not None:
                    s = s + bias_ref[h, near * TILE + r * HEAD_DIM:near * TILE + (r + 1) * HEAD_DIM, :]
                s_parts.append(s)
            for c in range(TILE // LANES):
                lanes = slice(c * LANES, (c + 1) * LANES)
                sc = jnp.concatenate([s[:, lanes] for s in s_parts], axis=0)
                m_old = m_ref[h:h + 1, lanes]
                m_new = jnp.maximum(m_old, jnp.max(sc, axis=0, keepdims=True))
                p_ref[slot, h, :, lanes] = jnp.exp2(sc - m_new).astype(BF16)
                alpha_ref[slot, h:h + 1, lanes] = jnp.exp2(m_old - m_new)
                m_ref[h:h + 1, lanes] = m_new

    def values(j, slot):
        for h in range(N_HEADS):
            feat = slice(h * HEAD_DIM, (h + 1) * HEAD_DIM)
            pv = jnp.dot(vt_ref[j, h * VT_ROWS:(h + 1) * VT_ROWS, :], p_ref[slot, h], preferred_element_type=F32)
            alpha = alpha_ref[slot, h:h + 1, :]
            acc_ref[feat, :] = alpha * acc_ref[feat, :] + pv[:HEAD_DIM, :]
            l_ref[h:h + 1, :] = alpha * l_ref[h:h + 1, :] + pv[HEAD_DIM:HEAD_DIM + 1, :]

    def group(tiles):
        for slot, (j, near) in enumerate(tiles):
            numerators(j, near, slot)
        for slot, (j, near) in enumerate(tiles):
            values(j, slot)

    n_far = jnp.maximum(i - 1, 0)

    def far_group(t, carry):
        group([(ATT_GROUP * t + r, None) for r in range(ATT_GROUP)])
        return carry

    lax.fori_loop(0, n_far // ATT_GROUP, far_group, 0)
    rest = n_far % ATT_GROUP
    done = n_far - rest

    @pl.when(rest >= 2)
    def _():
        group([(done, None), (done + 1, None)])

    @pl.when(rest % 2 == 1)
    def _():
        group([(n_far - 1, None)])

    @pl.when(i >= 1)
    def _():
        group([(i - 1, 0), (i, 1)])

    @pl.when(i == 0)
    def _():
        group([(i, 1)])

    for h in range(N_HEADS):
        feat = slice(h * HEAD_DIM, (h + 1) * HEAD_DIM)
        o_ref[:, feat] = (acc_ref[feat, :] / l_ref[h:h + 1, :]).T.astype(o_ref.dtype)


def _attention(qt, k, vt, mask, rel_bias, batch, seq):
    nq = seq // TILE
    vmem = (2 * ATTN_DIM * TILE * 2 * 2 + seq * (ATTN_DIM + N_HEADS * VT_ROWS) * 2 + 2 * seq * TILE * 2
            + N_HEADS * 2 * TILE * TILE * 4 + ATTN_DIM * TILE * 4 + ATT_GROUP * N_HEADS * TILE * TILE * 2
            + 4 * TILE * TILE * 4)
    resident = pl.Buffered(1)
    return pl.pallas_call(
        _attn_body,
        grid=(batch, nq),
        in_specs=[pl.BlockSpec(memory_space=pltpu.SMEM),
                  pl.BlockSpec((1, ATTN_DIM, TILE), lambda b, i: (b * nq + i, 0, 0)),
                  pl.BlockSpec((seq, ATTN_DIM), lambda b, i: (b, 0), pipeline_mode=resident),
                  pl.BlockSpec((nq, N_HEADS * VT_ROWS, TILE), lambda b, i: (b, 0, 0), pipeline_mode=resident),
                  pl.BlockSpec((1, 1, seq, TILE), lambda b, i: (b, i, 0, 0))],
        out_specs=pl.BlockSpec((TILE, ATTN_DIM), lambda b, i: (b * nq + i, 0)),
        out_shape=jax.ShapeDtypeStruct((batch * seq, ATTN_DIM), BF16),
        scratch_shapes=[pltpu.VMEM((N_HEADS, 2 * TILE, TILE), F32),
                        pltpu.VMEM((ATTN_DIM, TILE), F32),
                        pltpu.VMEM((N_HEADS, TILE), F32),
                        pltpu.VMEM((N_HEADS, TILE), F32),
                        pltpu.VMEM((ATT_GROUP, N_HEADS, TILE), F32),
                        pltpu.VMEM((ATT_GROUP, N_HEADS, TILE, TILE), BF16)],
        compiler_params=pltpu.CompilerParams(
            dimension_semantics=("arbitrary", "arbitrary"), vmem_limit_bytes=_vmem_limit(vmem)),
        name="attention",
    )(rel_bias, qt, k, vt, mask)


def _conv_body(cur_ref, halo_ref, dw_ref, dwb_ref, lng_ref, lnb_ref, o_ref, h_ref, y_ref):
    i = pl.program_id(1)

    def glu(x):
        return x[:, :CONV_CH].astype(F32) * jax.nn.sigmoid(x[:, CONV_CH:].astype(F32))

    h_ref[0, :CONV_HALO, :] = jnp.where(i > 0, glu(halo_ref[...]), 0.0)
    h_ref[0, CONV_HALO:, :] = glu(cur_ref[...])
    n_rows = CONV_TS + CONV_HALO
    for r in range(1, SUBLANES):
        h_ref[r, SUBLANES:, :] = h_ref[0, SUBLANES - r:n_rows - r, :]

    for rb in range(CONV_TS // CONV_RB):
        for c in range(CONV_CH // CONV_CB):
            cols = slice(c * CONV_CB, (c + 1) * CONV_CB)
            acc = jnp.broadcast_to(dwb_ref[:, cols], (CONV_RB, CONV_CB))
            for j in range(CONV_WIDTH):
                groups, r = divmod(CONV_WIDTH - 1 - j, SUBLANES)
                start = CONV_HALO - groups * SUBLANES + rb * CONV_RB
                acc = acc + h_ref[r, start:start + CONV_RB, cols] * dw_ref[j:j + 1, cols]
            y_ref[rb * CONV_RB:(rb + 1) * CONV_RB, cols] = acc

    y = y_ref[...]
    mu = jnp.mean(y, axis=-1, keepdims=True)
    var = jnp.mean(jnp.square(y - mu), axis=-1, keepdims=True)
    z = (y - mu) * lax.rsqrt(var + EPS) * lng_ref[...] + lnb_ref[...]
    o_ref[...] = (z * jax.nn.sigmoid(z)).astype(o_ref.dtype)


def _conv_module(glu_in, dw, dw_b, ln_g, ln_b, batch, seq):
    ns = seq // CONV_TS
    halo_per_tile = CONV_TS // CONV_HALO
    vec = pl.BlockSpec((1, CONV_CH), lambda b, i: (0, 0))
    vmem = (2 * (CONV_TS + CONV_HALO) * 2 * CONV_CH * 2 + 2 * CONV_TS * CONV_CH * 2
            + ((SUBLANES + 1) * CONV_TS + SUBLANES * CONV_HALO) * CONV_CH * 4 + 4 * CONV_TS * CONV_CH * 4)
    return pl.pallas_call(
        _conv_body,
        grid=(batch, ns),
        in_specs=[pl.BlockSpec((CONV_TS, 2 * CONV_CH), lambda b, i: (b * ns + i, 0)),
                  pl.BlockSpec((CONV_HALO, 2 * CONV_CH),
                               lambda b, i: (jnp.maximum((b * ns + i) * halo_per_tile - 1, 0), 0)),
                  pl.BlockSpec((CONV_WIDTH, CONV_CH), lambda b, i: (0, 0)),
                  vec, vec, vec],
        out_specs=pl.BlockSpec((CONV_TS, CONV_CH), lambda b, i: (b * ns + i, 0)),
        out_shape=jax.ShapeDtypeStruct((batch * seq, CONV_CH), BF16),
        scratch_shapes=[pltpu.VMEM((SUBLANES, CONV_TS + CONV_HALO, CONV_CH), F32),
                        pltpu.VMEM((CONV_TS, CONV_CH), F32)],
        compiler_params=pltpu.CompilerParams(
            dimension_semantics=("arbitrary", "arbitrary"), vmem_limit_bytes=_vmem_limit(vmem)),
        name="conv_module",
    )(glu_in, glu_in, dw, dw_b, ln_g, ln_b)


def _mix_body(attn_ref, conv_ref, gates_ref, h_ref, wo_ref, wpw_ref, wout_ref, g_ref, o_ref):
    a = jnp.dot(attn_ref[...], wo_ref[...], preferred_element_type=F32)
    c = jnp.dot(conv_ref[...], wpw_ref[...], preferred_element_type=F32)
    mixed = (gates_ref[:, :D_MODEL].astype(F32) * a + gates_ref[:, D_MODEL:].astype(F32) * c).astype(BF16)
    y = jnp.dot(mixed, wout_ref[...], preferred_element_type=F32)
    o_ref[...] = h_ref[...] + _rms(y, g_ref[...])


def _mix_out(attn, conv, gates, h, w_o, w_pw2, w_out, post_g):
    t = h.shape[0]
    resident = pl.Buffered(1)

    def rows(n):
        return pl.BlockSpec((OUT_TM, n), lambda i: (i, 0))

    def whole(a):
        return pl.BlockSpec(a.shape, lambda i: (0, 0), pipeline_mode=resident)

    vmem = ((w_o.size + w_pw2.size + w_out.size) * 2
            + 2 * OUT_TM * (2 * ATTN_DIM * 2 + 2 * D_MODEL * 2 + 2 * D_MODEL * 4) + 5 * OUT_TM * D_MODEL * 4)
    return pl.pallas_call(
        _mix_body,
        grid=(t // OUT_TM,),
        in_specs=[rows(ATTN_DIM), rows(CONV_CH), rows(2 * D_MODEL), rows(D_MODEL),
                  whole(w_o), whole(w_pw2), whole(w_out), pl.BlockSpec((1, D_MODEL), lambda i: (0, 0))],
        out_specs=rows(D_MODEL),
        out_shape=jax.ShapeDtypeStruct((t, D_MODEL), F32),
        compiler_params=pltpu.CompilerParams(
            dimension_semantics=("arbitrary",), vmem_limit_bytes=_vmem_limit(vmem)),
        name="mix_out",
    )(attn, conv, gates, h, w_o, w_pw2, w_out, post_g)


def kernel(x, rel_bias, ffn1_pre_g, ffn1_wg, ffn1_wu, ffn1_wd, ffn1_post_g, mix_pre_g, w_in, b_gate, w_o,
           conv_dw, conv_dw_b, conv_ln_g, conv_ln_b, w_pw2, w_out, mix_post_g, ffn2_pre_g, ffn2_wg, ffn2_wu,
           ffn2_wd, ffn2_post_g):
    batch, seq, d = x.shape
    depth = ffn1_wg.shape[0]
    h = x.reshape(batch * seq, d)
    o_k = ATTN_DIM
    o_v = o_k + ATTN_DIM
    o_qi = o_v + ATTN_DIM
    o_ki = o_qi + IDX_HEADS * IDX_DIM
    o_wi = o_ki + IDX_DIM
    o_glu = o_wi + IDX_HEADS
    o_gates = o_glu + 2 * CONV_CH
    for l in range(depth):
        wl = w_in[l]
        w_t = jnp.concatenate([_weight_t(wl, 0, o_k, HEAD_DIM ** -0.5 * LOG2E),
                               _weight_t(wl, o_qi, o_ki - o_qi)], axis=0)
        w_vt = _weight_t(wl, o_v, o_qi - o_v)
        w_wt = wl[:, o_wi:o_glu].T.astype(BF16)
        w_k = wl[:, o_k:o_v].astype(BF16)
        w_ki = jnp.pad(wl[:, o_ki:o_wi], ((0, 0), (0, LANES - IDX_DIM))).astype(BF16)
        w_glu = wl[:, o_glu:o_gates].astype(BF16)
        w_gates = wl[:, o_gates:].astype(BF16)

        h, u = _ffn(h, ffn1_pre_g[l][None], ffn1_wg[l].astype(BF16), ffn1_wu[l].astype(BF16),
                    ffn1_wd[l].astype(BF16), ffn1_post_g[l][None], mix_pre_g[l][None])

        qt, vt, wt = _proj_t(u, w_t, w_vt, w_wt)
        k = _proj(u, w_k, BF16, "proj_k")
        kw = _proj(u, w_ki, F32, "proj_ki")
        glu_in = _proj(u, w_glu, BF16, "proj_glu")
        gates = _proj(u, w_gates, BF16, "proj_gates", bias=b_gate[l][None])

        mask = _indexer(qt, wt, kw, batch, seq)
        attn = _attention(qt, k, vt, mask, rel_bias, batch, seq)
        conv = _conv_module(glu_in, conv_dw[l], conv_dw_b[l][None], conv_ln_g[l][None], conv_ln_b[l][None],
                            batch, seq)
        h = _mix_out(attn, conv, gates, h, w_o[l].astype(BF16), w_pw2[l].astype(BF16),
                     w_out[l].astype(BF16), mix_post_g[l][None])

        h = _ffn(h, ffn2_pre_g[l][None], ffn2_wg[l].astype(BF16), ffn2_wu[l].astype(BF16),
                 ffn2_wd[l].astype(BF16), ffn2_post_g[l][None])
    return h.reshape(batch, seq, d)
```

```python
import functools
import math

import jax
import jax.numpy as jnp
from jax import lax
from jax.experimental import pallas as pl
from jax.experimental.pallas import tpu as pltpu

D_MODEL = 2048
N_HEADS = 8
HEAD_DIM = 128
ATTN_DIM = N_HEADS * HEAD_DIM
IDX_HEADS = 16
IDX_DIM = 64
TOPK_MAX = 256
NUM_BUCKETS = 32
MAX_DISTANCE = 128
CONV_CH = 1024
CONV_WIDTH = 31
D_FF = 5632
FFN_RES_WEIGHT = 0.5
EPS = 1e-6

F32 = jnp.float32
BF16 = jnp.bfloat16
LANES = 128
SUBLANES = 8
V7X_VMEM_BYTES = 64 * 1024 * 1024
MASKED = -1e30
INT_MIN = -(2 ** 31)
I16 = jnp.int16
I16_MIN = -(2 ** 15)
PACKED_ROWS = 16
LOG2E = math.log2(math.e)

FFN_TM, FFN_TF = 512, 512
PROJ_TM, PROJ_TN = 1024, 1024
TILE = 256
PROJT_ROWS = 512
WT_COLS = 512
ATT_GROUP = 4
COUNT_TILES = 4
VT_ROWS = HEAD_DIM + PACKED_ROWS
CONV_TS = 512
CONV_HALO = 32
CONV_RB, CONV_CB = 128, 256
OUT_TM = 256

_NT = (((1,), (1,)), ((), ()))


def _vmem_limit(nbytes):
    return int(min(nbytes + (8 << 20), V7X_VMEM_BYTES - (4 << 20)))


def _rms(x, g):
    y = x * lax.rsqrt(jnp.mean(x * x, axis=-1, keepdims=True) + EPS)
    return y * g


def _ffn_body(emit_next, x_ref, pre_g_ref, wg_ref, wu_ref, wd_ref, post_g_ref, *rest):
    if emit_next:
        next_g_ref, out_ref, u_ref, xn_ref, acc_ref, a_ref = rest
    else:
        out_ref, xn_ref, acc_ref, a_ref = rest
    j = pl.program_id(1)
    last = pl.num_programs(1) - 1

    def gate_up(slot):
        xn = xn_ref[...]
        g = jnp.dot(xn, wg_ref[...], preferred_element_type=F32)
        u = jnp.dot(xn, wu_ref[...], preferred_element_type=F32)
        a_ref[slot] = (g * jax.nn.sigmoid(g) * u).astype(BF16)

    def down(slot):
        acc_ref[...] += jnp.dot(a_ref[slot], wd_ref[...], preferred_element_type=F32)

    @pl.when(j == 0)
    def _():
        xn_ref[...] = _rms(x_ref[...], pre_g_ref[...]).astype(BF16)
        acc_ref[...] = jnp.zeros_like(acc_ref)
        gate_up(0)

    @pl.when((j > 0) & (j < last))
    def _():
        gate_up(j % 2)
        down((j + 1) % 2)

    @pl.when(j == last)
    def _():
        down((j + 1) % 2)
        h = x_ref[...] + FFN_RES_WEIGHT * _rms(acc_ref[...], post_g_ref[...])
        out_ref[...] = h
        if emit_next:
            u_ref[...] = _rms(h, next_g_ref[...]).astype(BF16)


def _ffn(x, pre_g, wg, wu, wd, post_g, next_g=None):
    t, d = x.shape
    n_chunks = wg.shape[1] // FFN_TF
    emit_next = next_g is not None
    row = pl.BlockSpec((FFN_TM, d), lambda i, j: (i, 0))
    gain = pl.BlockSpec((1, d), lambda i, j: (0, 0))
    in_specs = [row, gain,
                pl.BlockSpec((d, FFN_TF), lambda i, j: (0, jnp.minimum(j, n_chunks - 1))),
                pl.BlockSpec((d, FFN_TF), lambda i, j: (0, jnp.minimum(j, n_chunks - 1))),
                pl.BlockSpec((FFN_TF, d), lambda i, j: (jnp.maximum(j - 1, 0), 0)),
                gain]
    args = [x, pre_g, wg, wu, wd, post_g]
    out_shape = [jax.ShapeDtypeStruct((t, d), F32)]
    out_specs = [row]
    if emit_next:
        in_specs.append(gain)
        args.append(next_g)
        out_shape.append(jax.ShapeDtypeStruct((t, d), BF16))
        out_specs.append(row)
    vmem = (2 * FFN_TM * d * 4 * 2 + 2 * FFN_TM * d * 2 + FFN_TM * d * (2 + 4)
            + 2 * 3 * d * FFN_TF * 2 + 3 * FFN_TM * FFN_TF * 4 + 2 * FFN_TM * FFN_TF * 2)
    res = pl.pallas_call(
        functools.partial(_ffn_body, emit_next),
        grid=(t // FFN_TM, n_chunks + 1),
        in_specs=in_specs, out_specs=out_specs, out_shape=out_shape,
        scratch_shapes=[pltpu.VMEM((FFN_TM, d), BF16), pltpu.VMEM((FFN_TM, d), F32),
                        pltpu.VMEM((2, FFN_TM, FFN_TF), BF16)],
        compiler_params=pltpu.CompilerParams(
            dimension_semantics=("arbitrary", "arbitrary"), vmem_limit_bytes=_vmem_limit(vmem)),
        name="ffn_next" if emit_next else "ffn",
    )(*args)
    return res if emit_next else res[0]


def _proj_body(gate, x_ref, w_ref, *rest):
    if gate:
        b_ref, o_ref = rest
    else:
        (o_ref,) = rest
    y = jnp.dot(x_ref[...], w_ref[...], preferred_element_type=F32)
    if gate:
        y = jax.nn.sigmoid(y + b_ref[...])
    o_ref[...] = y.astype(o_ref.dtype)


def _proj(x, w, out_dtype, name, bias=None):
    t, k = x.shape
    n = w.shape[1]
    tn = min(n, PROJ_TN)
    gate = bias is not None
    in_specs = [pl.BlockSpec((PROJ_TM, k), lambda i, j: (i, 0)),
                pl.BlockSpec((k, tn), lambda i, j: (0, j))]
    args = [x, w]
    if gate:
        in_specs.append(pl.BlockSpec((1, tn), lambda i, j: (0, j)))
        args.append(bias)
    vmem = 2 * PROJ_TM * k * 2 + 2 * k * tn * 2 + 3 * PROJ_TM * tn * 4
    return pl.pallas_call(
        functools.partial(_proj_body, gate),
        grid=(t // PROJ_TM, n // tn),
        in_specs=in_specs,
        out_specs=pl.BlockSpec((PROJ_TM, tn), lambda i, j: (i, j)),
        out_shape=jax.ShapeDtypeStruct((t, n), out_dtype),
        compiler_params=pltpu.CompilerParams(
            dimension_semantics=("arbitrary", "arbitrary"), vmem_limit_bytes=_vmem_limit(vmem)),
        name=name,
    )(*args)


def _weight_t_body(scale, w_ref, o_ref):
    o_ref[...] = (w_ref[...] * scale).T.astype(BF16)


def _weight_t(w, col0, ncols, scale=1.0):
    k = w.shape[0]
    assert col0 % WT_COLS == 0 and ncols % WT_COLS == 0
    vmem = 2 * k * WT_COLS * (4 + 2) + 2 * k * WT_COLS * 4
    return pl.pallas_call(
        functools.partial(_weight_t_body, scale),
        grid=(ncols // WT_COLS,),
        in_specs=[pl.BlockSpec((k, WT_COLS), lambda j: (0, col0 // WT_COLS + j))],
        out_specs=pl.BlockSpec((WT_COLS, k), lambda j: (j, 0)),
        out_shape=jax.ShapeDtypeStruct((ncols, k), BF16),
        compiler_params=pltpu.CompilerParams(
            dimension_semantics=("arbitrary",), vmem_limit_bytes=_vmem_limit(vmem)),
        name="weight_t",
    )(w)


def _proj_t_body(u_ref, wt_ref, wvt_ref, wwt_ref, o_ref, v_ref, w_ref):
    u = u_ref[...]
    for r in range(wt_ref.shape[0] // PROJT_ROWS):
        rows = slice(r * PROJT_ROWS, (r + 1) * PROJT_ROWS)
        o_ref[0, rows, :] = lax.dot_general(wt_ref[rows, :], u, _NT, preferred_element_type=F32).astype(BF16)
    heads_per_dot = PROJT_ROWS // HEAD_DIM
    for r in range(wvt_ref.shape[0] // PROJT_ROWS):
        vt = lax.dot_general(wvt_ref[r * PROJT_ROWS:(r + 1) * PROJT_ROWS, :], u, _NT,
                             preferred_element_type=F32).astype(BF16)
        for hh in range(heads_per_dot):
            h = r * heads_per_dot + hh
            v_ref[0, h * VT_ROWS:h * VT_ROWS + HEAD_DIM, :] = vt[hh * HEAD_DIM:(hh + 1) * HEAD_DIM, :]
            v_ref[0, h * VT_ROWS + HEAD_DIM:(h + 1) * VT_ROWS, :] = jnp.ones((VT_ROWS - HEAD_DIM, TILE), BF16)
    w_ref[0] = lax.dot_general(wwt_ref[...], u, _NT, preferred_element_type=F32)


def _proj_t(u, wt, wvt, wwt):
    t, k = u.shape
    n, nw = wt.shape[0], wwt.shape[0]
    nv = N_HEADS * VT_ROWS
    resident = pl.Buffered(1)
    vmem = (2 * TILE * k * 2 + (n + wvt.shape[0] + nw) * k * 2 + 2 * ((n + nv) * 2 + nw * 4) * TILE
            + 2 * PROJT_ROWS * TILE * 4)
    return pl.pallas_call(
        _proj_t_body,
        grid=(t // TILE,),
        in_specs=[pl.BlockSpec((TILE, k), lambda i: (i, 0)),
                  pl.BlockSpec((n, k), lambda i: (0, 0), pipeline_mode=resident),
                  pl.BlockSpec(wvt.shape, lambda i: (0, 0), pipeline_mode=resident),
                  pl.BlockSpec((nw, k), lambda i: (0, 0), pipeline_mode=resident)],
        out_specs=[pl.BlockSpec((1, n, TILE), lambda i: (i, 0, 0)),
                   pl.BlockSpec((1, nv, TILE), lambda i: (i, 0, 0)),
                   pl.BlockSpec((1, nw, TILE), lambda i: (i, 0, 0))],
        out_shape=[jax.ShapeDtypeStruct((t // TILE, n, TILE), BF16),
                   jax.ShapeDtypeStruct((t // TILE, nv, TILE), BF16),
                   jax.ShapeDtypeStruct((t // TILE, nw, TILE), F32)],
        compiler_params=pltpu.CompilerParams(
            dimension_semantics=("arbitrary",), vmem_limit_bytes=_vmem_limit(vmem)),
        name="proj_t",
    )(u, wt, wvt, wwt)


def _indexer_body(seq, qi_ref, w_ref, kw_ref, mask_ref, kb_ref, key_ref, hi_ref, lo_ref, mem_ref):
    i = pl.program_id(1)
    nch = i + 1
    n_tiles = seq // TILE

    @pl.when(i == 0)
    def _():
        kb_ref[...] = kw_ref[:, :IDX_DIM].astype(BF16)

    w = w_ref[0] * (IDX_HEADS ** -0.5 * IDX_DIM ** -0.5)
    q_pos = i * TILE + lax.broadcasted_iota(jnp.int32, (1, TILE), 1)

    def rows_of(c):
        return pl.ds(pl.multiple_of(c * TILE, TILE), TILE)

    def score_tile(c):
        ks = kb_ref[rows_of(c), :]
        acc = jnp.zeros((TILE, TILE), F32)
        for h in range(IDX_HEADS):
            d = jnp.dot(ks, qi_ref[0, h * IDX_DIM:(h + 1) * IDX_DIM, :], preferred_element_type=F32)
            acc = acc + jnp.maximum(d, 0.0) * w[h:h + 1, :]
        bits = pltpu.bitcast(acc, jnp.int32)
        key = bits ^ ((bits >> 31) & jnp.int32(0x7FFFFFFF))
        k_pos = c * TILE + lax.broadcasted_iota(jnp.int32, (TILE, 1), 0)
        key = jnp.where(k_pos <= q_pos, key, jnp.int32(INT_MIN))
        key_ref[rows_of(c), :] = key
        hi_ref[rows_of(c), :] = (key >> 16).astype(I16)
        lo_ref[rows_of(c), :] = ((key & jnp.int32(0xFFFF)) + I16_MIN).astype(I16)

    def score_pair(t, carry):
        score_tile(2 * t)
        score_tile(2 * t + 1)
        return carry

    lax.fori_loop(0, nch // 2, score_pair, 0)

    @pl.when(nch % 2 == 1)
    def _():
        score_tile(nch - 1)

    n_steps = (nch + COUNT_TILES - 1) // COUNT_TILES
    step_rows = COUNT_TILES * TILE

    def rows_of_step(c):
        return pl.ds(pl.multiple_of(c * step_rows, step_rows), step_rows)

    def pad_chunk(c, carry):
        key_ref[rows_of(c), :] = jnp.full((TILE, TILE), INT_MIN, jnp.int32)
        hi_ref[rows_of(c), :] = jnp.full((TILE, TILE), I16_MIN, I16)
        lo_ref[rows_of(c), :] = jnp.full((TILE, TILE), I16_MIN, I16)
        return carry

    lax.fori_loop(nch, n_steps * COUNT_TILES, pad_chunk, 0)

    def packed(row):
        return jnp.broadcast_to(row, (PACKED_ROWS, TILE)).astype(I16)

    tile_groups = TILE // PACKED_ROWS

    def load16(ref, tile):
        return ref[rows_of(tile), :].reshape(tile_groups, PACKED_ROWS, TILE)

    def count16(c, hit_of_tile):
        parts = [None] * 4
        for t in range(COUNT_TILES):
            ones = jnp.where(hit_of_tile(c * COUNT_TILES + t), jnp.int16(1), jnp.int16(0))
            for g in range(tile_groups):
                parts[g % 4] = ones[g] if parts[g % 4] is None else parts[g % 4] + ones[g]
        return (parts[0] + parts[1]) + (parts[2] + parts[3])

    def total16(cnt):
        return jnp.sum(cnt.astype(jnp.int32), axis=0, keepdims=True)

    def search16(src_ref, need):
        def search_pass(b, carry):
            u, n_u = carry
            cand_u = u | lax.shift_left(jnp.int32(1), 15 - b)
            cand = packed(cand_u + I16_MIN)
            cnt = lax.fori_loop(
                0, n_steps, lambda c, cnt: cnt + count16(c, lambda tile: load16(src_ref, tile) >= cand[None]),
                jnp.zeros((PACKED_ROWS, TILE), I16))
            total = total16(cnt)
            ok = total >= need
            return jnp.where(ok, cand_u, u), jnp.where(ok, total, n_u)

        zero = jnp.zeros((1, TILE), jnp.int32)
        return lax.fori_loop(0, 16, search_pass, (zero, zero))

    u_hi, _ = search16(hi_ref, jnp.int32(TOPK_MAX))
    t_hi = u_hi + I16_MIN
    t_hi16 = packed(t_hi)

    def member_chunk(c, carry):
        n_gt, n_mem = carry

        def members(tile):
            member = load16(hi_ref, tile) == t_hi16[None]
            mem_ref[rows_of(tile), :] = jnp.where(member, load16(lo_ref, tile), jnp.int16(I16_MIN)).reshape(TILE, TILE)
            return member

        n_mem = n_mem + count16(c, members)
        return n_gt + count16(c, lambda tile: load16(hi_ref, tile) > t_hi16[None]), n_mem

    zero16 = jnp.zeros((PACKED_ROWS, TILE), I16)
    n_gt, n_mem = lax.fori_loop(0, n_steps, member_chunk, (zero16, zero16))
    n_gt, n_mem = total16(n_gt), total16(n_mem)
    u_lo, n_lo = search16(mem_ref, TOPK_MAX - n_gt)
    thr = jnp.maximum(t_hi * 65536 + u_lo, jnp.int32(INT_MIN + 1))
    n_ge = jnp.where(u_hi > 0, n_gt + jnp.where(u_lo > 0, n_lo, n_mem), 0)

    def count32(pred):
        def count_chunk(c, cnt):
            hit = jnp.where(pred(key_ref[rows_of_step(c), :]), 1, 0)
            return cnt + jnp.sum(hit.reshape(step_rows // SUBLANES, SUBLANES, TILE), axis=0)

        cnt = lax.fori_loop(0, n_steps, count_chunk, jnp.zeros((SUBLANES, TILE), jnp.int32))
        return jnp.sum(cnt, axis=0, keepdims=True)

    has_ties = jnp.max(n_ge) > TOPK_MAX

    @pl.when(jnp.logical_not(has_ties))
    def _():
        def write_chunk(c, carry):
            mask_ref[0, 0, rows_of(c), :] = jnp.where(key_ref[rows_of(c), :] >= thr, 0.0, MASKED).astype(BF16)
            return carry

        lax.fori_loop(0, nch, write_chunk, 0)

    @pl.when(has_ties)
    def _():
        keep = (TOPK_MAX - count32(lambda key: key > thr)).astype(F32)
        below = jnp.where(lax.broadcasted_iota(jnp.int32, (TILE, TILE), 0)
                          >= lax.broadcasted_iota(jnp.int32, (TILE, TILE), 1), 1.0, 0.0).astype(BF16)

        def write_chunk(c, seen):
            key = key_ref[rows_of(c), :]
            tied = key == thr
            rank = jnp.dot(below, jnp.where(tied, 1.0, 0.0).astype(BF16), preferred_element_type=F32) + seen
            m = jnp.where(key > thr, 0.0, jnp.where(tied, jnp.where(rank <= keep, 0.0, MASKED), MASKED))
            mask_ref[0, 0, rows_of(c), :] = m.astype(BF16)
            return rank[TILE - 1:TILE, :]

        lax.fori_loop(0, nch, write_chunk, jnp.zeros((1, TILE), F32))

    def fill_chunk(c, carry):
        mask_ref[0, 0, rows_of(c), :] = jnp.full((TILE, TILE), MASKED, BF16)
        return carry

    lax.fori_loop(nch, n_tiles, fill_chunk, 0)


def _indexer(qt, wt, kw, batch, seq):
    nq = seq // TILE
    qi_block = 1
    vmem = (2 * IDX_HEADS * IDX_DIM * TILE * 2 + 2 * seq * LANES * 4 + 2 * seq * TILE * 2
            + seq * LANES * 2 + seq * TILE * (4 + 2 + 2 + 2) + 8 * TILE * TILE * 4)
    return pl.pallas_call(
        functools.partial(_indexer_body, seq),
        grid=(batch, nq),
        in_specs=[pl.BlockSpec((1, IDX_HEADS * IDX_DIM, TILE), lambda b, i: (b * nq + i, qi_block, 0)),
                  pl.BlockSpec((1, IDX_HEADS, TILE), lambda b, i: (b * nq + i, 0, 0)),
                  pl.BlockSpec((seq, LANES), lambda b, i: (b, 0))],
        out_specs=pl.BlockSpec((1, 1, seq, TILE), lambda b, i: (b, i, 0, 0)),
        out_shape=jax.ShapeDtypeStruct((batch, nq, seq, TILE), BF16),
        scratch_shapes=[pltpu.VMEM((seq, IDX_DIM), BF16),
                        pltpu.VMEM((seq, TILE), jnp.int32),
                        pltpu.VMEM((seq, TILE), I16),
                        pltpu.VMEM((seq, TILE), I16),
                        pltpu.VMEM((seq, TILE), I16)],
        compiler_params=pltpu.CompilerParams(
            dimension_semantics=("arbitrary", "arbitrary"), vmem_limit_bytes=_vmem_limit(vmem)),
        name="indexer",
    )(qt, wt, kw)


def _bias_init(rb_ref, bias_ref):
    shape = (2 * TILE, TILE)
    dist = (lax.broadcasted_iota(jnp.int32, shape, 1) - lax.broadcasted_iota(jnp.int32, shape, 0) + TILE)
    max_exact = NUM_BUCKETS // 2
    n = jnp.maximum(dist, 0)
    nf = jnp.maximum(n, max_exact).astype(F32)
    large = max_exact + (jnp.log(nf / max_exact) / math.log(MAX_DISTANCE / max_exact)
                         * (NUM_BUCKETS - max_exact)).astype(jnp.int32)
    large = jnp.minimum(large, NUM_BUCKETS - 1)
    bucket = jnp.where(n < max_exact, n, large)
    for h in range(N_HEADS):
        far = rb_ref[NUM_BUCKETS - 1, h]
        acc = jnp.zeros(shape, F32)
        for b in range(NUM_BUCKETS - 1):
            acc = jnp.where(bucket == b, (rb_ref[b, h] - far) * LOG2E, acc)
        bias_ref[h] = acc


def _attn_body(rb_ref, q_ref, k_ref, vt_ref, mask_ref, o_ref, bias_ref, acc_ref, m_ref, l_ref, alpha_ref,
               p_ref):
    b = pl.program_id(0)
    i = pl.program_id(1)

    @pl.when((b == 0) & (i == 0))
    def _():
        _bias_init(rb_ref, bias_ref)

    acc_ref[...] = jnp.zeros_like(acc_ref)
    m_ref[...] = jnp.full_like(m_ref, MASKED)
    l_ref[...] = jnp.zeros_like(l_ref)

    eye = jnp.where(lax.broadcasted_iota(jnp.int32, (HEAD_DIM, HEAD_DIM), 0)
                    == lax.broadcasted_iota(jnp.int32, (HEAD_DIM, HEAD_DIM), 1), 1.0, 0.0).astype(BF16)

    def numerators(j, near, slot):
        halves = [pl.ds(pl.multiple_of(j * TILE + r * HEAD_DIM, HEAD_DIM), HEAD_DIM)
                  for r in range(TILE // HEAD_DIM)]
        masks = [mask_ref[0, 0, rows, :] for rows in halves]
        for h in range(N_HEADS):
            feat = slice(h * HEAD_DIM, (h + 1) * HEAD_DIM)
            s_parts = []
            for r, rows in enumerate(halves):
                lhs = jnp.concatenate([k_ref[rows, feat], eye], axis=1)
                rhs = jnp.concatenate([q_ref[0, feat, :], masks[r]], axis=0)
                s = jnp.dot(lhs, rhs, preferred_element_type=F32)
                if near is not None:
                    s = s + bias_ref[h, near * TILE + r * HEAD_DIM:near * TILE + (r + 1) * HEAD_DIM, :]
                s_parts.append(s)
            for c in range(TILE // LANES):
                lanes = slice(c * LANES, (c + 1) * LANES)
                sc = jnp.concatenate([s[:, lanes] for s in s_parts], axis=0)
                m_old = m_ref[h:h + 1, lanes]
                m_new = jnp.maximum(m_old, jnp.max(sc, axis=0, keepdims=True))
                p_ref[slot, h, :, lanes] = jnp.exp2(sc - m_new).astype(BF16)
                alpha_ref[slot, h:h + 1, lanes] = jnp.exp2(m_old - m_new)
                m_ref[h:h + 1, lanes] = m_new

    def values(j, slot):
        for h in range(N_HEADS):
            feat = slice(h * HEAD_DIM, (h + 1) * HEAD_DIM)
            pv = jnp.dot(vt_ref[j, h * VT_ROWS:(h + 1) * VT_ROWS, :], p_ref[slot, h], preferred_element_type=F32)
            alpha = alpha_ref[slot, h:h + 1, :]
            acc_ref[feat, :] = alpha * acc_ref[feat, :] + pv[:HEAD_DIM, :]
            l_ref[h:h + 1, :] = alpha * l_ref[h:h + 1, :] + pv[HEAD_DIM:HEAD_DIM + 1, :]

    def group(tiles):
        for slot, (j, near) in enumerate(tiles):
            numerators(j, near, slot)
        for slot, (j, near) in enumerate(tiles):
            values(j, slot)

    n_far = jnp.maximum(i - 1, 0)

    def far_group(t, carry):
        group([(ATT_GROUP * t + r, None) for r in range(ATT_GROUP)])
        return carry

    lax.fori_loop(0, n_far // ATT_GROUP, far_group, 0)
    rest = n_far % ATT_GROUP
    done = n_far - rest

    @pl.when(rest >= 2)
    def _():
        group([(done, None), (done + 1, None)])

    @pl.when(rest % 2 == 1)
    def _():
        group([(n_far - 1, None)])

    @pl.when(i >= 1)
    def _():
        group([(i - 1, 0), (i, 1)])

    @pl.when(i == 0)
    def _():
        group([(i, 1)])

    for h in range(N_HEADS):
        feat = slice(h * HEAD_DIM, (h + 1) * HEAD_DIM)
        o_ref[:, feat] = (acc_ref[feat, :] / l_ref[h:h + 1, :]).T.astype(o_ref.dtype)


def _attention(qt, k, vt, mask, rel_bias, batch, seq):
    nq = seq // TILE
    vmem = (2 * ATTN_DIM * TILE * 2 * 2 + seq * (ATTN_DIM + N_HEADS * VT_ROWS) * 2 + 2 * seq * TILE * 2
            + N_HEADS * 2 * TILE * TILE * 4 + ATTN_DIM * TILE * 4 + ATT_GROUP * N_HEADS * TILE * TILE * 2
            + 4 * TILE * TILE * 4)
    resident = pl.Buffered(1)
    return pl.pallas_call(
        _attn_body,
        grid=(batch, nq),
        in_specs=[pl.BlockSpec(memory_space=pltpu.SMEM),
                  pl.BlockSpec((1, ATTN_DIM, TILE), lambda b, i: (b * nq + i, 0, 0)),
                  pl.BlockSpec((seq, ATTN_DIM), lambda b, i: (b, 0), pipeline_mode=resident),
                  pl.BlockSpec((nq, N_HEADS * VT_ROWS, TILE), lambda b, i: (b, 0, 0), pipeline_mode=resident),
                  pl.BlockSpec((1, 1, seq, TILE), lambda b, i: (b, i, 0, 0))],
        out_specs=pl.BlockSpec((TILE, ATTN_DIM), lambda b, i: (b * nq + i, 0)),
        out_shape=jax.ShapeDtypeStruct((batch * seq, ATTN_DIM), BF16),
        scratch_shapes=[pltpu.VMEM((N_HEADS, 2 * TILE, TILE), F32),
                        pltpu.VMEM((ATTN_DIM, TILE), F32),
                        pltpu.VMEM((N_HEADS, TILE), F32),
                        pltpu.VMEM((N_HEADS, TILE), F32),
                        pltpu.VMEM((ATT_GROUP, N_HEADS, TILE), F32),
                        pltpu.VMEM((ATT_GROUP, N_HEADS, TILE, TILE), BF16)],
        compiler_params=pltpu.CompilerParams(
            dimension_semantics=("arbitrary", "arbitrary"), vmem_limit_bytes=_vmem_limit(vmem)),
        name="attention",
    )(rel_bias, qt, k, vt, mask)


def _conv_body(cur_ref, halo_ref, dw_ref, dwb_ref, lng_ref, lnb_ref, o_ref, h_ref, y_ref):
    i = pl.program_id(1)

    def glu(x):
        return x[:, :CONV_CH].astype(F32) * jax.nn.sigmoid(x[:, CONV_CH:].astype(F32))

    h_ref[0, :CONV_HALO, :] = jnp.where(i > 0, glu(halo_ref[...]), 0.0)
    h_ref[0, CONV_HALO:, :] = glu(cur_ref[...])
    n_rows = CONV_TS + CONV_HALO
    for r in range(1, SUBLANES):
        h_ref[r, SUBLANES:, :] = h_ref[0, SUBLANES - r:n_rows - r, :]

    for rb in range(CONV_TS // CONV_RB):
        for c in range(CONV_CH // CONV_CB):
            cols = slice(c * CONV_CB, (c + 1) * CONV_CB)
            acc = jnp.broadcast_to(dwb_ref[:, cols], (CONV_RB, CONV_CB))
            for j in range(CONV_WIDTH):
                groups, r = divmod(CONV_WIDTH - 1 - j, SUBLANES)
                start = CONV_HALO - groups * SUBLANES + rb * CONV_RB
                acc = acc + h_ref[r, start:start + CONV_RB, cols] * dw_ref[j:j + 1, cols]
            y_ref[rb * CONV_RB:(rb + 1) * CONV_RB, cols] = acc

    y = y_ref[...]
    mu = jnp.mean(y, axis=-1, keepdims=True)
    var = jnp.mean(jnp.square(y - mu), axis=-1, keepdims=True)
    z = (y - mu) * lax.rsqrt(var + EPS) * lng_ref[...] + lnb_ref[...]
    o_ref[...] = (z * jax.nn.sigmoid(z)).astype(o_ref.dtype)


def _conv_module(glu_in, dw, dw_b, ln_g, ln_b, batch, seq):
    ns = seq // CONV_TS
    halo_per_tile = CONV_TS // CONV_HALO
    vec = pl.BlockSpec((1, CONV_CH), lambda b, i: (0, 0))
    vmem = (2 * (CONV_TS + CONV_HALO) * 2 * CONV_CH * 2 + 2 * CONV_TS * CONV_CH * 2
            + ((SUBLANES + 1) * CONV_TS + SUBLANES * CONV_HALO) * CONV_CH * 4 + 4 * CONV_TS * CONV_CH * 4)
    return pl.pallas_call(
        _conv_body,
        grid=(batch, ns),
        in_specs=[pl.BlockSpec((CONV_TS, 2 * CONV_CH), lambda b, i: (b * ns + i, 0)),
                  pl.BlockSpec((CONV_HALO, 2 * CONV_CH),
                               lambda b, i: (jnp.maximum((b * ns + i) * halo_per_tile - 1, 0), 0)),
                  pl.BlockSpec((CONV_WIDTH, CONV_CH), lambda b, i: (0, 0)),
                  vec, vec, vec],
        out_specs=pl.BlockSpec((CONV_TS, CONV_CH), lambda b, i: (b * ns + i, 0)),
        out_shape=jax.ShapeDtypeStruct((batch * seq, CONV_CH), BF16),
        scratch_shapes=[pltpu.VMEM((SUBLANES, CONV_TS + CONV_HALO, CONV_CH), F32),
                        pltpu.VMEM((CONV_TS, CONV_CH), F32)],
        compiler_params=pltpu.CompilerParams(
            dimension_semantics=("arbitrary", "arbitrary"), vmem_limit_bytes=_vmem_limit(vmem)),
        name="conv_module",
    )(glu_in, glu_in, dw, dw_b, ln_g, ln_b)


def _mix_body(attn_ref, conv_ref, gates_ref, h_ref, wo_ref, wpw_ref, wout_ref, g_ref, o_ref):
    a = jnp.dot(attn_ref[...], wo_ref[...], preferred_element_type=F32)
    c = jnp.dot(conv_ref[...], wpw_ref[...], preferred_element_type=F32)
    mixed = (gates_ref[:, :D_MODEL].astype(F32) * a + gates_ref[:, D_MODEL:].astype(F32) * c).astype(BF16)
    y = jnp.dot(mixed, wout_ref[...], preferred_element_type=F32)
    o_ref[...] = h_ref[...] + _rms(y, g_ref[...])


def _mix_out(attn, conv, gates, h, w_o, w_pw2, w_out, post_g):
    t = h.shape[0]
    resident = pl.Buffered(1)

    def rows(n):
        return pl.BlockSpec((OUT_TM, n), lambda i: (i, 0))

    def whole(a):
        return pl.BlockSpec(a.shape, lambda i: (0, 0), pipeline_mode=resident)

    vmem = ((w_o.size + w_pw2.size + w_out.size) * 2
            + 2 * OUT_TM * (2 * ATTN_DIM * 2 + 2 * D_MODEL * 2 + 2 * D_MODEL * 4) + 5 * OUT_TM * D_MODEL * 4)
    return pl.pallas_call(
        _mix_body,
        grid=(t // OUT_TM,),
        in_specs=[rows(ATTN_DIM), rows(CONV_CH), rows(2 * D_MODEL), rows(D_MODEL),
                  whole(w_o), whole(w_pw2), whole(w_out), pl.BlockSpec((1, D_MODEL), lambda i: (0, 0))],
        out_specs=rows(D_MODEL),
        out_shape=jax.ShapeDtypeStruct((t, D_MODEL), F32),
        compiler_params=pltpu.CompilerParams(
            dimension_semantics=("arbitrary",), vmem_limit_bytes=_vmem_limit(vmem)),
        name="mix_out",
    )(attn, conv, gates, h, w_o, w_pw2, w_out, post_g)


def kernel(x, rel_bias, ffn1_pre_g, ffn1_wg, ffn1_wu, ffn1_wd, ffn1_post_g, mix_pre_g, w_in, b_gate, w_o,
           conv_dw, conv_dw_b, conv_ln_g, conv_ln_b, w_pw2, w_out, mix_post_g, ffn2_pre_g, ffn2_wg, ffn2_wu,
           ffn2_wd, ffn2_post_g):
    batch, seq, d = x.shape
    depth = ffn1_wg.shape[0]
    h = x.reshape(batch * seq, d)
    o_k = ATTN_DIM
    o_v = o_k + ATTN_DIM
    o_qi = o_v + ATTN_DIM
    o_ki = o_qi + IDX_HEADS * IDX_DIM
    o_wi = o_ki + IDX_DIM
    o_glu = o_wi + IDX_HEADS
    o_gates = o_glu + 2 * CONV_CH
    for l in range(depth):
        wl = w_in[l]
        w_t = jnp.concatenate([_weight_t(wl, 0, o_k, HEAD_DIM ** -0.5 * LOG2E),
                               _weight_t(wl, o_qi, o_ki - o_qi)], axis=0)
        w_vt = _weight_t(wl, o_v, o_qi - o_v)
        w_wt = wl[:, o_wi:o_glu].T.astype(BF16)
        w_k = wl[:, o_k:o_v].astype(BF16)
        w_ki = jnp.pad(wl[:, o_ki:o_wi], ((0, 0), (0, LANES - IDX_DIM))).astype(BF16)
        w_glu = wl[:, o_glu:o_gates].astype(BF16)
        w_gates = wl[:, o_gates:].astype(BF16)

        h, u = _ffn(h, ffn1_pre_g[l][None], ffn1_wg[l].astype(BF16), ffn1_wu[l].astype(BF16),
                    ffn1_wd[l].astype(BF16), ffn1_post_g[l][None], mix_pre_g[l][None])

        qt, vt, wt = _proj_t(u, w_t, w_vt, w_wt)
        k = _proj(u, w_k, BF16, "proj_k")
        kw = _proj(u, w_ki, F32, "proj_ki")
        glu_in = _proj(u, w_glu, BF16, "proj_glu")
        gates = _proj(u, w_gates, BF16, "proj_gates", bias=b_gate[l][None])

        mask = _indexer(qt, wt, kw, batch, seq)
        attn = _attention(qt, k, vt, mask, rel_bias, batch, seq)
        conv = _conv_module(glu_in, conv_dw[l], conv_dw_b[l][None], conv_ln_g[l][None], conv_ln_b[l][None],
                            batch, seq)
        h = _mix_out(attn, conv, gates, h, w_o[l].astype(BF16), w_pw2[l].astype(BF16),
                     w_out[l].astype(BF16), mix_post_g[l][None])

        h = _ffn(h, ffn2_pre_g[l][None], ffn2_wg[l].astype(BF16), ffn2_wu[l].astype(BF16),
                 ffn2_wd[l].astype(BF16), ffn2_post_g[l][None])
    return h.reshape(batch, seq, d)
```

```python
import functools
import math

import jax
import jax.numpy as jnp
from jax import lax
from jax.experimental import pallas as pl
from jax.experimental.pallas import tpu as pltpu

D_MODEL = 2048
N_HEADS = 8
HEAD_DIM = 128
ATTN_DIM = N_HEADS * HEAD_DIM
IDX_HEADS = 16
IDX_DIM = 64
TOPK_MAX = 256
NUM_BUCKETS = 32
MAX_DISTANCE = 128
CONV_CH = 1024
CONV_WIDTH = 31
D_FF = 5632
FFN_RES_WEIGHT = 0.5
EPS = 1e-6

F32 = jnp.float32
BF16 = jnp.bfloat16
LANES = 128
SUBLANES = 8
V7X_VMEM_BYTES = 64 * 1024 * 1024
MASKED = -1e30
INT_MIN = -(2 ** 31)
I16 = jnp.int16
I16_MIN = -(2 ** 15)
PACKED_ROWS = 16
LOG2E = math.log2(math.e)

FFN_TM, FFN_TF = 512, 512
PROJ_TM, PROJ_TN = 1024, 1024
TILE = 256
PROJT_ROWS = 512
WT_COLS = 512
ATT_GROUP = 4
COUNT_TILES = 4
MEMBER_TOPS = 6
VT_ROWS = HEAD_DIM + PACKED_ROWS
CONV_TS = 512
CONV_HALO = 32
CONV_RB, CONV_CB = 128, 256
OUT_TM = 256

_NT = (((1,), (1,)), ((), ()))


def _vmem_limit(nbytes):
    return int(min(nbytes + (8 << 20), V7X_VMEM_BYTES - (4 << 20)))


def _rms(x, g):
    y = x * lax.rsqrt(jnp.mean(x * x, axis=-1, keepdims=True) + EPS)
    return y * g


def _ffn_body(emit_next, x_ref, pre_g_ref, wg_ref, wu_ref, wd_ref, post_g_ref, *rest):
    if emit_next:
        next_g_ref, out_ref, u_ref, xn_ref, acc_ref = rest
    else:
        out_ref, xn_ref, acc_ref = rest
    j = pl.program_id(1)

    @pl.when(j == 0)
    def _():
        xn_ref[...] = _rms(x_ref[...], pre_g_ref[...]).astype(BF16)
        acc_ref[...] = jnp.zeros_like(acc_ref)

    xn = xn_ref[...]
    g = jnp.dot(xn, wg_ref[...], preferred_element_type=F32)
    u = jnp.dot(xn, wu_ref[...], preferred_element_type=F32)
    a = (g * jax.nn.sigmoid(g) * u).astype(BF16)
    acc_ref[...] += jnp.dot(a, wd_ref[...], preferred_element_type=F32)

    @pl.when(j == pl.num_programs(1) - 1)
    def _():
        h = x_ref[...] + FFN_RES_WEIGHT * _rms(acc_ref[...], post_g_ref[...])
        out_ref[...] = h
        if emit_next:
            u_ref[...] = _rms(h, next_g_ref[...]).astype(BF16)


def _ffn(x, pre_g, wg, wu, wd, post_g, next_g=None):
    t, d = x.shape
    dff = wg.shape[1]
    emit_next = next_g is not None
    row = pl.BlockSpec((FFN_TM, d), lambda i, j: (i, 0))
    gain = pl.BlockSpec((1, d), lambda i, j: (0, 0))
    in_specs = [row, gain,
                pl.BlockSpec((d, FFN_TF), lambda i, j: (0, j)),
                pl.BlockSpec((d, FFN_TF), lambda i, j: (0, j)),
                pl.BlockSpec((FFN_TF, d), lambda i, j: (j, 0)),
                gain]
    args = [x, pre_g, wg, wu, wd, post_g]
    out_shape = [jax.ShapeDtypeStruct((t, d), F32)]
    out_specs = [row]
    if emit_next:
        in_specs.append(gain)
        args.append(next_g)
        out_shape.append(jax.ShapeDtypeStruct((t, d), BF16))
        out_specs.append(row)
    vmem = (2 * FFN_TM * d * 4 * 2 + 2 * FFN_TM * d * 2 + FFN_TM * d * (2 + 4)
            + 2 * 3 * d * FFN_TF * 2 + 3 * FFN_TM * FFN_TF * 4)
    res = pl.pallas_call(
        functools.partial(_ffn_body, emit_next),
        grid=(t // FFN_TM, dff // FFN_TF),
        in_specs=in_specs, out_specs=out_specs, out_shape=out_shape,
        scratch_shapes=[pltpu.VMEM((FFN_TM, d), BF16), pltpu.VMEM((FFN_TM, d), F32)],
        compiler_params=pltpu.CompilerParams(
            dimension_semantics=("arbitrary", "arbitrary"), vmem_limit_bytes=_vmem_limit(vmem)),
        name="ffn_next" if emit_next else "ffn",
    )(*args)
    return res if emit_next else res[0]


def _proj_body(gate, x_ref, w_ref, *rest):
    if gate:
        b_ref, o_ref = rest
    else:
        (o_ref,) = rest
    y = jnp.dot(x_ref[...], w_ref[...], preferred_element_type=F32)
    if gate:
        y = jax.nn.sigmoid(y + b_ref[...])
    o_ref[...] = y.astype(o_ref.dtype)


def _proj(x, w, out_dtype, name, bias=None):
    t, k = x.shape
    n = w.shape[1]
    tn = min(n, PROJ_TN)
    gate = bias is not None
    in_specs = [pl.BlockSpec((PROJ_TM, k), lambda i, j: (i, 0)),
                pl.BlockSpec((k, tn), lambda i, j: (0, j))]
    args = [x, w]
    if gate:
        in_specs.append(pl.BlockSpec((1, tn), lambda i, j: (0, j)))
        args.append(bias)
    vmem = 2 * PROJ_TM * k * 2 + 2 * k * tn * 2 + 3 * PROJ_TM * tn * 4
    return pl.pallas_call(
        functools.partial(_proj_body, gate),
        grid=(t // PROJ_TM, n // tn),
        in_specs=in_specs,
        out_specs=pl.BlockSpec((PROJ_TM, tn), lambda i, j: (i, j)),
        out_shape=jax.ShapeDtypeStruct((t, n), out_dtype),
        compiler_params=pltpu.CompilerParams(
            dimension_semantics=("arbitrary", "arbitrary"), vmem_limit_bytes=_vmem_limit(vmem)),
        name=name,
    )(*args)


def _weight_t_body(scale, w_ref, o_ref):
    o_ref[...] = (w_ref[...] * scale).T.astype(BF16)


def _weight_t(w, col0, ncols, scale=1.0):
    k = w.shape[0]
    assert col0 % WT_COLS == 0 and ncols % WT_COLS == 0
    vmem = 2 * k * WT_COLS * (4 + 2) + 2 * k * WT_COLS * 4
    return pl.pallas_call(
        functools.partial(_weight_t_body, scale),
        grid=(ncols // WT_COLS,),
        in_specs=[pl.BlockSpec((k, WT_COLS), lambda j: (0, col0 // WT_COLS + j))],
        out_specs=pl.BlockSpec((WT_COLS, k), lambda j: (j, 0)),
        out_shape=jax.ShapeDtypeStruct((ncols, k), BF16),
        compiler_params=pltpu.CompilerParams(
            dimension_semantics=("arbitrary",), vmem_limit_bytes=_vmem_limit(vmem)),
        name="weight_t",
    )(w)


def _proj_t_body(u_ref, wt_ref, wvt_ref, wwt_ref, o_ref, v_ref, w_ref):
    u = u_ref[...]
    for r in range(wt_ref.shape[0] // PROJT_ROWS):
        rows = slice(r * PROJT_ROWS, (r + 1) * PROJT_ROWS)
        o_ref[0, rows, :] = lax.dot_general(wt_ref[rows, :], u, _NT, preferred_element_type=F32).astype(BF16)
    heads_per_dot = PROJT_ROWS // HEAD_DIM
    for r in range(wvt_ref.shape[0] // PROJT_ROWS):
        vt = lax.dot_general(wvt_ref[r * PROJT_ROWS:(r + 1) * PROJT_ROWS, :], u, _NT,
                             preferred_element_type=F32).astype(BF16)
        for hh in range(heads_per_dot):
            h = r * heads_per_dot + hh
            v_ref[0, h * VT_ROWS:h * VT_ROWS + HEAD_DIM, :] = vt[hh * HEAD_DIM:(hh + 1) * HEAD_DIM, :]
            v_ref[0, h * VT_ROWS + HEAD_DIM:(h + 1) * VT_ROWS, :] = jnp.ones((VT_ROWS - HEAD_DIM, TILE), BF16)
    w_ref[0] = lax.dot_general(wwt_ref[...], u, _NT, preferred_element_type=F32)


def _proj_t(u, wt, wvt, wwt):
    t, k = u.shape
    n, nw = wt.shape[0], wwt.shape[0]
    nv = N_HEADS * VT_ROWS
    resident = pl.Buffered(1)
    vmem = (2 * TILE * k * 2 + (n + wvt.shape[0] + nw) * k * 2 + 2 * ((n + nv) * 2 + nw * 4) * TILE
            + 2 * PROJT_ROWS * TILE * 4)
    return pl.pallas_call(
        _proj_t_body,
        grid=(t // TILE,),
        in_specs=[pl.BlockSpec((TILE, k), lambda i: (i, 0)),
                  pl.BlockSpec((n, k), lambda i: (0, 0), pipeline_mode=resident),
                  pl.BlockSpec(wvt.shape, lambda i: (0, 0), pipeline_mode=resident),
                  pl.BlockSpec((nw, k), lambda i: (0, 0), pipeline_mode=resident)],
        out_specs=[pl.BlockSpec((1, n, TILE), lambda i: (i, 0, 0)),
                   pl.BlockSpec((1, nv, TILE), lambda i: (i, 0, 0)),
                   pl.BlockSpec((1, nw, TILE), lambda i: (i, 0, 0))],
        out_shape=[jax.ShapeDtypeStruct((t // TILE, n, TILE), BF16),
                   jax.ShapeDtypeStruct((t // TILE, nv, TILE), BF16),
                   jax.ShapeDtypeStruct((t // TILE, nw, TILE), F32)],
        compiler_params=pltpu.CompilerParams(
            dimension_semantics=("arbitrary",), vmem_limit_bytes=_vmem_limit(vmem)),
        name="proj_t",
    )(u, wt, wvt, wwt)


def _indexer_body(seq, qi_ref, w_ref, kw_ref, mask_ref, kb_ref, key_ref, hi_ref, lo_ref, mem_ref, low_ref):
    i = pl.program_id(1)
    nch = i + 1
    n_tiles = seq // TILE

    @pl.when(i == 0)
    def _():
        kb_ref[...] = kw_ref[:, :IDX_DIM].astype(BF16)

    w = w_ref[0] * (IDX_HEADS ** -0.5 * IDX_DIM ** -0.5)
    q_pos = i * TILE + lax.broadcasted_iota(jnp.int32, (1, TILE), 1)

    def rows_of(c):
        return pl.ds(pl.multiple_of(c * TILE, TILE), TILE)

    def score_tile(c):
        ks = kb_ref[rows_of(c), :]
        acc = jnp.zeros((TILE, TILE), F32)
        for h in range(IDX_HEADS):
            d = jnp.dot(ks, qi_ref[0, h * IDX_DIM:(h + 1) * IDX_DIM, :], preferred_element_type=F32)
            acc = acc + jnp.maximum(d, 0.0) * w[h:h + 1, :]
        bits = pltpu.bitcast(acc, jnp.int32)
        key = bits ^ ((bits >> 31) & jnp.int32(0x7FFFFFFF))
        k_pos = c * TILE + lax.broadcasted_iota(jnp.int32, (TILE, 1), 0)
        key = jnp.where(k_pos <= q_pos, key, jnp.int32(INT_MIN))
        key_ref[rows_of(c), :] = key
        hi_ref[rows_of(c), :] = (key >> 16).astype(I16)
        lo_ref[rows_of(c), :] = ((key & jnp.int32(0xFFFF)) + I16_MIN).astype(I16)

    def score_pair(t, carry):
        score_tile(2 * t)
        score_tile(2 * t + 1)
        return carry

    lax.fori_loop(0, nch // 2, score_pair, 0)

    @pl.when(nch % 2 == 1)
    def _():
        score_tile(nch - 1)

    n_steps = (nch + COUNT_TILES - 1) // COUNT_TILES
    step_rows = COUNT_TILES * TILE

    def rows_of_step(c):
        return pl.ds(pl.multiple_of(c * step_rows, step_rows), step_rows)

    def pad_chunk(c, carry):
        key_ref[rows_of(c), :] = jnp.full((TILE, TILE), INT_MIN, jnp.int32)
        hi_ref[rows_of(c), :] = jnp.full((TILE, TILE), I16_MIN, I16)
        lo_ref[rows_of(c), :] = jnp.full((TILE, TILE), I16_MIN, I16)
        return carry

    lax.fori_loop(nch, n_steps * COUNT_TILES, pad_chunk, 0)

    def packed(row):
        return jnp.broadcast_to(row, (PACKED_ROWS, TILE)).astype(I16)

    tile_groups = TILE // PACKED_ROWS

    def load16(ref, tile):
        return ref[rows_of(tile), :].reshape(tile_groups, PACKED_ROWS, TILE)

    def sum16(hits):
        parts = [None] * 4
        for hit in hits:
            ones = jnp.where(hit, jnp.int16(1), jnp.int16(0))
            for g in range(tile_groups):
                parts[g % 4] = ones[g] if parts[g % 4] is None else parts[g % 4] + ones[g]
        return (parts[0] + parts[1]) + (parts[2] + parts[3])

    def count16(c, hit_of_tile):
        return sum16(hit_of_tile(c * COUNT_TILES + t) for t in range(COUNT_TILES))

    def total16(cnt):
        return jnp.sum(cnt.astype(jnp.int32), axis=0, keepdims=True)

    def search16(src_ref, need):
        def search_pass(b, carry):
            u, n_u = carry
            cand_u = u | lax.shift_left(jnp.int32(1), 15 - b)
            cand = packed(cand_u + I16_MIN)
            cnt = lax.fori_loop(
                0, n_steps, lambda c, cnt: cnt + count16(c, lambda tile: load16(src_ref, tile) >= cand[None]),
                jnp.zeros((PACKED_ROWS, TILE), I16))
            total = total16(cnt)
            ok = total >= need
            return jnp.where(ok, cand_u, u), jnp.where(ok, total, n_u)

        zero = jnp.zeros((1, TILE), jnp.int32)
        return lax.fori_loop(0, 16, search_pass, (zero, zero))

    u_hi, _ = search16(hi_ref, jnp.int32(TOPK_MAX))
    t_hi = u_hi + I16_MIN
    t_hi16 = packed(t_hi)

    def member_chunk(c, carry):
        n_gt, n_mem, tops = carry
        tops = list(tops)
        hits = []
        for t in range(COUNT_TILES):
            tile = c * COUNT_TILES + t
            member = load16(hi_ref, tile) == t_hi16[None]
            vals = jnp.where(member, load16(lo_ref, tile), jnp.int16(I16_MIN))
            mem_ref[rows_of(tile), :] = vals.reshape(TILE, TILE)
            hits.append(member)
            for g in range(tile_groups):
                x = vals[g]
                for k in range(MEMBER_TOPS):
                    keep = tops[k] >= x
                    tops[k], x = jnp.where(keep, tops[k], x), jnp.where(keep, x, tops[k])
        n_mem = n_mem + sum16(hits)
        return n_gt + count16(c, lambda tile: load16(hi_ref, tile) > t_hi16[None]), n_mem, tuple(tops)

    zero16 = jnp.zeros((PACKED_ROWS, TILE), I16)
    floor16 = jnp.full((PACKED_ROWS, TILE), I16_MIN, I16)
    n_gt, n_mem, tops = lax.fori_loop(0, n_steps, member_chunk, (zero16, zero16, (floor16,) * MEMBER_TOPS))
    n_gt, n_mem = total16(n_gt), total16(n_mem)
    need_lo = TOPK_MAX - n_gt

    def kept_pass(b, carry):
        u, n_u = carry
        cand_u = u | lax.shift_left(jnp.int32(1), 15 - b)
        cand = packed(cand_u + I16_MIN)
        cnt = zero16
        for top in tops:
            cnt = cnt + jnp.where(top >= cand, jnp.int16(1), jnp.int16(0))
        ok = total16(cnt) >= need_lo
        return jnp.where(ok, cand_u, u), n_u

    zero = jnp.zeros((1, TILE), jnp.int32)
    u_kept, _ = lax.fori_loop(0, 16, kept_pass, (zero, zero))
    cand_kept = packed(u_kept + I16_MIN)
    n_at, n_above = lax.fori_loop(
        0, n_steps,
        lambda c, cnt: (cnt[0] + count16(c, lambda tile: load16(mem_ref, tile) >= cand_kept[None]),
                        cnt[1] + count16(c, lambda tile: load16(mem_ref, tile) > cand_kept[None])),
        (zero16, zero16))
    n_at, n_above = total16(n_at), total16(n_above)
    low_ref[0:1, :] = u_kept
    low_ref[1:2, :] = n_at

    @pl.when(jnp.max(jnp.where(n_above >= need_lo, 1, 0)) > 0)
    def _():
        u_full, n_full = search16(mem_ref, need_lo)
        low_ref[0:1, :] = u_full
        low_ref[1:2, :] = n_full

    u_lo, n_lo = low_ref[0:1, :], low_ref[1:2, :]
    thr = jnp.maximum(t_hi * 65536 + u_lo, jnp.int32(INT_MIN + 1))
    n_ge = jnp.where(u_hi > 0, n_gt + jnp.where(u_lo > 0, n_lo, n_mem), 0)

    def count32(pred):
        def count_chunk(c, cnt):
            hit = jnp.where(pred(key_ref[rows_of_step(c), :]), 1, 0)
            return cnt + jnp.sum(hit.reshape(step_rows // SUBLANES, SUBLANES, TILE), axis=0)

        cnt = lax.fori_loop(0, n_steps, count_chunk, jnp.zeros((SUBLANES, TILE), jnp.int32))
        return jnp.sum(cnt, axis=0, keepdims=True)

    has_ties = jnp.max(n_ge) > TOPK_MAX

    @pl.when(jnp.logical_not(has_ties))
    def _():
        def write_chunk(c, carry):
            mask_ref[0, 0, rows_of(c), :] = jnp.where(key_ref[rows_of(c), :] >= thr, 0.0, MASKED).astype(BF16)
            return carry

        lax.fori_loop(0, nch, write_chunk, 0)

    @pl.when(has_ties)
    def _():
        keep = (TOPK_MAX - count32(lambda key: key > thr)).astype(F32)
        below = jnp.where(lax.broadcasted_iota(jnp.int32, (TILE, TILE), 0)
                          >= lax.broadcasted_iota(jnp.int32, (TILE, TILE), 1), 1.0, 0.0).astype(BF16)

        def write_chunk(c, seen):
            key = key_ref[rows_of(c), :]
            tied = key == thr
            rank = jnp.dot(below, jnp.where(tied, 1.0, 0.0).astype(BF16), preferred_element_type=F32) + seen
            m = jnp.where(key > thr, 0.0, jnp.where(tied, jnp.where(rank <= keep, 0.0, MASKED), MASKED))
            mask_ref[0, 0, rows_of(c), :] = m.astype(BF16)
            return rank[TILE - 1:TILE, :]

        lax.fori_loop(0, nch, write_chunk, jnp.zeros((1, TILE), F32))

    def fill_chunk(c, carry):
        mask_ref[0, 0, rows_of(c), :] = jnp.full((TILE, TILE), MASKED, BF16)
        return carry

    lax.fori_loop(nch, n_tiles, fill_chunk, 0)


def _indexer(qt, wt, kw, batch, seq):
    nq = seq // TILE
    qi_block = 1
    vmem = (2 * IDX_HEADS * IDX_DIM * TILE * 2 + 2 * seq * LANES * 4 + 2 * seq * TILE * 2
            + seq * LANES * 2 + seq * TILE * (4 + 2 + 2 + 2) + 8 * TILE * TILE * 4)
    return pl.pallas_call(
        functools.partial(_indexer_body, seq),
        grid=(batch, nq),
        in_specs=[pl.BlockSpec((1, IDX_HEADS * IDX_DIM, TILE), lambda b, i: (b * nq + i, qi_block, 0)),
                  pl.BlockSpec((1, IDX_HEADS, TILE), lambda b, i: (b * nq + i, 0, 0)),
                  pl.BlockSpec((seq, LANES), lambda b, i: (b, 0))],
        out_specs=pl.BlockSpec((1, 1, seq, TILE), lambda b, i: (b, i, 0, 0)),
        out_shape=jax.ShapeDtypeStruct((batch, nq, seq, TILE), BF16),
        scratch_shapes=[pltpu.VMEM((seq, IDX_DIM), BF16),
                        pltpu.VMEM((seq, TILE), jnp.int32),
                        pltpu.VMEM((seq, TILE), I16),
                        pltpu.VMEM((seq, TILE), I16),
                        pltpu.VMEM((seq, TILE), I16),
                        pltpu.VMEM((SUBLANES, TILE), jnp.int32)],
        compiler_params=pltpu.CompilerParams(
            dimension_semantics=("arbitrary", "arbitrary"), vmem_limit_bytes=_vmem_limit(vmem)),
        name="indexer",
    )(qt, wt, kw)


def _bias_init(rb_ref, bias_ref):
    shape = (2 * TILE, TILE)
    dist = (lax.broadcasted_iota(jnp.int32, shape, 1) - lax.broadcasted_iota(jnp.int32, shape, 0) + TILE)
    max_exact = NUM_BUCKETS // 2
    n = jnp.maximum(dist, 0)
    nf = jnp.maximum(n, max_exact).astype(F32)
    large = max_exact + (jnp.log(nf / max_exact) / math.log(MAX_DISTANCE / max_exact)
                         * (NUM_BUCKETS - max_exact)).astype(jnp.int32)
    large = jnp.minimum(large, NUM_BUCKETS - 1)
    bucket = jnp.where(n < max_exact, n, large)
    for h in range(N_HEADS):
        far = rb_ref[NUM_BUCKETS - 1, h]
        acc = jnp.zeros(shape, F32)
        for b in range(NUM_BUCKETS - 1):
            acc = jnp.where(bucket == b, (rb_ref[b, h] - far) * LOG2E, acc)
        bias_ref[h] = acc


def _attn_body(rb_ref, q_ref, k_ref, vt_ref, mask_ref, o_ref, bias_ref, acc_ref, m_ref, l_ref, alpha_ref,
               p_ref):
    b = pl.program_id(0)
    i = pl.program_id(1)

    @pl.when((b == 0) & (i == 0))
    def _():
        _bias_init(rb_ref, bias_ref)

    acc_ref[...] = jnp.zeros_like(acc_ref)
    m_ref[...] = jnp.full_like(m_ref, MASKED)
    l_ref[...] = jnp.zeros_like(l_ref)

    eye = jnp.where(lax.broadcasted_iota(jnp.int32, (HEAD_DIM, HEAD_DIM), 0)
                    == lax.broadcasted_iota(jnp.int32, (HEAD_DIM, HEAD_DIM), 1), 1.0, 0.0).astype(BF16)

    def numerators(j, near, slot):
        halves = [pl.ds(pl.multiple_of(j * TILE + r * HEAD_DIM, HEAD_DIM), HEAD_DIM)
                  for r in range(TILE // HEAD_DIM)]
        masks = [mask_ref[0, 0, rows, :] for rows in halves]
        for h in range(N_HEADS):
            feat = slice(h * HEAD_DIM, (h + 1) * HEAD_DIM)
            s_parts = []
            for r, rows in enumerate(halves):
                lhs = jnp.concatenate([k_ref[rows, feat], eye], axis=1)
                rhs = jnp.concatenate([q_ref[0, feat, :], masks[r]], axis=0)
                s = jnp.dot(lhs, rhs, preferred_element_type=F32)
                if near is not None:
                    s = s + bias_ref[h, near * TILE + r * HEAD_DIM:near * TILE + (r + 1) * HEAD_DIM, :]
                s_parts.append(s)
            for c in range(TILE // LANES):
                lanes = slice(c * LANES, (c + 1) * LANES)
                sc = jnp.concatenate([s[:, lanes] for s in s_parts], axis=0)
                m_old = m_ref[h:h + 1, lanes]
                m_new = jnp.maximum(m_old, jnp.max(sc, axis=0, keepdims=True))
                p_ref[slot, h, :, lanes] = jnp.exp2(sc - m_new).astype(BF16)
                alpha_ref[slot, h:h + 1, lanes] = jnp.exp2(m_old - m_new)
                m_ref[h:h + 1, lanes] = m_new

    def values(j, slot):
        for h in range(N_HEADS):
            feat = slice(h * HEAD_DIM, (h + 1) * HEAD_DIM)
            pv = jnp.dot(vt_ref[j, h * VT_ROWS:(h + 1) * VT_ROWS, :], p_ref[slot, h], preferred_element_type=F32)
            alpha = alpha_ref[slot, h:h + 1, :]
            acc_ref[feat, :] = alpha * acc_ref[feat, :] + pv[:HEAD_DIM, :]
            l_ref[h:h + 1, :] = alpha * l_ref[h:h + 1, :] + pv[HEAD_DIM:HEAD_DIM + 1, :]

    def group(tiles):
        for slot, (j, near) in enumerate(tiles):
            numerators(j, near, slot)
        for slot, (j, near) in enumerate(tiles):
            values(j, slot)

    n_far = jnp.maximum(i - 1, 0)

    def far_group(t, carry):
        group([(ATT_GROUP * t + r, None) for r in range(ATT_GROUP)])
        return carry

    lax.fori_loop(0, n_far // ATT_GROUP, far_group, 0)
    rest = n_far % ATT_GROUP
    done = n_far - rest

    @pl.when(rest >= 2)
    def _():
        group([(done, None), (done + 1, None)])

    @pl.when(rest % 2 == 1)
    def _():
        group([(n_far - 1, None)])

    @pl.when(i >= 1)
    def _():
        group([(i - 1, 0), (i, 1)])

    @pl.when(i == 0)
    def _():
        group([(i, 1)])

    for h in range(N_HEADS):
        feat = slice(h * HEAD_DIM, (h + 1) * HEAD_DIM)
        o_ref[:, feat] = (acc_ref[feat, :] / l_ref[h:h + 1, :]).T.astype(o_ref.dtype)


def _attention(qt, k, vt, mask, rel_bias, batch, seq):
    nq = seq // TILE
    vmem = (2 * ATTN_DIM * TILE * 2 * 2 + seq * (ATTN_DIM + N_HEADS * VT_ROWS) * 2 + 2 * seq * TILE * 2
            + N_HEADS * 2 * TILE * TILE * 4 + ATTN_DIM * TILE * 4 + ATT_GROUP * N_HEADS * TILE * TILE * 2
            + 4 * TILE * TILE * 4)
    resident = pl.Buffered(1)
    return pl.pallas_call(
        _attn_body,
        grid=(batch, nq),
        in_specs=[pl.BlockSpec(memory_space=pltpu.SMEM),
                  pl.BlockSpec((1, ATTN_DIM, TILE), lambda b, i: (b * nq + i, 0, 0)),
                  pl.BlockSpec((seq, ATTN_DIM), lambda b, i: (b, 0), pipeline_mode=resident),
                  pl.BlockSpec((nq, N_HEADS * VT_ROWS, TILE), lambda b, i: (b, 0, 0), pipeline_mode=resident),
                  pl.BlockSpec((1, 1, seq, TILE), lambda b, i: (b, i, 0, 0))],
        out_specs=pl.BlockSpec((TILE, ATTN_DIM), lambda b, i: (b * nq + i, 0)),
        out_shape=jax.ShapeDtypeStruct((batch * seq, ATTN_DIM), BF16),
        scratch_shapes=[pltpu.VMEM((N_HEADS, 2 * TILE, TILE), F32),
                        pltpu.VMEM((ATTN_DIM, TILE), F32),
                        pltpu.VMEM((N_HEADS, TILE), F32),
                        pltpu.VMEM((N_HEADS, TILE), F32),
                        pltpu.VMEM((ATT_GROUP, N_HEADS, TILE), F32),
                        pltpu.VMEM((ATT_GROUP, N_HEADS, TILE, TILE), BF16)],
        compiler_params=pltpu.CompilerParams(
            dimension_semantics=("arbitrary", "arbitrary"), vmem_limit_bytes=_vmem_limit(vmem)),
        name="attention",
    )(rel_bias, qt, k, vt, mask)


def _conv_body(cur_ref, halo_ref, dw_ref, dwb_ref, lng_ref, lnb_ref, o_ref, h_ref, y_ref):
    i = pl.program_id(1)

    def glu(x):
        return x[:, :CONV_CH].astype(F32) * jax.nn.sigmoid(x[:, CONV_CH:].astype(F32))

    h_ref[0, :CONV_HALO, :] = jnp.where(i > 0, glu(halo_ref[...]), 0.0)
    h_ref[0, CONV_HALO:, :] = glu(cur_ref[...])
    n_rows = CONV_TS + CONV_HALO
    for r in range(1, SUBLANES):
        h_ref[r, SUBLANES:, :] = h_ref[0, SUBLANES - r:n_rows - r, :]

    for rb in range(CONV_TS // CONV_RB):
        for c in range(CONV_CH // CONV_CB):
            cols = slice(c * CONV_CB, (c + 1) * CONV_CB)
            acc = jnp.broadcast_to(dwb_ref[:, cols], (CONV_RB, CONV_CB))
            for j in range(CONV_WIDTH):
                groups, r = divmod(CONV_WIDTH - 1 - j, SUBLANES)
                start = CONV_HALO - groups * SUBLANES + rb * CONV_RB
                acc = acc + h_ref[r, start:start + CONV_RB, cols] * dw_ref[j:j + 1, cols]
            y_ref[rb * CONV_RB:(rb + 1) * CONV_RB, cols] = acc

    y = y_ref[...]
    mu = jnp.mean(y, axis=-1, keepdims=True)
    var = jnp.mean(jnp.square(y - mu), axis=-1, keepdims=True)
    z = (y - mu) * lax.rsqrt(var + EPS) * lng_ref[...] + lnb_ref[...]
    o_ref[...] = (z * jax.nn.sigmoid(z)).astype(o_ref.dtype)


def _conv_module(glu_in, dw, dw_b, ln_g, ln_b, batch, seq):
    ns = seq // CONV_TS
    halo_per_tile = CONV_TS // CONV_HALO
    vec = pl.BlockSpec((1, CONV_CH), lambda b, i: (0, 0))
    vmem = (2 * (CONV_TS + CONV_HALO) * 2 * CONV_CH * 2 + 2 * CONV_TS * CONV_CH * 2
            + ((SUBLANES + 1) * CONV_TS + SUBLANES * CONV_HALO) * CONV_CH * 4 + 4 * CONV_TS * CONV_CH * 4)
    return pl.pallas_call(
        _conv_body,
        grid=(batch, ns),
        in_specs=[pl.BlockSpec((CONV_TS, 2 * CONV_CH), lambda b, i: (b * ns + i, 0)),
                  pl.BlockSpec((CONV_HALO, 2 * CONV_CH),
                               lambda b, i: (jnp.maximum((b * ns + i) * halo_per_tile - 1, 0), 0)),
                  pl.BlockSpec((CONV_WIDTH, CONV_CH), lambda b, i: (0, 0)),
                  vec, vec, vec],
        out_specs=pl.BlockSpec((CONV_TS, CONV_CH), lambda b, i: (b * ns + i, 0)),
        out_shape=jax.ShapeDtypeStruct((batch * seq, CONV_CH), BF16),
        scratch_shapes=[pltpu.VMEM((SUBLANES, CONV_TS + CONV_HALO, CONV_CH), F32),
                        pltpu.VMEM((CONV_TS, CONV_CH), F32)],
        compiler_params=pltpu.CompilerParams(
            dimension_semantics=("arbitrary", "arbitrary"), vmem_limit_bytes=_vmem_limit(vmem)),
        name="conv_module",
    )(glu_in, glu_in, dw, dw_b, ln_g, ln_b)


def _mix_body(attn_ref, conv_ref, gates_ref, h_ref, wo_ref, wpw_ref, wout_ref, g_ref, o_ref):
    a = jnp.dot(attn_ref[...], wo_ref[...], preferred_element_type=F32)
    c = jnp.dot(conv_ref[...], wpw_ref[...], preferred_element_type=F32)
    mixed = (gates_ref[:, :D_MODEL].astype(F32) * a + gates_ref[:, D_MODEL:].astype(F32) * c).astype(BF16)
    y = jnp.dot(mixed, wout_ref[...], preferred_element_type=F32)
    o_ref[...] = h_ref[...] + _rms(y, g_ref[...])


def _mix_out(attn, conv, gates, h, w_o, w_pw2, w_out, post_g):
    t = h.shape[0]
    resident = pl.Buffered(1)

    def rows(n):
        return pl.BlockSpec((OUT_TM, n), lambda i: (i, 0))

    def whole(a):
        return pl.BlockSpec(a.shape, lambda i: (0, 0), pipeline_mode=resident)

    vmem = ((w_o.size + w_pw2.size + w_out.size) * 2
            + 2 * OUT_TM * (2 * ATTN_DIM * 2 + 2 * D_MODEL * 2 + 2 * D_MODEL * 4) + 5 * OUT_TM * D_MODEL * 4)
    return pl.pallas_call(
        _mix_body,
        grid=(t // OUT_TM,),
        in_specs=[rows(ATTN_DIM), rows(CONV_CH), rows(2 * D_MODEL), rows(D_MODEL),
                  whole(w_o), whole(w_pw2), whole(w_out), pl.BlockSpec((1, D_MODEL), lambda i: (0, 0))],
        out_specs=rows(D_MODEL),
        out_shape=jax.ShapeDtypeStruct((t, D_MODEL), F32),
        compiler_params=pltpu.CompilerParams(
            dimension_semantics=("arbitrary",), vmem_limit_bytes=_vmem_limit(vmem)),
        name="mix_out",
    )(attn, conv, gates, h, w_o, w_pw2, w_out, post_g)


def kernel(x, rel_bias, ffn1_pre_g, ffn1_wg, ffn1_wu, ffn1_wd, ffn1_post_g, mix_pre_g, w_in, b_gate, w_o,
           conv_dw, conv_dw_b, conv_ln_g, conv_ln_b, w_pw2, w_out, mix_post_g, ffn2_pre_g, ffn2_wg, ffn2_wu,
           ffn2_wd, ffn2_post_g):
    batch, seq, d = x.shape
    depth = ffn1_wg.shape[0]
    h = x.reshape(batch * seq, d)
    o_k = ATTN_DIM
    o_v = o_k + ATTN_DIM
    o_qi = o_v + ATTN_DIM
    o_ki = o_qi + IDX_HEADS * IDX_DIM
    o_wi = o_ki + IDX_DIM
    o_glu = o_wi + IDX_HEADS
    o_gates = o_glu + 2 * CONV_CH
    for l in range(depth):
        wl = w_in[l]
        w_t = jnp.concatenate([_weight_t(wl, 0, o_k, HEAD_DIM ** -0.5 * LOG2E),
                               _weight_t(wl, o_qi, o_ki - o_qi)], axis=0)
        w_vt = _weight_t(wl, o_v, o_qi - o_v)
        w_wt = wl[:, o_wi:o_glu].T.astype(BF16)
        w_k = wl[:, o_k:o_v].astype(BF16)
        w_ki = jnp.pad(wl[:, o_ki:o_wi], ((0, 0), (0, LANES - IDX_DIM))).astype(BF16)
        w_glu = wl[:, o_glu:o_gates].astype(BF16)
        w_gates = wl[:, o_gates:].astype(BF16)

        h, u = _ffn(h, ffn1_pre_g[l][None], ffn1_wg[l].astype(BF16), ffn1_wu[l].astype(BF16),
                    ffn1_wd[l].astype(BF16), ffn1_post_g[l][None], mix_pre_g[l][None])

        qt, vt, wt = _proj_t(u, w_t, w_vt, w_wt)
        k = _proj(u, w_k, BF16, "proj_k")
        kw = _proj(u, w_ki, F32, "proj_ki")
        glu_in = _proj(u, w_glu, BF16, "proj_glu")
        gates = _proj(u, w_gates, BF16, "proj_gates", bias=b_gate[l][None])

        mask = _indexer(qt, wt, kw, batch, seq)
        attn = _attention(qt, k, vt, mask, rel_bias, batch, seq)
        conv = _conv_module(glu_in, conv_dw[l], conv_dw_b[l][None], conv_ln_g[l][None], conv_ln_b[l][None],
                            batch, seq)
        h = _mix_out(attn, conv, gates, h, w_o[l].astype(BF16), w_pw2[l].astype(BF16),
                     w_out[l].astype(BF16), mix_post_g[l][None])

        h = _ffn(h, ffn2_pre_g[l][None], ffn2_wg[l].astype(BF16), ffn2_wu[l].astype(BF16),
                 ffn2_wd[l].astype(BF16), ffn2_post_g[l][None])
    return h.reshape(batch, seq, d)
```

```python
import functools
import math

import jax
import jax.numpy as jnp
from jax import lax
from jax.experimental import pallas as pl
from jax.experimental.pallas import tpu as pltpu

D_MODEL = 2048
N_HEADS = 8
HEAD_DIM = 128
ATTN_DIM = N_HEADS * HEAD_DIM
IDX_HEADS = 16
IDX_DIM = 64
TOPK_MAX = 256
NUM_BUCKETS = 32
MAX_DISTANCE = 128
CONV_CH = 1024
CONV_WIDTH = 31
D_FF = 5632
FFN_RES_WEIGHT = 0.5
EPS = 1e-6

F32 = jnp.float32
BF16 = jnp.bfloat16
LANES = 128
SUBLANES = 8
V7X_VMEM_BYTES = 64 * 1024 * 1024
MASKED = -1e30
INT_MIN = -(2 ** 31)
I16 = jnp.int16
I16_MIN = -(2 ** 15)
PACKED_ROWS = 16
LOG2E = math.log2(math.e)

FFN_TM, FFN_TF = 512, 512
PROJ_TM, PROJ_TN = 1024, 1024
TILE = 256
PROJT_ROWS = 512
WT_COLS = 512
ATT_GROUP = 4
COUNT_TILES = 4
MEMBER_TOPS = 6
VT_ROWS = HEAD_DIM + PACKED_ROWS
CONV_TS = 512
CONV_HALO = 32
CONV_RB, CONV_CB = 128, 256
OUT_TM = 256

_NT = (((1,), (1,)), ((), ()))


def _vmem_limit(nbytes):
    return int(min(nbytes + (8 << 20), V7X_VMEM_BYTES - (4 << 20)))


def _rms(x, g):
    y = x * lax.rsqrt(jnp.mean(x * x, axis=-1, keepdims=True) + EPS)
    return y * g


def _ffn_body(emit_next, x_ref, pre_g_ref, wg_ref, wu_ref, wd_ref, post_g_ref, *rest):
    if emit_next:
        next_g_ref, out_ref, u_ref, xn_ref, acc_ref = rest
    else:
        out_ref, xn_ref, acc_ref = rest
    j = pl.program_id(1)

    @pl.when(j == 0)
    def _():
        xn_ref[...] = _rms(x_ref[...], pre_g_ref[...]).astype(BF16)
        acc_ref[...] = jnp.zeros_like(acc_ref)

    xn = xn_ref[...]
    g = jnp.dot(xn, wg_ref[...], preferred_element_type=F32)
    u = jnp.dot(xn, wu_ref[...], preferred_element_type=F32)
    a = (g * jax.nn.sigmoid(g) * u).astype(BF16)
    acc_ref[...] += jnp.dot(a, wd_ref[...], preferred_element_type=F32)

    @pl.when(j == pl.num_programs(1) - 1)
    def _():
        h = x_ref[...] + FFN_RES_WEIGHT * _rms(acc_ref[...], post_g_ref[...])
        out_ref[...] = h
        if emit_next:
            u_ref[...] = _rms(h, next_g_ref[...]).astype(BF16)


def _ffn(x, pre_g, wg, wu, wd, post_g, next_g=None):
    t, d = x.shape
    dff = wg.shape[1]
    emit_next = next_g is not None
    row = pl.BlockSpec((FFN_TM, d), lambda i, j: (i, 0))
    gain = pl.BlockSpec((1, d), lambda i, j: (0, 0))
    in_specs = [row, gain,
                pl.BlockSpec((d, FFN_TF), lambda i, j: (0, j)),
                pl.BlockSpec((d, FFN_TF), lambda i, j: (0, j)),
                pl.BlockSpec((FFN_TF, d), lambda i, j: (j, 0)),
                gain]
    args = [x, pre_g, wg, wu, wd, post_g]
    out_shape = [jax.ShapeDtypeStruct((t, d), F32)]
    out_specs = [row]
    if emit_next:
        in_specs.append(gain)
        args.append(next_g)
        out_shape.append(jax.ShapeDtypeStruct((t, d), BF16))
        out_specs.append(row)
    vmem = (2 * FFN_TM * d * 4 * 2 + 2 * FFN_TM * d * 2 + FFN_TM * d * (2 + 4)
            + 2 * 3 * d * FFN_TF * 2 + 3 * FFN_TM * FFN_TF * 4)
    res = pl.pallas_call(
        functools.partial(_ffn_body, emit_next),
        grid=(t // FFN_TM, dff // FFN_TF),
        in_specs=in_specs, out_specs=out_specs, out_shape=out_shape,
        scratch_shapes=[pltpu.VMEM((FFN_TM, d), BF16), pltpu.VMEM((FFN_TM, d), F32)],
        compiler_params=pltpu.CompilerParams(
            dimension_semantics=("arbitrary", "arbitrary"), vmem_limit_bytes=_vmem_limit(vmem)),
        name="ffn_next" if emit_next else "ffn",
    )(*args)
    return res if emit_next else res[0]


def _proj_body(gate, x_ref, w_ref, *rest):
    if gate:
        b_ref, o_ref = rest
    else:
        (o_ref,) = rest
    y = jnp.dot(x_ref[...], w_ref[...], preferred_element_type=F32)
    if gate:
        y = jax.nn.sigmoid(y + b_ref[...])
    o_ref[...] = y.astype(o_ref.dtype)


def _proj(x, w, out_dtype, name, bias=None):
    t, k = x.shape
    n = w.shape[1]
    tn = min(n, PROJ_TN)
    gate = bias is not None
    in_specs = [pl.BlockSpec((PROJ_TM, k), lambda i, j: (i, 0)),
                pl.BlockSpec((k, tn), lambda i, j: (0, j))]
    args = [x, w]
    if gate:
        in_specs.append(pl.BlockSpec((1, tn), lambda i, j: (0, j)))
        args.append(bias)
    vmem = 2 * PROJ_TM * k * 2 + 2 * k * tn * 2 + 3 * PROJ_TM * tn * 4
    return pl.pallas_call(
        functools.partial(_proj_body, gate),
        grid=(t // PROJ_TM, n // tn),
        in_specs=in_specs,
        out_specs=pl.BlockSpec((PROJ_TM, tn), lambda i, j: (i, j)),
        out_shape=jax.ShapeDtypeStruct((t, n), out_dtype),
        compiler_params=pltpu.CompilerParams(
            dimension_semantics=("arbitrary", "arbitrary"), vmem_limit_bytes=_vmem_limit(vmem)),
        name=name,
    )(*args)


def _weight_t_body(scale, w_ref, o_ref):
    o_ref[...] = (w_ref[...] * scale).T.astype(BF16)


def _weight_t(w, col0, ncols, scale=1.0):
    k = w.shape[0]
    assert col0 % WT_COLS == 0 and ncols % WT_COLS == 0
    vmem = 2 * k * WT_COLS * (4 + 2) + 2 * k * WT_COLS * 4
    return pl.pallas_call(
        functools.partial(_weight_t_body, scale),
        grid=(ncols // WT_COLS,),
        in_specs=[pl.BlockSpec((k, WT_COLS), lambda j: (0, col0 // WT_COLS + j))],
        out_specs=pl.BlockSpec((WT_COLS, k), lambda j: (j, 0)),
        out_shape=jax.ShapeDtypeStruct((ncols, k), BF16),
        compiler_params=pltpu.CompilerParams(
            dimension_semantics=("arbitrary",), vmem_limit_bytes=_vmem_limit(vmem)),
        name="weight_t",
    )(w)


def _proj_t_body(u_ref, wt_ref, wvt_ref, wwt_ref, o_ref, v_ref, w_ref):
    u = u_ref[...]
    for r in range(wt_ref.shape[0] // PROJT_ROWS):
        rows = slice(r * PROJT_ROWS, (r + 1) * PROJT_ROWS)
        o_ref[0, rows, :] = lax.dot_general(wt_ref[rows, :], u, _NT, preferred_element_type=F32).astype(BF16)
    heads_per_dot = PROJT_ROWS // HEAD_DIM
    for r in range(wvt_ref.shape[0] // PROJT_ROWS):
        vt = lax.dot_general(wvt_ref[r * PROJT_ROWS:(r + 1) * PROJT_ROWS, :], u, _NT,
                             preferred_element_type=F32).astype(BF16)
        for hh in range(heads_per_dot):
            h = r * heads_per_dot + hh
            v_ref[0, h * VT_ROWS:h * VT_ROWS + HEAD_DIM, :] = vt[hh * HEAD_DIM:(hh + 1) * HEAD_DIM, :]
            v_ref[0, h * VT_ROWS + HEAD_DIM:(h + 1) * VT_ROWS, :] = jnp.ones((VT_ROWS - HEAD_DIM, TILE), BF16)
    w_ref[0] = lax.dot_general(wwt_ref[...], u, _NT, preferred_element_type=F32)


def _proj_t(u, wt, wvt, wwt):
    t, k = u.shape
    n, nw = wt.shape[0], wwt.shape[0]
    nv = N_HEADS * VT_ROWS
    resident = pl.Buffered(1)
    vmem = (2 * TILE * k * 2 + (n + wvt.shape[0] + nw) * k * 2 + 2 * ((n + nv) * 2 + nw * 4) * TILE
            + 2 * PROJT_ROWS * TILE * 4)
    return pl.pallas_call(
        _proj_t_body,
        grid=(t // TILE,),
        in_specs=[pl.BlockSpec((TILE, k), lambda i: (i, 0)),
                  pl.BlockSpec((n, k), lambda i: (0, 0), pipeline_mode=resident),
                  pl.BlockSpec(wvt.shape, lambda i: (0, 0), pipeline_mode=resident),
                  pl.BlockSpec((nw, k), lambda i: (0, 0), pipeline_mode=resident)],
        out_specs=[pl.BlockSpec((1, n, TILE), lambda i: (i, 0, 0)),
                   pl.BlockSpec((1, nv, TILE), lambda i: (i, 0, 0)),
                   pl.BlockSpec((1, nw, TILE), lambda i: (i, 0, 0))],
        out_shape=[jax.ShapeDtypeStruct((t // TILE, n, TILE), BF16),
                   jax.ShapeDtypeStruct((t // TILE, nv, TILE), BF16),
                   jax.ShapeDtypeStruct((t // TILE, nw, TILE), F32)],
        compiler_params=pltpu.CompilerParams(
            dimension_semantics=("arbitrary",), vmem_limit_bytes=_vmem_limit(vmem)),
        name="proj_t",
    )(u, wt, wvt, wwt)


def _indexer_body(seq, qi_ref, w_ref, kw_ref, mask_ref, kb_ref, key_ref, hi_ref, lo_ref, mem_ref, low_ref):
    i = pl.program_id(1)
    nch = i + 1
    n_tiles = seq // TILE

    @pl.when(i == 0)
    def _():
        kb_ref[...] = kw_ref[:, :IDX_DIM].astype(BF16)

    w = w_ref[0] * (IDX_HEADS ** -0.5 * IDX_DIM ** -0.5)
    q_pos = i * TILE + lax.broadcasted_iota(jnp.int32, (1, TILE), 1)

    def rows_of(c):
        return pl.ds(pl.multiple_of(c * TILE, TILE), TILE)

    def score_tile(c):
        ks = kb_ref[rows_of(c), :]
        acc = jnp.zeros((TILE, TILE), F32)
        for h in range(IDX_HEADS):
            d = jnp.dot(ks, qi_ref[0, h * IDX_DIM:(h + 1) * IDX_DIM, :], preferred_element_type=F32)
            acc = acc + jnp.maximum(d, 0.0) * w[h:h + 1, :]
        bits = pltpu.bitcast(acc, jnp.int32)
        key = bits ^ ((bits >> 31) & jnp.int32(0x7FFFFFFF))
        k_pos = c * TILE + lax.broadcasted_iota(jnp.int32, (TILE, 1), 0)
        key = jnp.where(k_pos <= q_pos, key, jnp.int32(INT_MIN))
        key_ref[rows_of(c), :] = key
        hi_ref[rows_of(c), :] = (key >> 16).astype(I16)
        lo_ref[rows_of(c), :] = ((key & jnp.int32(0xFFFF)) + I16_MIN).astype(I16)

    def score_pair(t, carry):
        for r in range(4):
            score_tile(4 * t + r)
        return carry

    lax.fori_loop(0, nch // 4, score_pair, 0)
    scored = nch - nch % 4

    @pl.when(nch % 4 >= 2)
    def _():
        score_tile(scored)
        score_tile(scored + 1)

    @pl.when(nch % 2 == 1)
    def _():
        score_tile(nch - 1)

    n_steps = (nch + COUNT_TILES - 1) // COUNT_TILES
    step_rows = COUNT_TILES * TILE

    def rows_of_step(c):
        return pl.ds(pl.multiple_of(c * step_rows, step_rows), step_rows)

    def pad_chunk(c, carry):
        key_ref[rows_of(c), :] = jnp.full((TILE, TILE), INT_MIN, jnp.int32)
        hi_ref[rows_of(c), :] = jnp.full((TILE, TILE), I16_MIN, I16)
        lo_ref[rows_of(c), :] = jnp.full((TILE, TILE), I16_MIN, I16)
        return carry

    lax.fori_loop(nch, n_steps * COUNT_TILES, pad_chunk, 0)

    def packed(row):
        return jnp.broadcast_to(row, (PACKED_ROWS, TILE)).astype(I16)

    tile_groups = TILE // PACKED_ROWS

    def load16(ref, tile):
        return ref[rows_of(tile), :].reshape(tile_groups, PACKED_ROWS, TILE)

    def sum16(hits):
        parts = [None] * 4
        for hit in hits:
            ones = jnp.where(hit, jnp.int16(1), jnp.int16(0))
            for g in range(tile_groups):
                parts[g % 4] = ones[g] if parts[g % 4] is None else parts[g % 4] + ones[g]
        return (parts[0] + parts[1]) + (parts[2] + parts[3])

    def count16(c, hit_of_tile):
        return sum16(hit_of_tile(c * COUNT_TILES + t) for t in range(COUNT_TILES))

    def total16(cnt):
        return jnp.sum(cnt.astype(jnp.int32), axis=0, keepdims=True)

    def search16(src_ref, need):
        def search_pass(b, carry):
            u, n_u = carry
            cand_u = u | lax.shift_left(jnp.int32(1), 15 - b)
            cand = packed(cand_u + I16_MIN)
            cnt = lax.fori_loop(
                0, n_steps, lambda c, cnt: cnt + count16(c, lambda tile: load16(src_ref, tile) >= cand[None]),
                jnp.zeros((PACKED_ROWS, TILE), I16))
            total = total16(cnt)
            ok = total >= need
            return jnp.where(ok, cand_u, u), jnp.where(ok, total, n_u)

        zero = jnp.zeros((1, TILE), jnp.int32)
        return lax.fori_loop(0, 16, search_pass, (zero, zero))

    u_hi, _ = search16(hi_ref, jnp.int32(TOPK_MAX))
    t_hi = u_hi + I16_MIN
    t_hi16 = packed(t_hi)

    def member_chunk(c, carry):
        n_gt, n_mem, tops = carry
        tops = list(tops)
        hits = []
        for t in range(COUNT_TILES):
            tile = c * COUNT_TILES + t
            member = load16(hi_ref, tile) == t_hi16[None]
            vals = jnp.where(member, load16(lo_ref, tile), jnp.int16(I16_MIN))
            mem_ref[rows_of(tile), :] = vals.reshape(TILE, TILE)
            hits.append(member)
            for g in range(tile_groups):
                x = vals[g]
                for k in range(MEMBER_TOPS):
                    keep = tops[k] >= x
                    tops[k], x = jnp.where(keep, tops[k], x), jnp.where(keep, x, tops[k])
        n_mem = n_mem + sum16(hits)
        return n_gt + count16(c, lambda tile: load16(hi_ref, tile) > t_hi16[None]), n_mem, tuple(tops)

    zero16 = jnp.zeros((PACKED_ROWS, TILE), I16)
    floor16 = jnp.full((PACKED_ROWS, TILE), I16_MIN, I16)
    n_gt, n_mem, tops = lax.fori_loop(0, n_steps, member_chunk, (zero16, zero16, (floor16,) * MEMBER_TOPS))
    n_gt, n_mem = total16(n_gt), total16(n_mem)
    need_lo = TOPK_MAX - n_gt

    def kept_pass(b, carry):
        u, n_u = carry
        cand_u = u | lax.shift_left(jnp.int32(1), 15 - b)
        cand = packed(cand_u + I16_MIN)
        cnt = zero16
        for top in tops:
            cnt = cnt + jnp.where(top >= cand, jnp.int16(1), jnp.int16(0))
        ok = total16(cnt) >= need_lo
        return jnp.where(ok, cand_u, u), n_u

    zero = jnp.zeros((1, TILE), jnp.int32)
    u_kept, _ = lax.fori_loop(0, 16, kept_pass, (zero, zero))
    cand_kept = packed(u_kept + I16_MIN)
    n_at, n_above = lax.fori_loop(
        0, n_steps,
        lambda c, cnt: (cnt[0] + count16(c, lambda tile: load16(mem_ref, tile) >= cand_kept[None]),
                        cnt[1] + count16(c, lambda tile: load16(mem_ref, tile) > cand_kept[None])),
        (zero16, zero16))
    n_at, n_above = total16(n_at), total16(n_above)
    low_ref[0:1, :] = u_kept
    low_ref[1:2, :] = n_at

    @pl.when(jnp.max(jnp.where(n_above >= need_lo, 1, 0)) > 0)
    def _():
        u_full, n_full = search16(mem_ref, need_lo)
        low_ref[0:1, :] = u_full
        low_ref[1:2, :] = n_full

    u_lo, n_lo = low_ref[0:1, :], low_ref[1:2, :]
    thr = jnp.maximum(t_hi * 65536 + u_lo, jnp.int32(INT_MIN + 1))
    n_ge = jnp.where(u_hi > 0, n_gt + jnp.where(u_lo > 0, n_lo, n_mem), 0)

    def count32(pred):
        def count_chunk(c, cnt):
            hit = jnp.where(pred(key_ref[rows_of_step(c), :]), 1, 0)
            return cnt + jnp.sum(hit.reshape(step_rows // SUBLANES, SUBLANES, TILE), axis=0)

        cnt = lax.fori_loop(0, n_steps, count_chunk, jnp.zeros((SUBLANES, TILE), jnp.int32))
        return jnp.sum(cnt, axis=0, keepdims=True)

    has_ties = jnp.max(n_ge) > TOPK_MAX

    @pl.when(jnp.logical_not(has_ties))
    def _():
        def write_chunk(c, carry):
            mask_ref[0, 0, rows_of(c), :] = jnp.where(key_ref[rows_of(c), :] >= thr, 0.0, MASKED).astype(BF16)
            return carry

        lax.fori_loop(0, nch, write_chunk, 0)

    @pl.when(has_ties)
    def _():
        keep = (TOPK_MAX - count32(lambda key: key > thr)).astype(F32)
        below = jnp.where(lax.broadcasted_iota(jnp.int32, (TILE, TILE), 0)
                          >= lax.broadcasted_iota(jnp.int32, (TILE, TILE), 1), 1.0, 0.0).astype(BF16)

        def write_chunk(c, seen):
            key = key_ref[rows_of(c), :]
            tied = key == thr
            rank = jnp.dot(below, jnp.where(tied, 1.0, 0.0).astype(BF16), preferred_element_type=F32) + seen
            m = jnp.where(key > thr, 0.0, jnp.where(tied, jnp.where(rank <= keep, 0.0, MASKED), MASKED))
            mask_ref[0, 0, rows_of(c), :] = m.astype(BF16)
            return rank[TILE - 1:TILE, :]

        lax.fori_loop(0, nch, write_chunk, jnp.zeros((1, TILE), F32))

    def fill_chunk(c, carry):
        mask_ref[0, 0, rows_of(c), :] = jnp.full((TILE, TILE), MASKED, BF16)
        return carry

    lax.fori_loop(nch, n_tiles, fill_chunk, 0)


def _indexer(qt, wt, kw, batch, seq):
    nq = seq // TILE
    qi_block = 1
    vmem = (2 * IDX_HEADS * IDX_DIM * TILE * 2 + 2 * seq * LANES * 4 + 2 * seq * TILE * 2
            + seq * LANES * 2 + seq * TILE * (4 + 2 + 2 + 2) + 8 * TILE * TILE * 4)
    return pl.pallas_call(
        functools.partial(_indexer_body, seq),
        grid=(batch, nq),
        in_specs=[pl.BlockSpec((1, IDX_HEADS * IDX_DIM, TILE), lambda b, i: (b * nq + i, qi_block, 0)),
                  pl.BlockSpec((1, IDX_HEADS, TILE), lambda b, i: (b * nq + i, 0, 0)),
                  pl.BlockSpec((seq, LANES), lambda b, i: (b, 0))],
        out_specs=pl.BlockSpec((1, 1, seq, TILE), lambda b, i: (b, i, 0, 0)),
        out_shape=jax.ShapeDtypeStruct((batch, nq, seq, TILE), BF16),
        scratch_shapes=[pltpu.VMEM((seq, IDX_DIM), BF16),
                        pltpu.VMEM((seq, TILE), jnp.int32),
                        pltpu.VMEM((seq, TILE), I16),
                        pltpu.VMEM((seq, TILE), I16),
                        pltpu.VMEM((seq, TILE), I16),
                        pltpu.VMEM((SUBLANES, TILE), jnp.int32)],
        compiler_params=pltpu.CompilerParams(
            dimension_semantics=("arbitrary", "arbitrary"), vmem_limit_bytes=_vmem_limit(vmem)),
        name="indexer",
    )(qt, wt, kw)


def _bias_init(rb_ref, bias_ref):
    shape = (2 * TILE, TILE)
    dist = (lax.broadcasted_iota(jnp.int32, shape, 1) - lax.broadcasted_iota(jnp.int32, shape, 0) + TILE)
    max_exact = NUM_BUCKETS // 2
    n = jnp.maximum(dist, 0)
    nf = jnp.maximum(n, max_exact).astype(F32)
    large = max_exact + (jnp.log(nf / max_exact) / math.log(MAX_DISTANCE / max_exact)
                         * (NUM_BUCKETS - max_exact)).astype(jnp.int32)
    large = jnp.minimum(large, NUM_BUCKETS - 1)
    bucket = jnp.where(n < max_exact, n, large)
    for h in range(N_HEADS):
        far = rb_ref[NUM_BUCKETS - 1, h]
        acc = jnp.zeros(shape, F32)
        for b in range(NUM_BUCKETS - 1):
            acc = jnp.where(bucket == b, (rb_ref[b, h] - far) * LOG2E, acc)
        bias_ref[h] = acc


def _attn_body(rb_ref, q_ref, k_ref, vt_ref, mask_ref, o_ref, bias_ref, acc_ref, m_ref, l_ref, alpha_ref,
               p_ref):
    b = pl.program_id(0)
    i = pl.program_id(1)

    @pl.when((b == 0) & (i == 0))
    def _():
        _bias_init(rb_ref, bias_ref)

    acc_ref[...] = jnp.zeros_like(acc_ref)
    m_ref[...] = jnp.full_like(m_ref, MASKED)
    l_ref[...] = jnp.zeros_like(l_ref)

    eye = jnp.where(lax.broadcasted_iota(jnp.int32, (HEAD_DIM, HEAD_DIM), 0)
                    == lax.broadcasted_iota(jnp.int32, (HEAD_DIM, HEAD_DIM), 1), 1.0, 0.0).astype(BF16)

    def numerators(j, near, slot):
        halves = [pl.ds(pl.multiple_of(j * TILE + r * HEAD_DIM, HEAD_DIM), HEAD_DIM)
                  for r in range(TILE // HEAD_DIM)]
        masks = [mask_ref[0, 0, rows, :] for rows in halves]
        for h in range(N_HEADS):
            feat = slice(h * HEAD_DIM, (h + 1) * HEAD_DIM)
            s_parts = []
            for r, rows in enumerate(halves):
                lhs = jnp.concatenate([k_ref[rows, feat], eye], axis=1)
                rhs = jnp.concatenate([q_ref[0, feat, :], masks[r]], axis=0)
                s = jnp.dot(lhs, rhs, preferred_element_type=F32)
                if near is not None:
                    s = s + bias_ref[h, near * TILE + r * HEAD_DIM:near * TILE + (r + 1) * HEAD_DIM, :]
                s_parts.append(s)
            for c in range(TILE // LANES):
                lanes = slice(c * LANES, (c + 1) * LANES)
                sc = jnp.concatenate([s[:, lanes] for s in s_parts], axis=0)
                m_old = m_ref[h:h + 1, lanes]
                m_new = jnp.maximum(m_old, jnp.max(sc, axis=0, keepdims=True))
                p_ref[slot, h, :, lanes] = jnp.exp2(sc - m_new).astype(BF16)
                alpha_ref[slot, h:h + 1, lanes] = jnp.exp2(m_old - m_new)
                m_ref[h:h + 1, lanes] = m_new

    def values(j, slot):
        for h in range(N_HEADS):
            feat = slice(h * HEAD_DIM, (h + 1) * HEAD_DIM)
            pv = jnp.dot(vt_ref[j, h * VT_ROWS:(h + 1) * VT_ROWS, :], p_ref[slot, h], preferred_element_type=F32)
            alpha = alpha_ref[slot, h:h + 1, :]
            acc_ref[feat, :] = alpha * acc_ref[feat, :] + pv[:HEAD_DIM, :]
            l_ref[h:h + 1, :] = alpha * l_ref[h:h + 1, :] + pv[HEAD_DIM:HEAD_DIM + 1, :]

    def group(tiles):
        for slot, (j, near) in enumerate(tiles):
            numerators(j, near, slot)
        for slot, (j, near) in enumerate(tiles):
            values(j, slot)

    n_far = jnp.maximum(i - 1, 0)

    def far_group(t, carry):
        group([(ATT_GROUP * t + r, None) for r in range(ATT_GROUP)])
        return carry

    lax.fori_loop(0, n_far // ATT_GROUP, far_group, 0)
    rest = n_far % ATT_GROUP
    done = n_far - rest

    @pl.when(rest >= 2)
    def _():
        group([(done, None), (done + 1, None)])

    @pl.when(rest % 2 == 1)
    def _():
        group([(n_far - 1, None)])

    @pl.when(i >= 1)
    def _():
        group([(i - 1, 0), (i, 1)])

    @pl.when(i == 0)
    def _():
        group([(i, 1)])

    for h in range(N_HEADS):
        feat = slice(h * HEAD_DIM, (h + 1) * HEAD_DIM)
        o_ref[:, feat] = (acc_ref[feat, :] / l_ref[h:h + 1, :]).T.astype(o_ref.dtype)


def _attention(qt, k, vt, mask, rel_bias, batch, seq):
    nq = seq // TILE
    vmem = (2 * ATTN_DIM * TILE * 2 * 2 + seq * (ATTN_DIM + N_HEADS * VT_ROWS) * 2 + 2 * seq * TILE * 2
            + N_HEADS * 2 * TILE * TILE * 4 + ATTN_DIM * TILE * 4 + ATT_GROUP * N_HEADS * TILE * TILE * 2
            + 4 * TILE * TILE * 4)
    resident = pl.Buffered(1)
    return pl.pallas_call(
        _attn_body,
        grid=(batch, nq),
        in_specs=[pl.BlockSpec(memory_space=pltpu.SMEM),
                  pl.BlockSpec((1, ATTN_DIM, TILE), lambda b, i: (b * nq + i, 0, 0)),
                  pl.BlockSpec((seq, ATTN_DIM), lambda b, i: (b, 0), pipeline_mode=resident),
                  pl.BlockSpec((nq, N_HEADS * VT_ROWS, TILE), lambda b, i: (b, 0, 0), pipeline_mode=resident),
                  pl.BlockSpec((1, 1, seq, TILE), lambda b, i: (b, i, 0, 0))],
        out_specs=pl.BlockSpec((TILE, ATTN_DIM), lambda b, i: (b * nq + i, 0)),
        out_shape=jax.ShapeDtypeStruct((batch * seq, ATTN_DIM), BF16),
        scratch_shapes=[pltpu.VMEM((N_HEADS, 2 * TILE, TILE), F32),
                        pltpu.VMEM((ATTN_DIM, TILE), F32),
                        pltpu.VMEM((N_HEADS, TILE), F32),
                        pltpu.VMEM((N_HEADS, TILE), F32),
                        pltpu.VMEM((ATT_GROUP, N_HEADS, TILE), F32),
                        pltpu.VMEM((ATT_GROUP, N_HEADS, TILE, TILE), BF16)],
        compiler_params=pltpu.CompilerParams(
            dimension_semantics=("arbitrary", "arbitrary"), vmem_limit_bytes=_vmem_limit(vmem)),
        name="attention",
    )(rel_bias, qt, k, vt, mask)


def _conv_body(cur_ref, halo_ref, dw_ref, dwb_ref, lng_ref, lnb_ref, o_ref, h_ref, y_ref):
    i = pl.program_id(1)

    def glu(x):
        return x[:, :CONV_CH].astype(F32) * jax.nn.sigmoid(x[:, CONV_CH:].astype(F32))

    h_ref[0, :CONV_HALO, :] = jnp.where(i > 0, glu(halo_ref[...]), 0.0)
    h_ref[0, CONV_HALO:, :] = glu(cur_ref[...])
    n_rows = CONV_TS + CONV_HALO
    for r in range(1, SUBLANES):
        h_ref[r, SUBLANES:, :] = h_ref[0, SUBLANES - r:n_rows - r, :]

    for rb in range(CONV_TS // CONV_RB):
        for c in range(CONV_CH // CONV_CB):
            cols = slice(c * CONV_CB, (c + 1) * CONV_CB)
            acc = jnp.broadcast_to(dwb_ref[:, cols], (CONV_RB, CONV_CB))
            for j in range(CONV_WIDTH):
                groups, r = divmod(CONV_WIDTH - 1 - j, SUBLANES)
                start = CONV_HALO - groups * SUBLANES + rb * CONV_RB
                acc = acc + h_ref[r, start:start + CONV_RB, cols] * dw_ref[j:j + 1, cols]
            y_ref[rb * CONV_RB:(rb + 1) * CONV_RB, cols] = acc

    y = y_ref[...]
    mu = jnp.mean(y, axis=-1, keepdims=True)
    var = jnp.mean(jnp.square(y - mu), axis=-1, keepdims=True)
    z = (y - mu) * lax.rsqrt(var + EPS) * lng_ref[...] + lnb_ref[...]
    o_ref[...] = (z * jax.nn.sigmoid(z)).astype(o_ref.dtype)


def _conv_module(glu_in, dw, dw_b, ln_g, ln_b, batch, seq):
    ns = seq // CONV_TS
    halo_per_tile = CONV_TS // CONV_HALO
    vec = pl.BlockSpec((1, CONV_CH), lambda b, i: (0, 0))
    vmem = (2 * (CONV_TS + CONV_HALO) * 2 * CONV_CH * 2 + 2 * CONV_TS * CONV_CH * 2
            + ((SUBLANES + 1) * CONV_TS + SUBLANES * CONV_HALO) * CONV_CH * 4 + 4 * CONV_TS * CONV_CH * 4)
    return pl.pallas_call(
        _conv_body,
        grid=(batch, ns),
        in_specs=[pl.BlockSpec((CONV_TS, 2 * CONV_CH), lambda b, i: (b * ns + i, 0)),
                  pl.BlockSpec((CONV_HALO, 2 * CONV_CH),
                               lambda b, i: (jnp.maximum((b * ns + i) * halo_per_tile - 1, 0), 0)),
                  pl.BlockSpec((CONV_WIDTH, CONV_CH), lambda b, i: (0, 0)),
                  vec, vec, vec],
        out_specs=pl.BlockSpec((CONV_TS, CONV_CH), lambda b, i: (b * ns + i, 0)),
        out_shape=jax.ShapeDtypeStruct((batch * seq, CONV_CH), BF16),
        scratch_shapes=[pltpu.VMEM((SUBLANES, CONV_TS + CONV_HALO, CONV_CH), F32),
                        pltpu.VMEM((CONV_TS, CONV_CH), F32)],
        compiler_params=pltpu.CompilerParams(
            dimension_semantics=("arbitrary", "arbitrary"), vmem_limit_bytes=_vmem_limit(vmem)),
        name="conv_module",
    )(glu_in, glu_in, dw, dw_b, ln_g, ln_b)


def _mix_body(attn_ref, conv_ref, gates_ref, h_ref, wo_ref, wpw_ref, wout_ref, g_ref, o_ref):
    a = jnp.dot(attn_ref[...], wo_ref[...], preferred_element_type=F32)
    c = jnp.dot(conv_ref[...], wpw_ref[...], preferred_element_type=F32)
    mixed = (gates_ref[:, :D_MODEL].astype(F32) * a + gates_ref[:, D_MODEL:].astype(F32) * c).astype(BF16)
    y = jnp.dot(mixed, wout_ref[...], preferred_element_type=F32)
    o_ref[...] = h_ref[...] + _rms(y, g_ref[...])


def _mix_out(attn, conv, gates, h, w_o, w_pw2, w_out, post_g):
    t = h.shape[0]
    resident = pl.Buffered(1)

    def rows(n):
        return pl.BlockSpec((OUT_TM, n), lambda i: (i, 0))

    def whole(a):
        return pl.BlockSpec(a.shape, lambda i: (0, 0), pipeline_mode=resident)

    vmem = ((w_o.size + w_pw2.size + w_out.size) * 2
            + 2 * OUT_TM * (2 * ATTN_DIM * 2 + 2 * D_MODEL * 2 + 2 * D_MODEL * 4) + 5 * OUT_TM * D_MODEL * 4)
    return pl.pallas_call(
        _mix_body,
        grid=(t // OUT_TM,),
        in_specs=[rows(ATTN_DIM), rows(CONV_CH), rows(2 * D_MODEL), rows(D_MODEL),
                  whole(w_o), whole(w_pw2), whole(w_out), pl.BlockSpec((1, D_MODEL), lambda i: (0, 0))],
        out_specs=rows(D_MODEL),
        out_shape=jax.ShapeDtypeStruct((t, D_MODEL), F32),
        compiler_params=pltpu.CompilerParams(
            dimension_semantics=("arbitrary",), vmem_limit_bytes=_vmem_limit(vmem)),
        name="mix_out",
    )(attn, conv, gates, h, w_o, w_pw2, w_out, post_g)


def kernel(x, rel_bias, ffn1_pre_g, ffn1_wg, ffn1_wu, ffn1_wd, ffn1_post_g, mix_pre_g, w_in, b_gate, w_o,
           conv_dw, conv_dw_b, conv_ln_g, conv_ln_b, w_pw2, w_out, mix_post_g, ffn2_pre_g, ffn2_wg, ffn2_wu,
           ffn2_wd, ffn2_post_g):
    batch, seq, d = x.shape
    depth = ffn1_wg.shape[0]
    h = x.reshape(batch * seq, d)
    o_k = ATTN_DIM
    o_v = o_k + ATTN_DIM
    o_qi = o_v + ATTN_DIM
    o_ki = o_qi + IDX_HEADS * IDX_DIM
    o_wi = o_ki + IDX_DIM
    o_glu = o_wi + IDX_HEADS
    o_gates = o_glu + 2 * CONV_CH
    for l in range(depth):
        wl = w_in[l]
        w_t = jnp.concatenate([_weight_t(wl, 0, o_k, HEAD_DIM ** -0.5 * LOG2E),
                               _weight_t(wl, o_qi, o_ki - o_qi)], axis=0)
        w_vt = _weight_t(wl, o_v, o_qi - o_v)
        w_wt = wl[:, o_wi:o_glu].T.astype(BF16)
        w_k = wl[:, o_k:o_v].astype(BF16)
        w_ki = jnp.pad(wl[:, o_ki:o_wi], ((0, 0), (0, LANES - IDX_DIM))).astype(BF16)
        w_glu = wl[:, o_glu:o_gates].astype(BF16)
        w_gates = wl[:, o_gates:].astype(BF16)

        h, u = _ffn(h, ffn1_pre_g[l][None], ffn1_wg[l].astype(BF16), ffn1_wu[l].astype(BF16),
                    ffn1_wd[l].astype(BF16), ffn1_post_g[l][None], mix_pre_g[l][None])

        qt, vt, wt = _proj_t(u, w_t, w_vt, w_wt)
        k = _proj(u, w_k, BF16, "proj_k")
        kw = _proj(u, w_ki, F32, "proj_ki")
        glu_in = _proj(u, w_glu, BF16, "proj_glu")
        gates = _proj(u, w_gates, BF16, "proj_gates", bias=b_gate[l][None])

        mask = _indexer(qt, wt, kw, batch, seq)
        attn = _attention(qt, k, vt, mask, rel_bias, batch, seq)
        conv = _conv_module(glu_in, conv_dw[l], conv_dw_b[l][None], conv_ln_g[l][None], conv_ln_b[l][None],
                            batch, seq)
        h = _mix_out(attn, conv, gates, h, w_o[l].astype(BF16), w_pw2[l].astype(BF16),
                     w_out[l].astype(BF16), mix_post_g[l][None])

        h = _ffn(h, ffn2_pre_g[l][None], ffn2_wg[l].astype(BF16), ffn2_wu[l].astype(BF16),
                 ffn2_wd[l].astype(BF16), ffn2_post_g[l][None])
    return h.reshape(batch, seq, d)
```

```python
import functools
import math

import jax
import jax.numpy as jnp
from jax import lax
from jax.experimental import pallas as pl
from jax.experimental.pallas import tpu as pltpu

D_MODEL = 2048
N_HEADS = 8
HEAD_DIM = 128
ATTN_DIM = N_HEADS * HEAD_DIM
IDX_HEADS = 16
IDX_DIM = 64
TOPK_MAX = 256
NUM_BUCKETS = 32
MAX_DISTANCE = 128
CONV_CH = 1024
CONV_WIDTH = 31
D_FF = 5632
FFN_RES_WEIGHT = 0.5
EPS = 1e-6

F32 = jnp.float32
BF16 = jnp.bfloat16
LANES = 128
SUBLANES = 8
V7X_VMEM_BYTES = 64 * 1024 * 1024
MASKED = -1e30
INT_MIN = -(2 ** 31)
I16 = jnp.int16
I16_MIN = -(2 ** 15)
PACKED_ROWS = 16
LOG2E = math.log2(math.e)

FFN_TM, FFN_TF = 512, 512
PROJ_TM, PROJ_TN = 1024, 1024
TILE = 256
PROJT_ROWS = 512
WT_COLS = 512
ATT_GROUP = 4
COUNT_TILES = 4
MEMBER_TOPS = 6
VT_ROWS = HEAD_DIM + PACKED_ROWS
CONV_HALO = 32
CONV_RB, CONV_CB = 128, 256
MIX_TM = 256

_NT = (((1,), (1,)), ((), ()))


def _vmem_limit(nbytes):
    return int(min(nbytes + (8 << 20), V7X_VMEM_BYTES - (4 << 20)))


def _rms(x, g):
    y = x * lax.rsqrt(jnp.mean(x * x, axis=-1, keepdims=True) + EPS)
    return y * g


def _ffn_body(emit_next, x_ref, pre_g_ref, wg_ref, wu_ref, wd_ref, post_g_ref, *rest):
    if emit_next:
        next_g_ref, out_ref, u_ref, xn_ref, acc_ref = rest
    else:
        out_ref, xn_ref, acc_ref = rest
    j = pl.program_id(1)

    @pl.when(j == 0)
    def _():
        xn_ref[...] = _rms(x_ref[...], pre_g_ref[...]).astype(BF16)
        acc_ref[...] = jnp.zeros_like(acc_ref)

    xn = xn_ref[...]
    g = jnp.dot(xn, wg_ref[...], preferred_element_type=F32)
    u = jnp.dot(xn, wu_ref[...], preferred_element_type=F32)
    a = (g * jax.nn.sigmoid(g) * u).astype(BF16)
    acc_ref[...] += jnp.dot(a, wd_ref[...], preferred_element_type=F32)

    @pl.when(j == pl.num_programs(1) - 1)
    def _():
        h = x_ref[...] + FFN_RES_WEIGHT * _rms(acc_ref[...], post_g_ref[...])
        out_ref[...] = h
        if emit_next:
            u_ref[...] = _rms(h, next_g_ref[...]).astype(BF16)


def _ffn(x, pre_g, wg, wu, wd, post_g, next_g=None):
    t, d = x.shape
    dff = wg.shape[1]
    emit_next = next_g is not None
    row = pl.BlockSpec((FFN_TM, d), lambda i, j: (i, 0))
    gain = pl.BlockSpec((1, d), lambda i, j: (0, 0))
    in_specs = [row, gain,
                pl.BlockSpec((d, FFN_TF), lambda i, j: (0, j)),
                pl.BlockSpec((d, FFN_TF), lambda i, j: (0, j)),
                pl.BlockSpec((FFN_TF, d), lambda i, j: (j, 0)),
                gain]
    args = [x, pre_g, wg, wu, wd, post_g]
    out_shape = [jax.ShapeDtypeStruct((t, d), F32)]
    out_specs = [row]
    if emit_next:
        in_specs.append(gain)
        args.append(next_g)
        out_shape.append(jax.ShapeDtypeStruct((t, d), BF16))
        out_specs.append(row)
    vmem = (2 * FFN_TM * d * 4 * 2 + 2 * FFN_TM * d * 2 + FFN_TM * d * (2 + 4)
            + 2 * 3 * d * FFN_TF * 2 + 3 * FFN_TM * FFN_TF * 4)
    res = pl.pallas_call(
        functools.partial(_ffn_body, emit_next),
        grid=(t // FFN_TM, dff // FFN_TF),
        in_specs=in_specs, out_specs=out_specs, out_shape=out_shape,
        scratch_shapes=[pltpu.VMEM((FFN_TM, d), BF16), pltpu.VMEM((FFN_TM, d), F32)],
        compiler_params=pltpu.CompilerParams(
            dimension_semantics=("arbitrary", "arbitrary"), vmem_limit_bytes=_vmem_limit(vmem)),
        name="ffn_next" if emit_next else "ffn",
    )(*args)
    return res if emit_next else res[0]


def _proj_body(gate, x_ref, w_ref, *rest):
    if gate:
        b_ref, o_ref = rest
    else:
        (o_ref,) = rest
    y = jnp.dot(x_ref[...], w_ref[...], preferred_element_type=F32)
    if gate:
        y = jax.nn.sigmoid(y + b_ref[...])
    o_ref[...] = y.astype(o_ref.dtype)


def _proj(x, w, out_dtype, name, bias=None):
    t, k = x.shape
    n = w.shape[1]
    tn = min(n, PROJ_TN)
    gate = bias is not None
    in_specs = [pl.BlockSpec((PROJ_TM, k), lambda i, j: (i, 0)),
                pl.BlockSpec((k, tn), lambda i, j: (0, j))]
    args = [x, w]
    if gate:
        in_specs.append(pl.BlockSpec((1, tn), lambda i, j: (0, j)))
        args.append(bias)
    vmem = 2 * PROJ_TM * k * 2 + 2 * k * tn * 2 + 3 * PROJ_TM * tn * 4
    return pl.pallas_call(
        functools.partial(_proj_body, gate),
        grid=(t // PROJ_TM, n // tn),
        in_specs=in_specs,
        out_specs=pl.BlockSpec((PROJ_TM, tn), lambda i, j: (i, j)),
        out_shape=jax.ShapeDtypeStruct((t, n), out_dtype),
        compiler_params=pltpu.CompilerParams(
            dimension_semantics=("arbitrary", "arbitrary"), vmem_limit_bytes=_vmem_limit(vmem)),
        name=name,
    )(*args)


def _weight_t_body(scale, w_ref, o_ref):
    o_ref[...] = (w_ref[...] * scale).T.astype(BF16)


def _weight_t(w, col0, ncols, scale=1.0):
    k = w.shape[0]
    assert col0 % WT_COLS == 0 and ncols % WT_COLS == 0
    vmem = 2 * k * WT_COLS * (4 + 2) + 2 * k * WT_COLS * 4
    return pl.pallas_call(
        functools.partial(_weight_t_body, scale),
        grid=(ncols // WT_COLS,),
        in_specs=[pl.BlockSpec((k, WT_COLS), lambda j: (0, col0 // WT_COLS + j))],
        out_specs=pl.BlockSpec((WT_COLS, k), lambda j: (j, 0)),
        out_shape=jax.ShapeDtypeStruct((ncols, k), BF16),
        compiler_params=pltpu.CompilerParams(
            dimension_semantics=("arbitrary",), vmem_limit_bytes=_vmem_limit(vmem)),
        name="weight_t",
    )(w)


def _proj_t_body(u_ref, wt_ref, wvt_ref, wwt_ref, o_ref, v_ref, w_ref):
    u = u_ref[...]
    for r in range(wt_ref.shape[0] // PROJT_ROWS):
        rows = slice(r * PROJT_ROWS, (r + 1) * PROJT_ROWS)
        o_ref[0, rows, :] = lax.dot_general(wt_ref[rows, :], u, _NT, preferred_element_type=F32).astype(BF16)
    heads_per_dot = PROJT_ROWS // HEAD_DIM
    for r in range(wvt_ref.shape[0] // PROJT_ROWS):
        vt = lax.dot_general(wvt_ref[r * PROJT_ROWS:(r + 1) * PROJT_ROWS, :], u, _NT,
                             preferred_element_type=F32).astype(BF16)
        for hh in range(heads_per_dot):
            h = r * heads_per_dot + hh
            v_ref[0, h * VT_ROWS:h * VT_ROWS + HEAD_DIM, :] = vt[hh * HEAD_DIM:(hh + 1) * HEAD_DIM, :]
            v_ref[0, h * VT_ROWS + HEAD_DIM:(h + 1) * VT_ROWS, :] = jnp.ones((VT_ROWS - HEAD_DIM, TILE), BF16)
    w_ref[0] = lax.dot_general(wwt_ref[...], u, _NT, preferred_element_type=F32)


def _proj_t(u, wt, wvt, wwt):
    t, k = u.shape
    n, nw = wt.shape[0], wwt.shape[0]
    nv = N_HEADS * VT_ROWS
    resident = pl.Buffered(1)
    vmem = (2 * TILE * k * 2 + (n + wvt.shape[0] + nw) * k * 2 + 2 * ((n + nv) * 2 + nw * 4) * TILE
            + 2 * PROJT_ROWS * TILE * 4)
    return pl.pallas_call(
        _proj_t_body,
        grid=(t // TILE,),
        in_specs=[pl.BlockSpec((TILE, k), lambda i: (i, 0)),
                  pl.BlockSpec((n, k), lambda i: (0, 0), pipeline_mode=resident),
                  pl.BlockSpec(wvt.shape, lambda i: (0, 0), pipeline_mode=resident),
                  pl.BlockSpec((nw, k), lambda i: (0, 0), pipeline_mode=resident)],
        out_specs=[pl.BlockSpec((1, n, TILE), lambda i: (i, 0, 0)),
                   pl.BlockSpec((1, nv, TILE), lambda i: (i, 0, 0)),
                   pl.BlockSpec((1, nw, TILE), lambda i: (i, 0, 0))],
        out_shape=[jax.ShapeDtypeStruct((t // TILE, n, TILE), BF16),
                   jax.ShapeDtypeStruct((t // TILE, nv, TILE), BF16),
                   jax.ShapeDtypeStruct((t // TILE, nw, TILE), F32)],
        compiler_params=pltpu.CompilerParams(
            dimension_semantics=("arbitrary",), vmem_limit_bytes=_vmem_limit(vmem)),
        name="proj_t",
    )(u, wt, wvt, wwt)


def _indexer_body(seq, qi_ref, w_ref, kw_ref, mask_ref, kb_ref, key_ref, hi_ref, lo_ref, mem_ref, low_ref):
    i = pl.program_id(1)
    nch = i + 1
    n_tiles = seq // TILE

    @pl.when(i == 0)
    def _():
        kb_ref[...] = kw_ref[:, :IDX_DIM].astype(BF16)

    w = w_ref[0] * (IDX_HEADS ** -0.5 * IDX_DIM ** -0.5)
    q_pos = i * TILE + lax.broadcasted_iota(jnp.int32, (1, TILE), 1)

    def rows_of(c):
        return pl.ds(pl.multiple_of(c * TILE, TILE), TILE)

    def score_tile(c):
        ks = kb_ref[rows_of(c), :]
        acc = jnp.zeros((TILE, TILE), F32)
        for h in range(IDX_HEADS):
            d = jnp.dot(ks, qi_ref[0, h * IDX_DIM:(h + 1) * IDX_DIM, :], preferred_element_type=F32)
            acc = acc + jnp.maximum(d, 0.0) * w[h:h + 1, :]
        bits = pltpu.bitcast(acc, jnp.int32)
        key = bits ^ ((bits >> 31) & jnp.int32(0x7FFFFFFF))
        k_pos = c * TILE + lax.broadcasted_iota(jnp.int32, (TILE, 1), 0)
        key = jnp.where(k_pos <= q_pos, key, jnp.int32(INT_MIN))
        key_ref[rows_of(c), :] = key
        hi_ref[rows_of(c), :] = (key >> 16).astype(I16)
        lo_ref[rows_of(c), :] = ((key & jnp.int32(0xFFFF)) + I16_MIN).astype(I16)

    def score_pair(t, carry):
        for r in range(4):
            score_tile(4 * t + r)
        return carry

    lax.fori_loop(0, nch // 4, score_pair, 0)
    scored = nch - nch % 4

    @pl.when(nch % 4 >= 2)
    def _():
        score_tile(scored)
        score_tile(scored + 1)

    @pl.when(nch % 2 == 1)
    def _():
        score_tile(nch - 1)

    n_steps = (nch + COUNT_TILES - 1) // COUNT_TILES
    step_rows = COUNT_TILES * TILE

    def rows_of_step(c):
        return pl.ds(pl.multiple_of(c * step_rows, step_rows), step_rows)

    def pad_chunk(c, carry):
        key_ref[rows_of(c), :] = jnp.full((TILE, TILE), INT_MIN, jnp.int32)
        hi_ref[rows_of(c), :] = jnp.full((TILE, TILE), I16_MIN, I16)
        lo_ref[rows_of(c), :] = jnp.full((TILE, TILE), I16_MIN, I16)
        return carry

    lax.fori_loop(nch, n_steps * COUNT_TILES, pad_chunk, 0)

    def packed(row):
        return jnp.broadcast_to(row, (PACKED_ROWS, TILE)).astype(I16)

    tile_groups = TILE // PACKED_ROWS

    def load16(ref, tile):
        return ref[rows_of(tile), :].reshape(tile_groups, PACKED_ROWS, TILE)

    def sum16(hits):
        parts = [None] * 4
        for hit in hits:
            ones = jnp.where(hit, jnp.int16(1), jnp.int16(0))
            for g in range(tile_groups):
                parts[g % 4] = ones[g] if parts[g % 4] is None else parts[g % 4] + ones[g]
        return (parts[0] + parts[1]) + (parts[2] + parts[3])

    def count16(c, hit_of_tile):
        return sum16(hit_of_tile(c * COUNT_TILES + t) for t in range(COUNT_TILES))

    def total16(cnt):
        return jnp.sum(cnt.astype(jnp.int32), axis=0, keepdims=True)

    def search16(src_ref, need):
        def search_pass(b, carry):
            u, n_u = carry
            cand_u = u | lax.shift_left(jnp.int32(1), 15 - b)
            cand = packed(cand_u + I16_MIN)
            cnt = lax.fori_loop(
                0, n_steps, lambda c, cnt: cnt + count16(c, lambda tile: load16(src_ref, tile) >= cand[None]),
                jnp.zeros((PACKED_ROWS, TILE), I16))
            total = total16(cnt)
            ok = total >= need
            return jnp.where(ok, cand_u, u), jnp.where(ok, total, n_u)

        zero = jnp.zeros((1, TILE), jnp.int32)
        return lax.fori_loop(0, 16, search_pass, (zero, zero))

    u_hi, _ = search16(hi_ref, jnp.int32(TOPK_MAX))
    t_hi = u_hi + I16_MIN
    t_hi16 = packed(t_hi)

    def member_chunk(c, carry):
        n_gt, n_mem, tops = carry
        tops = list(tops)
        hits = []
        for t in range(COUNT_TILES):
            tile = c * COUNT_TILES + t
            member = load16(hi_ref, tile) == t_hi16[None]
            vals = jnp.where(member, load16(lo_ref, tile), jnp.int16(I16_MIN))
            mem_ref[rows_of(tile), :] = vals.reshape(TILE, TILE)
            hits.append(member)
            for g in range(tile_groups):
                x = vals[g]
                for k in range(MEMBER_TOPS):
                    keep = tops[k] >= x
                    tops[k], x = jnp.where(keep, tops[k], x), jnp.where(keep, x, tops[k])
        n_mem = n_mem + sum16(hits)
        return n_gt + count16(c, lambda tile: load16(hi_ref, tile) > t_hi16[None]), n_mem, tuple(tops)

    zero16 = jnp.zeros((PACKED_ROWS, TILE), I16)
    floor16 = jnp.full((PACKED_ROWS, TILE), I16_MIN, I16)
    n_gt, n_mem, tops = lax.fori_loop(0, n_steps, member_chunk, (zero16, zero16, (floor16,) * MEMBER_TOPS))
    n_gt, n_mem = total16(n_gt), total16(n_mem)
    need_lo = TOPK_MAX - n_gt

    def kept_pass(b, carry):
        u, n_u = carry
        cand_u = u | lax.shift_left(jnp.int32(1), 15 - b)
        cand = packed(cand_u + I16_MIN)
        cnt = zero16
        for top in tops:
            cnt = cnt + jnp.where(top >= cand, jnp.int16(1), jnp.int16(0))
        ok = total16(cnt) >= need_lo
        return jnp.where(ok, cand_u, u), n_u

    zero = jnp.zeros((1, TILE), jnp.int32)
    u_kept, _ = lax.fori_loop(0, 16, kept_pass, (zero, zero))
    cand_kept = packed(u_kept + I16_MIN)
    n_at, n_above = lax.fori_loop(
        0, n_steps,
        lambda c, cnt: (cnt[0] + count16(c, lambda tile: load16(mem_ref, tile) >= cand_kept[None]),
                        cnt[1] + count16(c, lambda tile: load16(mem_ref, tile) > cand_kept[None])),
        (zero16, zero16))
    n_at, n_above = total16(n_at), total16(n_above)
    low_ref[0:1, :] = u_kept
    low_ref[1:2, :] = n_at

    @pl.when(jnp.max(jnp.where(n_above >= need_lo, 1, 0)) > 0)
    def _():
        u_full, n_full = search16(mem_ref, need_lo)
        low_ref[0:1, :] = u_full
        low_ref[1:2, :] = n_full

    u_lo, n_lo = low_ref[0:1, :], low_ref[1:2, :]
    thr = jnp.maximum(t_hi * 65536 + u_lo, jnp.int32(INT_MIN + 1))
    n_ge = jnp.where(u_hi > 0, n_gt + jnp.where(u_lo > 0, n_lo, n_mem), 0)

    def count32(pred):
        def count_chunk(c, cnt):
            hit = jnp.where(pred(key_ref[rows_of_step(c), :]), 1, 0)
            return cnt + jnp.sum(hit.reshape(step_rows // SUBLANES, SUBLANES, TILE), axis=0)

        cnt = lax.fori_loop(0, n_steps, count_chunk, jnp.zeros((SUBLANES, TILE), jnp.int32))
        return jnp.sum(cnt, axis=0, keepdims=True)

    has_ties = jnp.max(n_ge) > TOPK_MAX

    @pl.when(jnp.logical_not(has_ties))
    def _():
        def write_chunk(c, carry):
            mask_ref[0, 0, rows_of(c), :] = jnp.where(key_ref[rows_of(c), :] >= thr, 0.0, MASKED).astype(BF16)
            return carry

        lax.fori_loop(0, nch, write_chunk, 0)

    @pl.when(has_ties)
    def _():
        keep = (TOPK_MAX - count32(lambda key: key > thr)).astype(F32)
        below = jnp.where(lax.broadcasted_iota(jnp.int32, (TILE, TILE), 0)
                          >= lax.broadcasted_iota(jnp.int32, (TILE, TILE), 1), 1.0, 0.0).astype(BF16)

        def write_chunk(c, seen):
            key = key_ref[rows_of(c), :]
            tied = key == thr
            rank = jnp.dot(below, jnp.where(tied, 1.0, 0.0).astype(BF16), preferred_element_type=F32) + seen
            m = jnp.where(key > thr, 0.0, jnp.where(tied, jnp.where(rank <= keep, 0.0, MASKED), MASKED))
            mask_ref[0, 0, rows_of(c), :] = m.astype(BF16)
            return rank[TILE - 1:TILE, :]

        lax.fori_loop(0, nch, write_chunk, jnp.zeros((1, TILE), F32))

    def fill_chunk(c, carry):
        mask_ref[0, 0, rows_of(c), :] = jnp.full((TILE, TILE), MASKED, BF16)
        return carry

    lax.fori_loop(nch, n_tiles, fill_chunk, 0)


def _indexer(qt, wt, kw, batch, seq):
    nq = seq // TILE
    qi_block = 1
    vmem = (2 * IDX_HEADS * IDX_DIM * TILE * 2 + 2 * seq * LANES * 4 + 2 * seq * TILE * 2
            + seq * LANES * 2 + seq * TILE * (4 + 2 + 2 + 2) + 8 * TILE * TILE * 4)
    return pl.pallas_call(
        functools.partial(_indexer_body, seq),
        grid=(batch, nq),
        in_specs=[pl.BlockSpec((1, IDX_HEADS * IDX_DIM, TILE), lambda b, i: (b * nq + i, qi_block, 0)),
                  pl.BlockSpec((1, IDX_HEADS, TILE), lambda b, i: (b * nq + i, 0, 0)),
                  pl.BlockSpec((seq, LANES), lambda b, i: (b, 0))],
        out_specs=pl.BlockSpec((1, 1, seq, TILE), lambda b, i: (b, i, 0, 0)),
        out_shape=jax.ShapeDtypeStruct((batch, nq, seq, TILE), BF16),
        scratch_shapes=[pltpu.VMEM((seq, IDX_DIM), BF16),
                        pltpu.VMEM((seq, TILE), jnp.int32),
                        pltpu.VMEM((seq, TILE), I16),
                        pltpu.VMEM((seq, TILE), I16),
                        pltpu.VMEM((seq, TILE), I16),
                        pltpu.VMEM((SUBLANES, TILE), jnp.int32)],
        compiler_params=pltpu.CompilerParams(
            dimension_semantics=("arbitrary", "arbitrary"), vmem_limit_bytes=_vmem_limit(vmem)),
        name="indexer",
    )(qt, wt, kw)


def _bias_init(rb_ref, bias_ref):
    shape = (2 * TILE, TILE)
    dist = (lax.broadcasted_iota(jnp.int32, shape, 1) - lax.broadcasted_iota(jnp.int32, shape, 0) + TILE)
    max_exact = NUM_BUCKETS // 2
    n = jnp.maximum(dist, 0)
    nf = jnp.maximum(n, max_exact).astype(F32)
    large = max_exact + (jnp.log(nf / max_exact) / math.log(MAX_DISTANCE / max_exact)
                         * (NUM_BUCKETS - max_exact)).astype(jnp.int32)
    large = jnp.minimum(large, NUM_BUCKETS - 1)
    bucket = jnp.where(n < max_exact, n, large)
    for h in range(N_HEADS):
        far = rb_ref[NUM_BUCKETS - 1, h]
        acc = jnp.zeros(shape, F32)
        for b in range(NUM_BUCKETS - 1):
            acc = jnp.where(bucket == b, (rb_ref[b, h] - far) * LOG2E, acc)
        bias_ref[h] = acc


def _attn_body(rb_ref, q_ref, k_ref, vt_ref, mask_ref, o_ref, bias_ref, acc_ref, m_ref, l_ref, alpha_ref,
               p_ref):
    b = pl.program_id(0)
    i = pl.program_id(1)

    @pl.when((b == 0) & (i == 0))
    def _():
        _bias_init(rb_ref, bias_ref)

    acc_ref[...] = jnp.zeros_like(acc_ref)
    m_ref[...] = jnp.full_like(m_ref, MASKED)
    l_ref[...] = jnp.zeros_like(l_ref)

    eye = jnp.where(lax.broadcasted_iota(jnp.int32, (HEAD_DIM, HEAD_DIM), 0)
                    == lax.broadcasted_iota(jnp.int32, (HEAD_DIM, HEAD_DIM), 1), 1.0, 0.0).astype(BF16)

    def numerators(j, near, slot):
        halves = [pl.ds(pl.multiple_of(j * TILE + r * HEAD_DIM, HEAD_DIM), HEAD_DIM)
                  for r in range(TILE // HEAD_DIM)]
        masks = [mask_ref[0, 0, rows, :] for rows in halves]
        for h in range(N_HEADS):
            feat = slice(h * HEAD_DIM, (h + 1) * HEAD_DIM)
            s_parts = []
            for r, rows in enumerate(halves):
                lhs = jnp.concatenate([k_ref[rows, feat], eye], axis=1)
                rhs = jnp.concatenate([q_ref[0, feat, :], masks[r]], axis=0)
                s = jnp.dot(lhs, rhs, preferred_element_type=F32)
                if near is not None:
                    s = s + bias_ref[h, near * TILE + r * HEAD_DIM:near * TILE + (r + 1) * HEAD_DIM, :]
                s_parts.append(s)
            for c in range(TILE // LANES):
                lanes = slice(c * LANES, (c + 1) * LANES)
                sc = jnp.concatenate([s[:, lanes] for s in s_parts], axis=0)
                m_old = m_ref[h:h + 1, lanes]
                m_new = jnp.maximum(m_old, jnp.max(sc, axis=0, keepdims=True))
                p_ref[slot, h, :, lanes] = jnp.exp2(sc - m_new).astype(BF16)
                alpha_ref[slot, h:h + 1, lanes] = jnp.exp2(m_old - m_new)
                m_ref[h:h + 1, lanes] = m_new

    def values(j, slot):
        for h in range(N_HEADS):
            feat = slice(h * HEAD_DIM, (h + 1) * HEAD_DIM)
            pv = jnp.dot(vt_ref[j, h * VT_ROWS:(h + 1) * VT_ROWS, :], p_ref[slot, h], preferred_element_type=F32)
            alpha = alpha_ref[slot, h:h + 1, :]
            acc_ref[feat, :] = alpha * acc_ref[feat, :] + pv[:HEAD_DIM, :]
            l_ref[h:h + 1, :] = alpha * l_ref[h:h + 1, :] + pv[HEAD_DIM:HEAD_DIM + 1, :]

    def group(tiles):
        for slot, (j, near) in enumerate(tiles):
            numerators(j, near, slot)
        for slot, (j, near) in enumerate(tiles):
            values(j, slot)

    n_far = jnp.maximum(i - 1, 0)

    def far_group(t, carry):
        group([(ATT_GROUP * t + r, None) for r in range(ATT_GROUP)])
        return carry

    lax.fori_loop(0, n_far // ATT_GROUP, far_group, 0)
    rest = n_far % ATT_GROUP
    done = n_far - rest

    @pl.when(rest >= 2)
    def _():
        group([(done, None), (done + 1, None)])

    @pl.when(rest % 2 == 1)
    def _():
        group([(n_far - 1, None)])

    @pl.when(i >= 1)
    def _():
        group([(i - 1, 0), (i, 1)])

    @pl.when(i == 0)
    def _():
        group([(i, 1)])

    for h in range(N_HEADS):
        feat = slice(h * HEAD_DIM, (h + 1) * HEAD_DIM)
        o_ref[:, feat] = (acc_ref[feat, :] / l_ref[h:h + 1, :]).T.astype(o_ref.dtype)


def _attention(qt, k, vt, mask, rel_bias, batch, seq):
    nq = seq // TILE
    vmem = (2 * ATTN_DIM * TILE * 2 * 2 + seq * (ATTN_DIM + N_HEADS * VT_ROWS) * 2 + 2 * seq * TILE * 2
            + N_HEADS * 2 * TILE * TILE * 4 + ATTN_DIM * TILE * 4 + ATT_GROUP * N_HEADS * TILE * TILE * 2
            + 4 * TILE * TILE * 4)
    resident = pl.Buffered(1)
    return pl.pallas_call(
        _attn_body,
        grid=(batch, nq),
        in_specs=[pl.BlockSpec(memory_space=pltpu.SMEM),
                  pl.BlockSpec((1, ATTN_DIM, TILE), lambda b, i: (b * nq + i, 0, 0)),
                  pl.BlockSpec((seq, ATTN_DIM), lambda b, i: (b, 0), pipeline_mode=resident),
                  pl.BlockSpec((nq, N_HEADS * VT_ROWS, TILE), lambda b, i: (b, 0, 0), pipeline_mode=resident),
                  pl.BlockSpec((1, 1, seq, TILE), lambda b, i: (b, i, 0, 0))],
        out_specs=pl.BlockSpec((TILE, ATTN_DIM), lambda b, i: (b * nq + i, 0)),
        out_shape=jax.ShapeDtypeStruct((batch * seq, ATTN_DIM), BF16),
        scratch_shapes=[pltpu.VMEM((N_HEADS, 2 * TILE, TILE), F32),
                        pltpu.VMEM((ATTN_DIM, TILE), F32),
                        pltpu.VMEM((N_HEADS, TILE), F32),
                        pltpu.VMEM((N_HEADS, TILE), F32),
                        pltpu.VMEM((ATT_GROUP, N_HEADS, TILE), F32),
                        pltpu.VMEM((ATT_GROUP, N_HEADS, TILE, TILE), BF16)],
        compiler_params=pltpu.CompilerParams(
            dimension_semantics=("arbitrary", "arbitrary"), vmem_limit_bytes=_vmem_limit(vmem)),
        name="attention",
    )(rel_bias, qt, k, vt, mask)


def _mix_body(tiles_per_seq, cur_ref, halo_ref, dw_ref, dwb_ref, lng_ref, lnb_ref, attn_ref, gates_ref, hres_ref,
              wo_ref, wpw_ref, wout_ref, g_ref, o_ref, h_ref, y_ref, conv_ref):
    s = pl.program_id(0)
    last = pl.num_programs(0) - 1

    def glu(x):
        return x[:, :CONV_CH].astype(F32) * jax.nn.sigmoid(x[:, CONV_CH:].astype(F32))

    def conv(slot):
        h_ref[0, :CONV_HALO, :] = jnp.where(s % tiles_per_seq != 0, glu(halo_ref[...]), 0.0)
        h_ref[0, CONV_HALO:, :] = glu(cur_ref[...])
        n_rows = MIX_TM + CONV_HALO
        for r in range(1, SUBLANES):
            h_ref[r, SUBLANES:, :] = h_ref[0, SUBLANES - r:n_rows - r, :]
        for rb in range(MIX_TM // CONV_RB):
            for c in range(CONV_CH // CONV_CB):
                cols = slice(c * CONV_CB, (c + 1) * CONV_CB)
                acc = jnp.broadcast_to(dwb_ref[:, cols], (CONV_RB, CONV_CB))
                for j in range(CONV_WIDTH):
                    groups, r = divmod(CONV_WIDTH - 1 - j, SUBLANES)
                    start = CONV_HALO - groups * SUBLANES + rb * CONV_RB
                    acc = acc + h_ref[r, start:start + CONV_RB, cols] * dw_ref[j:j + 1, cols]
                y_ref[rb * CONV_RB:(rb + 1) * CONV_RB, cols] = acc
        y = y_ref[...]
        mu = jnp.mean(y, axis=-1, keepdims=True)
        var = jnp.mean(jnp.square(y - mu), axis=-1, keepdims=True)
        z = (y - mu) * lax.rsqrt(var + EPS) * lng_ref[...] + lnb_ref[...]
        conv_ref[slot] = (z * jax.nn.sigmoid(z)).astype(BF16)

    def mix(slot):
        a = jnp.dot(attn_ref[...], wo_ref[...], preferred_element_type=F32)
        c = jnp.dot(conv_ref[slot], wpw_ref[...], preferred_element_type=F32)
        mixed = (gates_ref[:, :D_MODEL].astype(F32) * a + gates_ref[:, D_MODEL:].astype(F32) * c).astype(BF16)
        y = jnp.dot(mixed, wout_ref[...], preferred_element_type=F32)
        o_ref[...] = hres_ref[...] + _rms(y, g_ref[...])

    @pl.when(s == 0)
    def _():
        conv(0)

    @pl.when((s > 0) & (s < last))
    def _():
        mix((s + 1) % 2)
        conv(s % 2)

    @pl.when(s == last)
    def _():
        mix((s + 1) % 2)


def _mix_out(glu_in, attn, gates, h, dw, dw_b, ln_g, ln_b, w_o, w_pw2, w_out, post_g, seq):
    t = h.shape[0]
    nt = t // MIX_TM
    halo_per_tile = MIX_TM // CONV_HALO
    resident = pl.Buffered(1)

    def ahead(n):
        return pl.BlockSpec((MIX_TM, n), lambda s: (jnp.minimum(s, nt - 1), 0))

    def behind(n):
        return pl.BlockSpec((MIX_TM, n), lambda s: (jnp.maximum(s - 1, 0), 0))

    def whole(a):
        return pl.BlockSpec(a.shape, lambda s: (0, 0), pipeline_mode=resident)

    vec = pl.BlockSpec((1, CONV_CH), lambda s: (0, 0))
    vmem = ((w_o.size + w_pw2.size + w_out.size) * 2
            + 2 * MIX_TM * (2 * CONV_CH * 2 + ATTN_DIM * 2 + 2 * D_MODEL * 2 + 2 * D_MODEL * 4)
            + SUBLANES * (MIX_TM + CONV_HALO) * CONV_CH * 4 + MIX_TM * CONV_CH * (4 + 2 * 2)
            + 6 * MIX_TM * D_MODEL * 4)
    return pl.pallas_call(
        functools.partial(_mix_body, seq // MIX_TM),
        grid=(nt + 1,),
        in_specs=[ahead(2 * CONV_CH),
                  pl.BlockSpec((CONV_HALO, 2 * CONV_CH),
                               lambda s: (jnp.maximum(jnp.minimum(s, nt - 1) * halo_per_tile - 1, 0), 0)),
                  pl.BlockSpec((CONV_WIDTH, CONV_CH), lambda s: (0, 0)), vec, vec, vec,
                  behind(ATTN_DIM), behind(2 * D_MODEL), behind(D_MODEL),
                  whole(w_o), whole(w_pw2), whole(w_out), pl.BlockSpec((1, D_MODEL), lambda s: (0, 0))],
        out_specs=behind(D_MODEL),
        out_shape=jax.ShapeDtypeStruct((t, D_MODEL), F32),
        scratch_shapes=[pltpu.VMEM((SUBLANES, MIX_TM + CONV_HALO, CONV_CH), F32),
                        pltpu.VMEM((MIX_TM, CONV_CH), F32),
                        pltpu.VMEM((2, MIX_TM, CONV_CH), BF16)],
        compiler_params=pltpu.CompilerParams(
            dimension_semantics=("arbitrary",), vmem_limit_bytes=_vmem_limit(vmem)),
        name="mix_out",
    )(glu_in, glu_in, dw, dw_b, ln_g, ln_b, attn, gates, h, w_o, w_pw2, w_out, post_g)


def kernel(x, rel_bias, ffn1_pre_g, ffn1_wg, ffn1_wu, ffn1_wd, ffn1_post_g, mix_pre_g, w_in, b_gate, w_o,
           conv_dw, conv_dw_b, conv_ln_g, conv_ln_b, w_pw2, w_out, mix_post_g, ffn2_pre_g, ffn2_wg, ffn2_wu,
           ffn2_wd, ffn2_post_g):
    batch, seq, d = x.shape
    depth = ffn1_wg.shape[0]
    h = x.reshape(batch * seq, d)
    o_k = ATTN_DIM
    o_v = o_k + ATTN_DIM
    o_qi = o_v + ATTN_DIM
    o_ki = o_qi + IDX_HEADS * IDX_DIM
    o_wi = o_ki + IDX_DIM
    o_glu = o_wi + IDX_HEADS
    o_gates = o_glu + 2 * CONV_CH
    for l in range(depth):
        wl = w_in[l]
        w_t = jnp.concatenate([_weight_t(wl, 0, o_k, HEAD_DIM ** -0.5 * LOG2E),
                               _weight_t(wl, o_qi, o_ki - o_qi)], axis=0)
        w_vt = _weight_t(wl, o_v, o_qi - o_v)
        w_wt = wl[:, o_wi:o_glu].T.astype(BF16)
        w_k = wl[:, o_k:o_v].astype(BF16)
        w_ki = jnp.pad(wl[:, o_ki:o_wi], ((0, 0), (0, LANES - IDX_DIM))).astype(BF16)
        w_glu = wl[:, o_glu:o_gates].astype(BF16)
        w_gates = wl[:, o_gates:].astype(BF16)

        h, u = _ffn(h, ffn1_pre_g[l][None], ffn1_wg[l].astype(BF16), ffn1_wu[l].astype(BF16),
                    ffn1_wd[l].astype(BF16), ffn1_post_g[l][None], mix_pre_g[l][None])

        qt, vt, wt = _proj_t(u, w_t, w_vt, w_wt)
        k = _proj(u, w_k, BF16, "proj_k")
        kw = _proj(u, w_ki, F32, "proj_ki")
        glu_in = _proj(u, w_glu, BF16, "proj_glu")
        gates = _proj(u, w_gates, BF16, "proj_gates", bias=b_gate[l][None])

        mask = _indexer(qt, wt, kw, batch, seq)
        attn = _attention(qt, k, vt, mask, rel_bias, batch, seq)
        h = _mix_out(glu_in, attn, gates, h, conv_dw[l], conv_dw_b[l][None], conv_ln_g[l][None],
                     conv_ln_b[l][None], w_o[l].astype(BF16), w_pw2[l].astype(BF16), w_out[l].astype(BF16),
                     mix_post_g[l][None], seq)

        h = _ffn(h, ffn2_pre_g[l][None], ffn2_wg[l].astype(BF16), ffn2_wu[l].astype(BF16),
                 ffn2_wd[l].astype(BF16), ffn2_post_g[l][None])
    return h.reshape(batch, seq, d)
```

```python
import functools
import math

import jax
import jax.numpy as jnp
from jax import lax
from jax.experimental import pallas as pl
from jax.experimental.pallas import tpu as pltpu

D_MODEL = 2048
N_HEADS = 8
HEAD_DIM = 128
ATTN_DIM = N_HEADS * HEAD_DIM
IDX_HEADS = 16
IDX_DIM = 64
TOPK_MAX = 256
NUM_BUCKETS = 32
MAX_DISTANCE = 128
CONV_CH = 1024
CONV_WIDTH = 31
D_FF = 5632
FFN_RES_WEIGHT = 0.5
EPS = 1e-6

F32 = jnp.float32
BF16 = jnp.bfloat16
LANES = 128
SUBLANES = 8
V7X_VMEM_BYTES = 64 * 1024 * 1024
MASKED = -1e30
INT_MIN = -(2 ** 31)
I16 = jnp.int16
I16_MIN = -(2 ** 15)
PACKED_ROWS = 16
LOG2E = math.log2(math.e)

FFN_TM, FFN_TF = 512, 512
PROJ_TM, PROJ_TN = 1024, 1024
TILE = 256
PROJT_ROWS = 512
WT_COLS = 512
ATT_GROUP = 4
COUNT_TILES = 4
MEMBER_TOPS = 6
VT_ROWS = HEAD_DIM + PACKED_ROWS
CONV_HALO = 32
CONV_RB, CONV_CB = 128, 256
MIX_TM = 256

_NT = (((1,), (1,)), ((), ()))


def _vmem_limit(nbytes):
    return int(min(nbytes + (8 << 20), V7X_VMEM_BYTES - (4 << 20)))


def _rms(x, g):
    y = x * lax.rsqrt(jnp.mean(x * x, axis=-1, keepdims=True) + EPS)
    return y * g


def _ffn_body(emit_next, x_ref, pre_g_ref, wg_ref, wu_ref, wd_ref, post_g_ref, *rest):
    if emit_next:
        next_g_ref, out_ref, u_ref, xn_ref, acc_ref = rest
    else:
        out_ref, xn_ref, acc_ref = rest
    j = pl.program_id(1)

    @pl.when(j == 0)
    def _():
        xn_ref[...] = _rms(x_ref[...], pre_g_ref[...]).astype(BF16)
        acc_ref[...] = jnp.zeros_like(acc_ref)

    xn = xn_ref[...]
    g = jnp.dot(xn, wg_ref[...], preferred_element_type=F32)
    u = jnp.dot(xn, wu_ref[...], preferred_element_type=F32)
    a = (g * jax.nn.sigmoid(g) * u).astype(BF16)
    acc_ref[...] += jnp.dot(a, wd_ref[...], preferred_element_type=F32)

    @pl.when(j == pl.num_programs(1) - 1)
    def _():
        h = x_ref[...] + FFN_RES_WEIGHT * _rms(acc_ref[...], post_g_ref[...])
        out_ref[...] = h
        if emit_next:
            u_ref[...] = _rms(h, next_g_ref[...]).astype(BF16)


def _ffn(x, pre_g, wg, wu, wd, post_g, next_g=None):
    t, d = x.shape
    dff = wg.shape[1]
    emit_next = next_g is not None
    row = pl.BlockSpec((FFN_TM, d), lambda i, j: (i, 0))
    gain = pl.BlockSpec((1, d), lambda i, j: (0, 0))
    in_specs = [row, gain,
                pl.BlockSpec((d, FFN_TF), lambda i, j: (0, j)),
                pl.BlockSpec((d, FFN_TF), lambda i, j: (0, j)),
                pl.BlockSpec((FFN_TF, d), lambda i, j: (j, 0)),
                gain]
    args = [x, pre_g, wg, wu, wd, post_g]
    out_shape = [jax.ShapeDtypeStruct((t, d), F32)]
    out_specs = [row]
    if emit_next:
        in_specs.append(gain)
        args.append(next_g)
        out_shape.append(jax.ShapeDtypeStruct((t, d), BF16))
        out_specs.append(row)
    vmem = (2 * FFN_TM * d * 4 * 2 + 2 * FFN_TM * d * 2 + FFN_TM * d * (2 + 4)
            + 2 * 3 * d * FFN_TF * 2 + 3 * FFN_TM * FFN_TF * 4)
    res = pl.pallas_call(
        functools.partial(_ffn_body, emit_next),
        grid=(t // FFN_TM, dff // FFN_TF),
        in_specs=in_specs, out_specs=out_specs, out_shape=out_shape,
        scratch_shapes=[pltpu.VMEM((FFN_TM, d), BF16), pltpu.VMEM((FFN_TM, d), F32)],
        compiler_params=pltpu.CompilerParams(
            dimension_semantics=("arbitrary", "arbitrary"), vmem_limit_bytes=_vmem_limit(vmem)),
        name="ffn_next" if emit_next else "ffn",
    )(*args)
    return res if emit_next else res[0]


def _proj_body(gate, x_ref, w_ref, *rest):
    if gate:
        b_ref, o_ref = rest
    else:
        (o_ref,) = rest
    y = jnp.dot(x_ref[...], w_ref[...], preferred_element_type=F32)
    if gate:
        y = jax.nn.sigmoid(y + b_ref[...])
    o_ref[...] = y.astype(o_ref.dtype)


def _proj(x, w, out_dtype, name, bias=None):
    t, k = x.shape
    n = w.shape[1]
    tn = PROJ_TN if n % PROJ_TN == 0 else n
    gate = bias is not None
    in_specs = [pl.BlockSpec((PROJ_TM, k), lambda i, j: (i, 0)),
                pl.BlockSpec((k, tn), lambda i, j: (0, j))]
    args = [x, w]
    if gate:
        in_specs.append(pl.BlockSpec((1, tn), lambda i, j: (0, j)))
        args.append(bias)
    vmem = 2 * PROJ_TM * k * 2 + 2 * k * tn * 2 + 3 * PROJ_TM * tn * 4
    return pl.pallas_call(
        functools.partial(_proj_body, gate),
        grid=(t // PROJ_TM, n // tn),
        in_specs=in_specs,
        out_specs=pl.BlockSpec((PROJ_TM, tn), lambda i, j: (i, j)),
        out_shape=jax.ShapeDtypeStruct((t, n), out_dtype),
        compiler_params=pltpu.CompilerParams(
            dimension_semantics=("arbitrary", "arbitrary"), vmem_limit_bytes=_vmem_limit(vmem)),
        name=name,
    )(*args)


def _weight_t_body(scale, w_ref, o_ref):
    o_ref[...] = (w_ref[...] * scale).T.astype(BF16)


def _weight_t(w, col0, ncols, scale=1.0):
    k = w.shape[0]
    assert col0 % WT_COLS == 0 and ncols % WT_COLS == 0
    vmem = 2 * k * WT_COLS * (4 + 2) + 2 * k * WT_COLS * 4
    return pl.pallas_call(
        functools.partial(_weight_t_body, scale),
        grid=(ncols // WT_COLS,),
        in_specs=[pl.BlockSpec((k, WT_COLS), lambda j: (0, col0 // WT_COLS + j))],
        out_specs=pl.BlockSpec((WT_COLS, k), lambda j: (j, 0)),
        out_shape=jax.ShapeDtypeStruct((ncols, k), BF16),
        compiler_params=pltpu.CompilerParams(
            dimension_semantics=("arbitrary",), vmem_limit_bytes=_vmem_limit(vmem)),
        name="weight_t",
    )(w)


def _proj_t_body(u_ref, wt_ref, wvt_ref, wwt_ref, o_ref, v_ref, w_ref):
    u = u_ref[...]
    for r in range(wt_ref.shape[0] // PROJT_ROWS):
        rows = slice(r * PROJT_ROWS, (r + 1) * PROJT_ROWS)
        o_ref[0, rows, :] = lax.dot_general(wt_ref[rows, :], u, _NT, preferred_element_type=F32).astype(BF16)
    heads_per_dot = PROJT_ROWS // HEAD_DIM
    for r in range(wvt_ref.shape[0] // PROJT_ROWS):
        vt = lax.dot_general(wvt_ref[r * PROJT_ROWS:(r + 1) * PROJT_ROWS, :], u, _NT,
                             preferred_element_type=F32).astype(BF16)
        for hh in range(heads_per_dot):
            h = r * heads_per_dot + hh
            v_ref[0, h * VT_ROWS:h * VT_ROWS + HEAD_DIM, :] = vt[hh * HEAD_DIM:(hh + 1) * HEAD_DIM, :]
            v_ref[0, h * VT_ROWS + HEAD_DIM:(h + 1) * VT_ROWS, :] = jnp.ones((VT_ROWS - HEAD_DIM, TILE), BF16)
    w_ref[0] = lax.dot_general(wwt_ref[...], u, _NT, preferred_element_type=F32)


def _proj_t(u, wt, wvt, wwt):
    t, k = u.shape
    n, nw = wt.shape[0], wwt.shape[0]
    nv = N_HEADS * VT_ROWS
    resident = pl.Buffered(1)
    vmem = (2 * TILE * k * 2 + (n + wvt.shape[0] + nw) * k * 2 + 2 * ((n + nv) * 2 + nw * 4) * TILE
            + 2 * PROJT_ROWS * TILE * 4)
    return pl.pallas_call(
        _proj_t_body,
        grid=(t // TILE,),
        in_specs=[pl.BlockSpec((TILE, k), lambda i: (i, 0)),
                  pl.BlockSpec((n, k), lambda i: (0, 0), pipeline_mode=resident),
                  pl.BlockSpec(wvt.shape, lambda i: (0, 0), pipeline_mode=resident),
                  pl.BlockSpec((nw, k), lambda i: (0, 0), pipeline_mode=resident)],
        out_specs=[pl.BlockSpec((1, n, TILE), lambda i: (i, 0, 0)),
                   pl.BlockSpec((1, nv, TILE), lambda i: (i, 0, 0)),
                   pl.BlockSpec((1, nw, TILE), lambda i: (i, 0, 0))],
        out_shape=[jax.ShapeDtypeStruct((t // TILE, n, TILE), BF16),
                   jax.ShapeDtypeStruct((t // TILE, nv, TILE), BF16),
                   jax.ShapeDtypeStruct((t // TILE, nw, TILE), F32)],
        compiler_params=pltpu.CompilerParams(
            dimension_semantics=("arbitrary",), vmem_limit_bytes=_vmem_limit(vmem)),
        name="proj_t",
    )(u, wt, wvt, wwt)


def _indexer_body(seq, qi_ref, w_ref, kw_ref, mask_ref, kb_ref, key_ref, hi_ref, lo_ref, mem_ref, low_ref):
    i = pl.program_id(1)
    nch = i + 1
    n_tiles = seq // TILE

    @pl.when(i == 0)
    def _():
        kb_ref[...] = kw_ref[:, :IDX_DIM].astype(BF16)

    w = w_ref[0] * (IDX_HEADS ** -0.5 * IDX_DIM ** -0.5)
    q_pos = i * TILE + lax.broadcasted_iota(jnp.int32, (1, TILE), 1)

    def rows_of(c):
        return pl.ds(pl.multiple_of(c * TILE, TILE), TILE)

    def score_tile(c):
        ks = kb_ref[rows_of(c), :]
        acc = jnp.zeros((TILE, TILE), F32)
        for h in range(IDX_HEADS):
            d = jnp.dot(ks, qi_ref[0, h * IDX_DIM:(h + 1) * IDX_DIM, :], preferred_element_type=F32)
            acc = acc + jnp.maximum(d, 0.0) * w[h:h + 1, :]
        bits = pltpu.bitcast(acc, jnp.int32)
        key = bits ^ ((bits >> 31) & jnp.int32(0x7FFFFFFF))
        k_pos = c * TILE + lax.broadcasted_iota(jnp.int32, (TILE, 1), 0)
        key = jnp.where(k_pos <= q_pos, key, jnp.int32(INT_MIN))
        key_ref[rows_of(c), :] = key
        hi_ref[rows_of(c), :] = (key >> 16).astype(I16)
        lo_ref[rows_of(c), :] = ((key & jnp.int32(0xFFFF)) + I16_MIN).astype(I16)

    def score_pair(t, carry):
        for r in range(4):
            score_tile(4 * t + r)
        return carry

    lax.fori_loop(0, nch // 4, score_pair, 0)
    scored = nch - nch % 4

    @pl.when(nch % 4 >= 2)
    def _():
        score_tile(scored)
        score_tile(scored + 1)

    @pl.when(nch % 2 == 1)
    def _():
        score_tile(nch - 1)

    n_steps = (nch + COUNT_TILES - 1) // COUNT_TILES
    step_rows = COUNT_TILES * TILE

    def rows_of_step(c):
        return pl.ds(pl.multiple_of(c * step_rows, step_rows), step_rows)

    def pad_chunk(c, carry):
        key_ref[rows_of(c), :] = jnp.full((TILE, TILE), INT_MIN, jnp.int32)
        hi_ref[rows_of(c), :] = jnp.full((TILE, TILE), I16_MIN, I16)
        lo_ref[rows_of(c), :] = jnp.full((TILE, TILE), I16_MIN, I16)
        return carry

    lax.fori_loop(nch, n_steps * COUNT_TILES, pad_chunk, 0)

    def packed(row):
        return jnp.broadcast_to(row, (PACKED_ROWS, TILE)).astype(I16)

    tile_groups = TILE // PACKED_ROWS

    def load16(ref, tile):
        return ref[rows_of(tile), :].reshape(tile_groups, PACKED_ROWS, TILE)

    def sum16(hits):
        parts = [None] * 4
        for hit in hits:
            ones = jnp.where(hit, jnp.int16(1), jnp.int16(0))
            for g in range(tile_groups):
                parts[g % 4] = ones[g] if parts[g % 4] is None else parts[g % 4] + ones[g]
        return (parts[0] + parts[1]) + (parts[2] + parts[3])

    def count16(c, hit_of_tile):
        return sum16(hit_of_tile(c * COUNT_TILES + t) for t in range(COUNT_TILES))

    def total16(cnt):
        return jnp.sum(cnt.astype(jnp.int32), axis=0, keepdims=True)

    def search16(src_ref, need):
        def search_pass(b, carry):
            u, n_u = carry
            cand_u = u | lax.shift_left(jnp.int32(1), 15 - b)
            cand = packed(cand_u + I16_MIN)
            cnt = lax.fori_loop(
                0, n_steps, lambda c, cnt: cnt + count16(c, lambda tile: load16(src_ref, tile) >= cand[None]),
                jnp.zeros((PACKED_ROWS, TILE), I16))
            total = total16(cnt)
            ok = total >= need
            return jnp.where(ok, cand_u, u), jnp.where(ok, total, n_u)

        zero = jnp.zeros((1, TILE), jnp.int32)
        return lax.fori_loop(0, 16, search_pass, (zero, zero))

    u_hi, _ = search16(hi_ref, jnp.int32(TOPK_MAX))
    t_hi = u_hi + I16_MIN
    t_hi16 = packed(t_hi)

    def member_chunk(c, carry):
        n_gt, n_mem, tops = carry
        tops = list(tops)
        hits = []
        for t in range(COUNT_TILES):
            tile = c * COUNT_TILES + t
            member = load16(hi_ref, tile) == t_hi16[None]
            vals = jnp.where(member, load16(lo_ref, tile), jnp.int16(I16_MIN))
            mem_ref[rows_of(tile), :] = vals.reshape(TILE, TILE)
            hits.append(member)
            for g in range(tile_groups):
                x = vals[g]
                for k in range(MEMBER_TOPS):
                    keep = tops[k] >= x
                    tops[k], x = jnp.where(keep, tops[k], x), jnp.where(keep, x, tops[k])
        n_mem = n_mem + sum16(hits)
        return n_gt + count16(c, lambda tile: load16(hi_ref, tile) > t_hi16[None]), n_mem, tuple(tops)

    zero16 = jnp.zeros((PACKED_ROWS, TILE), I16)
    floor16 = jnp.full((PACKED_ROWS, TILE), I16_MIN, I16)
    n_gt, n_mem, tops = lax.fori_loop(0, n_steps, member_chunk, (zero16, zero16, (floor16,) * MEMBER_TOPS))
    n_gt, n_mem = total16(n_gt), total16(n_mem)
    need_lo = TOPK_MAX - n_gt

    def kept_pass(b, carry):
        u, n_u = carry
        cand_u = u | lax.shift_left(jnp.int32(1), 15 - b)
        cand = packed(cand_u + I16_MIN)
        cnt = zero16
        for top in tops:
            cnt = cnt + jnp.where(top >= cand, jnp.int16(1), jnp.int16(0))
        ok = total16(cnt) >= need_lo
        return jnp.where(ok, cand_u, u), n_u

    zero = jnp.zeros((1, TILE), jnp.int32)
    u_kept, _ = lax.fori_loop(0, 16, kept_pass, (zero, zero))
    cand_kept = packed(u_kept + I16_MIN)
    n_at, n_above = lax.fori_loop(
        0, n_steps,
        lambda c, cnt: (cnt[0] + count16(c, lambda tile: load16(mem_ref, tile) >= cand_kept[None]),
                        cnt[1] + count16(c, lambda tile: load16(mem_ref, tile) > cand_kept[None])),
        (zero16, zero16))
    n_at, n_above = total16(n_at), total16(n_above)
    low_ref[0:1, :] = u_kept
    low_ref[1:2, :] = n_at

    @pl.when(jnp.max(jnp.where(n_above >= need_lo, 1, 0)) > 0)
    def _():
        u_full, n_full = search16(mem_ref, need_lo)
        low_ref[0:1, :] = u_full
        low_ref[1:2, :] = n_full

    u_lo, n_lo = low_ref[0:1, :], low_ref[1:2, :]
    thr = jnp.maximum(t_hi * 65536 + u_lo, jnp.int32(INT_MIN + 1))
    n_ge = jnp.where(u_hi > 0, n_gt + jnp.where(u_lo > 0, n_lo, n_mem), 0)

    def count32(pred):
        def count_chunk(c, cnt):
            hit = jnp.where(pred(key_ref[rows_of_step(c), :]), 1, 0)
            return cnt + jnp.sum(hit.reshape(step_rows // SUBLANES, SUBLANES, TILE), axis=0)

        cnt = lax.fori_loop(0, n_steps, count_chunk, jnp.zeros((SUBLANES, TILE), jnp.int32))
        return jnp.sum(cnt, axis=0, keepdims=True)

    has_ties = jnp.max(n_ge) > TOPK_MAX

    @pl.when(jnp.logical_not(has_ties))
    def _():
        def write_chunk(c, carry):
            mask_ref[0, 0, rows_of(c), :] = jnp.where(key_ref[rows_of(c), :] >= thr, 0.0, MASKED).astype(BF16)
            return carry

        lax.fori_loop(0, nch, write_chunk, 0)

    @pl.when(has_ties)
    def _():
        keep = (TOPK_MAX - count32(lambda key: key > thr)).astype(F32)
        below = jnp.where(lax.broadcasted_iota(jnp.int32, (TILE, TILE), 0)
                          >= lax.broadcasted_iota(jnp.int32, (TILE, TILE), 1), 1.0, 0.0).astype(BF16)

        def write_chunk(c, seen):
            key = key_ref[rows_of(c), :]
            tied = key == thr
            rank = jnp.dot(below, jnp.where(tied, 1.0, 0.0).astype(BF16), preferred_element_type=F32) + seen
            m = jnp.where(key > thr, 0.0, jnp.where(tied, jnp.where(rank <= keep, 0.0, MASKED), MASKED))
            mask_ref[0, 0, rows_of(c), :] = m.astype(BF16)
            return rank[TILE - 1:TILE, :]

        lax.fori_loop(0, nch, write_chunk, jnp.zeros((1, TILE), F32))

    def fill_chunk(c, carry):
        mask_ref[0, 0, rows_of(c), :] = jnp.full((TILE, TILE), MASKED, BF16)
        return carry

    lax.fori_loop(nch, n_tiles, fill_chunk, 0)


def _indexer(qt, wt, kw, batch, seq):
    nq = seq // TILE
    qi_block = 1
    vmem = (2 * IDX_HEADS * IDX_DIM * TILE * 2 + 2 * seq * LANES * 4 + 2 * seq * TILE * 2
            + seq * LANES * 2 + seq * TILE * (4 + 2 + 2 + 2) + 8 * TILE * TILE * 4)
    return pl.pallas_call(
        functools.partial(_indexer_body, seq),
        grid=(batch, nq),
        in_specs=[pl.BlockSpec((1, IDX_HEADS * IDX_DIM, TILE), lambda b, i: (b * nq + i, qi_block, 0)),
                  pl.BlockSpec((1, IDX_HEADS, TILE), lambda b, i: (b * nq + i, 0, 0)),
                  pl.BlockSpec((seq, LANES), lambda b, i: (b, kw.shape[1] // LANES - 1))],
        out_specs=pl.BlockSpec((1, 1, seq, TILE), lambda b, i: (b, i, 0, 0)),
        out_shape=jax.ShapeDtypeStruct((batch, nq, seq, TILE), BF16),
        scratch_shapes=[pltpu.VMEM((seq, IDX_DIM), BF16),
                        pltpu.VMEM((seq, TILE), jnp.int32),
                        pltpu.VMEM((seq, TILE), I16),
                        pltpu.VMEM((seq, TILE), I16),
                        pltpu.VMEM((seq, TILE), I16),
                        pltpu.VMEM((SUBLANES, TILE), jnp.int32)],
        compiler_params=pltpu.CompilerParams(
            dimension_semantics=("arbitrary", "arbitrary"), vmem_limit_bytes=_vmem_limit(vmem)),
        name="indexer",
    )(qt, wt, kw)


def _bias_init(rb_ref, bias_ref):
    shape = (2 * TILE, TILE)
    dist = (lax.broadcasted_iota(jnp.int32, shape, 1) - lax.broadcasted_iota(jnp.int32, shape, 0) + TILE)
    max_exact = NUM_BUCKETS // 2
    n = jnp.maximum(dist, 0)
    nf = jnp.maximum(n, max_exact).astype(F32)
    large = max_exact + (jnp.log(nf / max_exact) / math.log(MAX_DISTANCE / max_exact)
                         * (NUM_BUCKETS - max_exact)).astype(jnp.int32)
    large = jnp.minimum(large, NUM_BUCKETS - 1)
    bucket = jnp.where(n < max_exact, n, large)
    for h in range(N_HEADS):
        far = rb_ref[NUM_BUCKETS - 1, h]
        acc = jnp.zeros(shape, F32)
        for b in range(NUM_BUCKETS - 1):
            acc = jnp.where(bucket == b, (rb_ref[b, h] - far) * LOG2E, acc)
        bias_ref[h] = acc


def _attn_body(rb_ref, q_ref, k_ref, vt_ref, mask_ref, o_ref, bias_ref, acc_ref, m_ref, l_ref, alpha_ref,
               p_ref):
    b = pl.program_id(0)
    i = pl.program_id(1)

    @pl.when((b == 0) & (i == 0))
    def _():
        _bias_init(rb_ref, bias_ref)

    acc_ref[...] = jnp.zeros_like(acc_ref)
    m_ref[...] = jnp.full_like(m_ref, MASKED)
    l_ref[...] = jnp.zeros_like(l_ref)

    eye = jnp.where(lax.broadcasted_iota(jnp.int32, (HEAD_DIM, HEAD_DIM), 0)
                    == lax.broadcasted_iota(jnp.int32, (HEAD_DIM, HEAD_DIM), 1), 1.0, 0.0).astype(BF16)

    def numerators(j, near, slot):
        halves = [pl.ds(pl.multiple_of(j * TILE + r * HEAD_DIM, HEAD_DIM), HEAD_DIM)
                  for r in range(TILE // HEAD_DIM)]
        masks = [mask_ref[0, 0, rows, :] for rows in halves]
        for h in range(N_HEADS):
            feat = slice(h * HEAD_DIM, (h + 1) * HEAD_DIM)
            s_parts = []
            for r, rows in enumerate(halves):
                lhs = jnp.concatenate([k_ref[rows, feat], eye], axis=1)
                rhs = jnp.concatenate([q_ref[0, feat, :], masks[r]], axis=0)
                s = jnp.dot(lhs, rhs, preferred_element_type=F32)
                if near is not None:
                    s = s + bias_ref[h, near * TILE + r * HEAD_DIM:near * TILE + (r + 1) * HEAD_DIM, :]
                s_parts.append(s)
            for c in range(TILE // LANES):
                lanes = slice(c * LANES, (c + 1) * LANES)
                sc = jnp.concatenate([s[:, lanes] for s in s_parts], axis=0)
                m_old = m_ref[h:h + 1, lanes]
                m_new = jnp.maximum(m_old, jnp.max(sc, axis=0, keepdims=True))
                p_ref[slot, h, :, lanes] = jnp.exp2(sc - m_new).astype(BF16)
                alpha_ref[slot, h:h + 1, lanes] = jnp.exp2(m_old - m_new)
                m_ref[h:h + 1, lanes] = m_new

    def values(j, slot):
        for h in range(N_HEADS):
            feat = slice(h * HEAD_DIM, (h + 1) * HEAD_DIM)
            pv = jnp.dot(vt_ref[j, h * VT_ROWS:(h + 1) * VT_ROWS, :], p_ref[slot, h], preferred_element_type=F32)
            alpha = alpha_ref[slot, h:h + 1, :]
            acc_ref[feat, :] = alpha * acc_ref[feat, :] + pv[:HEAD_DIM, :]
            l_ref[h:h + 1, :] = alpha * l_ref[h:h + 1, :] + pv[HEAD_DIM:HEAD_DIM + 1, :]

    def group(tiles):
        for slot, (j, near) in enumerate(tiles):
            numerators(j, near, slot)
        for slot, (j, near) in enumerate(tiles):
            values(j, slot)

    n_far = jnp.maximum(i - 1, 0)

    def far_group(t, carry):
        group([(ATT_GROUP * t + r, None) for r in range(ATT_GROUP)])
        return carry

    lax.fori_loop(0, n_far // ATT_GROUP, far_group, 0)
    rest = n_far % ATT_GROUP
    done = n_far - rest

    @pl.when(rest >= 2)
    def _():
        group([(done, None), (done + 1, None)])

    @pl.when(rest % 2 == 1)
    def _():
        group([(n_far - 1, None)])

    @pl.when(i >= 1)
    def _():
        group([(i - 1, 0), (i, 1)])

    @pl.when(i == 0)
    def _():
        group([(i, 1)])

    for h in range(N_HEADS):
        feat = slice(h * HEAD_DIM, (h + 1) * HEAD_DIM)
        o_ref[:, feat] = (acc_ref[feat, :] / l_ref[h:h + 1, :]).T.astype(o_ref.dtype)


def _attention(qt, k, vt, mask, rel_bias, batch, seq):
    nq = seq // TILE
    vmem = (2 * ATTN_DIM * TILE * 2 * 2 + seq * (ATTN_DIM + N_HEADS * VT_ROWS) * 2 + 2 * seq * TILE * 2
            + N_HEADS * 2 * TILE * TILE * 4 + ATTN_DIM * TILE * 4 + ATT_GROUP * N_HEADS * TILE * TILE * 2
            + 4 * TILE * TILE * 4)
    resident = pl.Buffered(1)
    return pl.pallas_call(
        _attn_body,
        grid=(batch, nq),
        in_specs=[pl.BlockSpec(memory_space=pltpu.SMEM),
                  pl.BlockSpec((1, ATTN_DIM, TILE), lambda b, i: (b * nq + i, 0, 0)),
                  pl.BlockSpec((seq, ATTN_DIM), lambda b, i: (b, 0), pipeline_mode=resident),
                  pl.BlockSpec((nq, N_HEADS * VT_ROWS, TILE), lambda b, i: (b, 0, 0), pipeline_mode=resident),
                  pl.BlockSpec((1, 1, seq, TILE), lambda b, i: (b, i, 0, 0))],
        out_specs=pl.BlockSpec((TILE, ATTN_DIM), lambda b, i: (b * nq + i, 0)),
        out_shape=jax.ShapeDtypeStruct((batch * seq, ATTN_DIM), BF16),
        scratch_shapes=[pltpu.VMEM((N_HEADS, 2 * TILE, TILE), F32),
                        pltpu.VMEM((ATTN_DIM, TILE), F32),
                        pltpu.VMEM((N_HEADS, TILE), F32),
                        pltpu.VMEM((N_HEADS, TILE), F32),
                        pltpu.VMEM((ATT_GROUP, N_HEADS, TILE), F32),
                        pltpu.VMEM((ATT_GROUP, N_HEADS, TILE, TILE), BF16)],
        compiler_params=pltpu.CompilerParams(
            dimension_semantics=("arbitrary", "arbitrary"), vmem_limit_bytes=_vmem_limit(vmem)),
        name="attention",
    )(rel_bias, qt, k, vt, mask)


def _mix_body(tiles_per_seq, cur_ref, halo_ref, dw_ref, dwb_ref, lng_ref, lnb_ref, attn_ref, gates_ref, hres_ref,
              wo_ref, wpw_ref, wout_ref, g_ref, o_ref, h_ref, y_ref, conv_ref):
    s = pl.program_id(0)
    last = pl.num_programs(0) - 1

    def glu(x):
        return x[:, :CONV_CH].astype(F32) * jax.nn.sigmoid(x[:, CONV_CH:].astype(F32))

    def conv(slot):
        h_ref[0, :CONV_HALO, :] = jnp.where(s % tiles_per_seq != 0, glu(halo_ref[...]), 0.0)
        h_ref[0, CONV_HALO:, :] = glu(cur_ref[...])
        n_rows = MIX_TM + CONV_HALO
        for r in range(1, SUBLANES):
            h_ref[r, SUBLANES:, :] = h_ref[0, SUBLANES - r:n_rows - r, :]
        for rb in range(MIX_TM // CONV_RB):
            for c in range(CONV_CH // CONV_CB):
                cols = slice(c * CONV_CB, (c + 1) * CONV_CB)
                acc = jnp.broadcast_to(dwb_ref[:, cols], (CONV_RB, CONV_CB))
                for j in range(CONV_WIDTH):
                    groups, r = divmod(CONV_WIDTH - 1 - j, SUBLANES)
                    start = CONV_HALO - groups * SUBLANES + rb * CONV_RB
                    acc = acc + h_ref[r, start:start + CONV_RB, cols] * dw_ref[j:j + 1, cols]
                y_ref[rb * CONV_RB:(rb + 1) * CONV_RB, cols] = acc
        y = y_ref[...]
        mu = jnp.mean(y, axis=-1, keepdims=True)
        var = jnp.mean(jnp.square(y - mu), axis=-1, keepdims=True)
        z = (y - mu) * lax.rsqrt(var + EPS) * lng_ref[...] + lnb_ref[...]
        conv_ref[slot] = (z * jax.nn.sigmoid(z)).astype(BF16)

    def mix(slot):
        a = jnp.dot(attn_ref[...], wo_ref[...], preferred_element_type=F32)
        c = jnp.dot(conv_ref[slot], wpw_ref[...], preferred_element_type=F32)
        mixed = (gates_ref[:, :D_MODEL].astype(F32) * a + gates_ref[:, D_MODEL:].astype(F32) * c).astype(BF16)
        y = jnp.dot(mixed, wout_ref[...], preferred_element_type=F32)
        o_ref[...] = hres_ref[...] + _rms(y, g_ref[...])

    @pl.when(s == 0)
    def _():
        conv(0)

    @pl.when((s > 0) & (s < last))
    def _():
        mix((s + 1) % 2)
        conv(s % 2)

    @pl.when(s == last)
    def _():
        mix((s + 1) % 2)


def _mix_out(glu_in, attn, gates, h, dw, dw_b, ln_g, ln_b, w_o, w_pw2, w_out, post_g, seq):
    t = h.shape[0]
    nt = t // MIX_TM
    halo_per_tile = MIX_TM // CONV_HALO
    resident = pl.Buffered(1)

    def ahead(n):
        return pl.BlockSpec((MIX_TM, n), lambda s: (jnp.minimum(s, nt - 1), 0))

    def behind(n):
        return pl.BlockSpec((MIX_TM, n), lambda s: (jnp.maximum(s - 1, 0), 0))

    def whole(a):
        return pl.BlockSpec(a.shape, lambda s: (0, 0), pipeline_mode=resident)

    vec = pl.BlockSpec((1, CONV_CH), lambda s: (0, 0))
    vmem = ((w_o.size + w_pw2.size + w_out.size) * 2
            + 2 * MIX_TM * (2 * CONV_CH * 2 + ATTN_DIM * 2 + 2 * D_MODEL * 2 + 2 * D_MODEL * 4)
            + SUBLANES * (MIX_TM + CONV_HALO) * CONV_CH * 4 + MIX_TM * CONV_CH * (4 + 2 * 2)
            + 6 * MIX_TM * D_MODEL * 4)
    return pl.pallas_call(
        functools.partial(_mix_body, seq // MIX_TM),
        grid=(nt + 1,),
        in_specs=[ahead(2 * CONV_CH),
                  pl.BlockSpec((CONV_HALO, 2 * CONV_CH),
                               lambda s: (jnp.maximum(jnp.minimum(s, nt - 1) * halo_per_tile - 1, 0), 0)),
                  pl.BlockSpec((CONV_WIDTH, CONV_CH), lambda s: (0, 0)), vec, vec, vec,
                  behind(ATTN_DIM), behind(2 * D_MODEL), behind(D_MODEL),
                  whole(w_o), whole(w_pw2), whole(w_out), pl.BlockSpec((1, D_MODEL), lambda s: (0, 0))],
        out_specs=behind(D_MODEL),
        out_shape=jax.ShapeDtypeStruct((t, D_MODEL), F32),
        scratch_shapes=[pltpu.VMEM((SUBLANES, MIX_TM + CONV_HALO, CONV_CH), F32),
                        pltpu.VMEM((MIX_TM, CONV_CH), F32),
                        pltpu.VMEM((2, MIX_TM, CONV_CH), BF16)],
        compiler_params=pltpu.CompilerParams(
            dimension_semantics=("arbitrary",), vmem_limit_bytes=_vmem_limit(vmem)),
        name="mix_out",
    )(glu_in, glu_in, dw, dw_b, ln_g, ln_b, attn, gates, h, w_o, w_pw2, w_out, post_g)


def kernel(x, rel_bias, ffn1_pre_g, ffn1_wg, ffn1_wu, ffn1_wd, ffn1_post_g, mix_pre_g, w_in, b_gate, w_o,
           conv_dw, conv_dw_b, conv_ln_g, conv_ln_b, w_pw2, w_out, mix_post_g, ffn2_pre_g, ffn2_wg, ffn2_wu,
           ffn2_wd, ffn2_post_g):
    batch, seq, d = x.shape
    depth = ffn1_wg.shape[0]
    h = x.reshape(batch * seq, d)
    o_k = ATTN_DIM
    o_v = o_k + ATTN_DIM
    o_qi = o_v + ATTN_DIM
    o_ki = o_qi + IDX_HEADS * IDX_DIM
    o_wi = o_ki + IDX_DIM
    o_glu = o_wi + IDX_HEADS
    o_gates = o_glu + 2 * CONV_CH
    for l in range(depth):
        wl = w_in[l]
        w_t = jnp.concatenate([_weight_t(wl, 0, o_k, HEAD_DIM ** -0.5 * LOG2E),
                               _weight_t(wl, o_qi, o_ki - o_qi)], axis=0)
        w_vt = _weight_t(wl, o_v, o_qi - o_v)
        w_wt = wl[:, o_wi:o_glu].T.astype(BF16)
        w_k = jnp.concatenate([wl[:, o_k:o_v], wl[:, o_ki:o_wi],
                               jnp.zeros((d, LANES - IDX_DIM), F32)], axis=1).astype(BF16)
        w_glu = wl[:, o_glu:o_gates].astype(BF16)
        w_gates = wl[:, o_gates:].astype(BF16)

        h, u = _ffn(h, ffn1_pre_g[l][None], ffn1_wg[l].astype(BF16), ffn1_wu[l].astype(BF16),
                    ffn1_wd[l].astype(BF16), ffn1_post_g[l][None], mix_pre_g[l][None])

        qt, vt, wt = _proj_t(u, w_t, w_vt, w_wt)
        k = _proj(u, w_k, BF16, "proj_k")
        glu_in = _proj(u, w_glu, BF16, "proj_glu")
        gates = _proj(u, w_gates, BF16, "proj_gates", bias=b_gate[l][None])

        mask = _indexer(qt, wt, k, batch, seq)
        attn = _attention(qt, k, vt, mask, rel_bias, batch, seq)
        h = _mix_out(glu_in, attn, gates, h, conv_dw[l], conv_dw_b[l][None], conv_ln_g[l][None],
                     conv_ln_b[l][None], w_o[l].astype(BF16), w_pw2[l].astype(BF16), w_out[l].astype(BF16),
                     mix_post_g[l][None], seq)

        h = _ffn(h, ffn2_pre_g[l][None], ffn2_wg[l].astype(BF16), ffn2_wu[l].astype(BF16),
                 ffn2_wd[l].astype(BF16), ffn2_post_g[l][None])
    return h.reshape(batch, seq, d)
```
